```python
import jax, jax.numpy as jnp
from jax import lax
import numpy as np

D_MODEL = 1024
BATCH = 8
SEQ = 2048
DEPTH = 4

HEAD_DIM = 64
Q_BLOCK = 128
ROPE_THETA = 10000.0
NORM_EPS = 1e-6
FOX_HEADS = 6
FOX_WIDTH = FOX_HEADS * HEAD_DIM
FORGET_BIAS_LO = 1.0
FORGET_BIAS_HI = 4.0
DIL_HEADS = 6
DIL_WIDTH = DIL_HEADS * HEAD_DIM
DIL_PAIRS = ((128, 1), (512, 4), (2048, 16))
DIL_BLOCK = 128
MLA_HEADS = 4
MLA_Q_RANK = 256
MLA_KV_RANK = 256
MLA_NOPE_DIM = 64
MLA_ROPE_DIM = 32
MLA_V_DIM = 64
MLA_WIDTH = MLA_HEADS * MLA_V_DIM
N_BRANCHES = 3
IN_SIZES = (FOX_WIDTH, FOX_WIDTH, FOX_WIDTH, FOX_HEADS,
            DIL_WIDTH, DIL_WIDTH, DIL_WIDTH,
            MLA_Q_RANK, MLA_KV_RANK, MLA_ROPE_DIM,
            N_BRANCHES * D_MODEL)
IN_WIDTH = sum(IN_SIZES)
D_FF = 2816
N_EXPERTS = 8
TOP_K = 2
D_FF_EXPERT = 1408
N_DENSE = (DEPTH + 1) // 2
N_MOE = DEPTH // 2
PLE_DIM = 256
MAX_POS_OFFSET = 1024

kernel_name = "hybrid_fox_dilated_mla_moe_trunk"


def rmsnorm(x, gain):
    xf = x.astype(jnp.float32)
    y = xf * lax.rsqrt(jnp.mean(xf * xf, axis=-1, keepdims=True) + NORM_EPS)
    return (y * gain.astype(jnp.float32)).astype(x.dtype)


def rope(x, positions):
    half = x.shape[-1] // 2
    inv_freq = ROPE_THETA ** (-jnp.arange(half, dtype=jnp.float32) / half)
    ang = positions.astype(jnp.float32)[:, None, :, None] * inv_freq
    cos, sin = jnp.cos(ang), jnp.sin(ang)
    xf = x.astype(jnp.float32)
    x1, x2 = xf[..., :half], xf[..., half:]
    return jnp.concatenate([x1 * cos - x2 * sin, x2 * cos + x1 * sin], axis=-1).astype(x.dtype)


def to_heads(a, n_heads):
    b, t, _ = a.shape
    return a.reshape(b, t, n_heads, -1).transpose(0, 2, 1, 3)


def from_heads(a):
    b, h, t, d = a.shape
    return a.transpose(0, 2, 1, 3).reshape(b, t, h * d)


def causal_block_attention(q, k, v, scale, decay=None):
    b, h, t, dk = q.shape
    nb = t // Q_BLOCK
    qb = q.reshape(b, h, nb, Q_BLOCK, dk).transpose(2, 0, 1, 3, 4)
    starts = jnp.arange(nb, dtype=jnp.int32) * Q_BLOCK
    k_pos = jnp.arange(t, dtype=jnp.int32)
    if decay is None:
        xs = (qb, starts)
    else:
        xs = (qb, starts, decay.reshape(b, h, nb, Q_BLOCK).transpose(2, 0, 1, 3))

    def one_block(xs_i):
        qi, start = xs_i[0], xs_i[1]
        s = jnp.einsum('bhqd,bhkd->bhqk', qi, k).astype(jnp.float32) * scale
        if decay is not None:
            s = s + (xs_i[2][..., :, None] - decay[:, :, None, :])
        q_pos = start + jnp.arange(Q_BLOCK, dtype=jnp.int32)
        s = jnp.where(k_pos[None, :] <= q_pos[:, None], s, -jnp.inf)
        pr = jax.nn.softmax(s, axis=-1)
        return jnp.einsum('bhqk,bhkd->bhqd', pr.astype(v.dtype), v)

    out = lax.map(one_block, xs)
    return out.transpose(1, 2, 0, 3, 4).reshape(b, h, t, v.shape[-1])


def dilated_branch(q, k, v, window, rate, scale):
    b, h, t, d = q.shape
    span = window // rate
    chunk = rate * DIL_BLOCK
    t_pad = -(-t // chunk) * chunk
    length = t_pad // rate
    nb = length // DIL_BLOCK

    def blocks(a):
        a = jnp.pad(a, ((0, 0), (0, 0), (0, t_pad - t), (0, 0)))
        a = a.reshape(b, h, length, rate, d).transpose(0, 1, 3, 2, 4)
        return a.reshape(b, h, rate, nb, DIL_BLOCK, d)

    def with_prev(a):
        prev = jnp.concatenate([jnp.zeros_like(a[:, :, :, :1]), a[:, :, :, :-1]], axis=3)
        return jnp.concatenate([prev, a], axis=4)

    qb = blocks(q)
    kb = with_prev(blocks(k))
    vb = with_prev(blocks(v))
    s = jnp.einsum('bhrnqd,bhrnkd->bhrnqk', qb, kb).astype(jnp.float32) * scale
    qi = jnp.arange(DIL_BLOCK)[:, None]
    ki = jnp.arange(2 * DIL_BLOCK)[None, :]
    dist = qi + DIL_BLOCK - ki
    band = (dist >= 0) & (dist <= span)
    has_prev = (jnp.arange(nb) > 0)[:, None, None] | (ki >= DIL_BLOCK)[None]
    valid = band[None] & has_prev
    s = jnp.where(valid, s, -jnp.inf)
    lse = jax.nn.logsumexp(s, axis=-1)
    pr = jnp.exp(s - lse[..., None])
    o = jnp.einsum('bhrnqk,bhrnkd->bhrnqd', pr.astype(v.dtype), vb)
    o = o.reshape(b, h, rate, length, d).transpose(0, 1, 3, 2, 4).reshape(b, h, t_pad, d)[:, :, :t]
    lse = lse.reshape(b, h, rate, length).transpose(0, 1, 3, 2).reshape(b, h, t_pad)[:, :, :t]
    return o, lse


def dilated_attention(q, k, v, scale):
    results = [dilated_branch(q, k, v, w, r, scale) for (w, r) in DIL_PAIRS]
    outs = jnp.stack([o for o, _ in results], axis=0)
    lses = jnp.stack([l for _, l in results], axis=0)
    alpha = jax.nn.softmax(lses, axis=0)
    return jnp.einsum('gbht,gbhtd->bhtd', alpha.astype(v.dtype), outs)


def swiglu(h, w_gate, w_up, w_down):
    return (jax.nn.silu(h @ w_gate) * (h @ w_up)) @ w_down


def moe_swiglu(h, w_router, w_gate, w_up, w_down):
    logits = (h @ w_router).astype(jnp.float32)
    top_val, top_idx = lax.top_k(logits, TOP_K)
    top_w = jax.nn.softmax(top_val, axis=-1)
    combine = jnp.sum(jax.nn.one_hot(top_idx, N_EXPERTS, dtype=jnp.float32) * top_w[..., None], axis=-2)
    combine = combine.astype(h.dtype)
    y = jnp.zeros_like(h)
    for e in range(N_EXPERTS):
        y = y + combine[..., e:e + 1] * swiglu(h, w_gate[e], w_up[e], w_down[e])
    return y


def setup_inputs(seed: int = 0) -> dict:
    key = jax.random.key(seed)
    ks = iter(jax.random.split(key, 32))
    f32 = jnp.float32

    def normal(shape, fan_in):
        return jax.random.normal(next(ks), shape, f32) * (fan_in ** -0.5)

    def gain(shape):
        return 1.0 + 0.05 * jax.random.normal(next(ks), shape, f32)

    x = jax.random.normal(next(ks), (BATCH, SEQ, D_MODEL), f32)
    p = jax.random.normal(next(ks), (DEPTH, BATCH, SEQ, PLE_DIM), f32)
    offsets = jax.random.randint(next(ks), (BATCH, 1), 0, MAX_POS_OFFSET, dtype=jnp.int32)
    positions = offsets + jnp.arange(SEQ, dtype=jnp.int32)[None, :]
    return {
        'x': x,
        'p': p,
        'positions': positions,
        'mix_norm': gain((DEPTH, D_MODEL)),
        'w_in': normal((DEPTH, D_MODEL, IN_WIDTH), D_MODEL),
        'b_forget': jax.random.uniform(next(ks), (DEPTH, FOX_HEADS), f32, FORGET_BIAS_LO, FORGET_BIAS_HI),
        'mla_q_norm': gain((DEPTH, MLA_Q_RANK)),
        'mla_kv_norm': gain((DEPTH, MLA_KV_RANK)),
        'w_uq': normal((DEPTH, MLA_Q_RANK, MLA_HEADS * (MLA_NOPE_DIM + MLA_ROPE_DIM)), MLA_Q_RANK),
        'w_ukv': normal((DEPTH, MLA_KV_RANK, MLA_HEADS * (MLA_NOPE_DIM + MLA_V_DIM)), MLA_KV_RANK),
        'w_br_fox': normal((DEPTH, FOX_WIDTH, D_MODEL), FOX_WIDTH),
        'w_br_dil': normal((DEPTH, DIL_WIDTH, D_MODEL), DIL_WIDTH),
        'w_br_mla': normal((DEPTH, MLA_WIDTH, D_MODEL), MLA_WIDTH),
        'w_out': normal((DEPTH, D_MODEL, D_MODEL), D_MODEL),
        'ffn_norm': gain((DEPTH, D_MODEL)),
        'w_ffn_gate': normal((N_DENSE, D_MODEL, D_FF), D_MODEL),
        'w_ffn_up': normal((N_DENSE, D_MODEL, D_FF), D_MODEL),
        'w_ffn_down': normal((N_DENSE, D_FF, D_MODEL), D_FF),
        'w_router': normal((N_MOE, D_MODEL, N_EXPERTS), D_MODEL),
        'w_exp_gate': normal((N_MOE, N_EXPERTS, D_MODEL, D_FF_EXPERT), D_MODEL),
        'w_exp_up': normal((N_MOE, N_EXPERTS, D_MODEL, D_FF_EXPERT), D_MODEL),
        'w_exp_down': normal((N_MOE, N_EXPERTS, D_FF_EXPERT, D_MODEL), D_FF_EXPERT),
        'ple_norm': gain((DEPTH, D_MODEL)),
        'w_ple_gate': normal((DEPTH, D_MODEL, D_MODEL), D_MODEL),
        'w_ple_proj': normal((DEPTH, PLE_DIM, D_MODEL), PLE_DIM),
        'final_norm': gain((D_MODEL,)),
    }


def reference(x, p, positions, mix_norm, w_in, b_forget, mla_q_norm, mla_kv_norm, w_uq, w_ukv,
              w_br_fox, w_br_dil, w_br_mla, w_out, ffn_norm, w_ffn_gate, w_ffn_up, w_ffn_down,
              w_router, w_exp_gate, w_exp_up, w_exp_down, ple_norm, w_ple_gate, w_ple_proj,
              final_norm):
    b, t, _ = x.shape
    split_at = np.cumsum(IN_SIZES)[:-1].tolist()
    head_scale = HEAD_DIM ** -0.5
    mla_scale = (MLA_NOPE_DIM + MLA_ROPE_DIM) ** -0.5
    for i in range(DEPTH):
        h = rmsnorm(x, mix_norm[i])
        proj = h @ w_in[i]
        (fq, fk, fv, fz, dq, dk, dv, cq, ckv, kr, gates) = jnp.split(proj, split_at, axis=-1)

        log_f = jax.nn.log_sigmoid(fz.astype(jnp.float32) + b_forget[i].astype(jnp.float32))
        decay = jnp.cumsum(log_f, axis=1).transpose(0, 2, 1)
        o_fox = causal_block_attention(to_heads(fq, FOX_HEADS), to_heads(fk, FOX_HEADS),
                                       to_heads(fv, FOX_HEADS), head_scale, decay)

        o_dil = dilated_attention(rope(to_heads(dq, DIL_HEADS), positions),
                                  rope(to_heads(dk, DIL_HEADS), positions),
                                  to_heads(dv, DIL_HEADS), head_scale)

        q_c = to_heads(rmsnorm(cq, mla_q_norm[i]) @ w_uq[i], MLA_HEADS)
        q_c = jnp.concatenate([q_c[..., :MLA_NOPE_DIM], rope(q_c[..., MLA_NOPE_DIM:], positions)], axis=-1)
        kv_c = to_heads(rmsnorm(ckv, mla_kv_norm[i]) @ w_ukv[i], MLA_HEADS)
        k_pe = rope(kr[:, None], positions)
        k_c = jnp.concatenate([kv_c[..., :MLA_NOPE_DIM],
                               jnp.broadcast_to(k_pe, (b, MLA_HEADS, t, MLA_ROPE_DIM))], axis=-1)
        o_mla = causal_block_attention(q_c, k_c, kv_c[..., MLA_NOPE_DIM:], mla_scale)

        g_fox, g_dil, g_mla = jnp.split(jax.nn.sigmoid(gates), N_BRANCHES, axis=-1)
        merged = (g_fox * (from_heads(o_fox) @ w_br_fox[i])
                  + g_dil * (from_heads(o_dil) @ w_br_dil[i])
                  + g_mla * (from_heads(o_mla) @ w_br_mla[i]))
        x = x + merged @ w_out[i]

        h = rmsnorm(x, ffn_norm[i])
        j = i // 2
        if i % 2 == 0:
            x = x + swiglu(h, w_ffn_gate[j], w_ffn_up[j], w_ffn_down[j])
        else:
            x = x + moe_swiglu(h, w_router[j], w_exp_gate[j], w_exp_up[j], w_exp_down[j])

        g = jax.nn.sigmoid(rmsnorm(x, ple_norm[i]) @ w_ple_gate[i])
        x = x + g * (p[i] @ w_ple_proj[i])
    return rmsnorm(x, final_norm)
```

```python
import functools

import jax
import jax.numpy as jnp
from jax import lax
from jax.experimental import pallas as pl
from jax.experimental.pallas import tpu as pltpu

F32 = jnp.float32
BF16 = jnp.bfloat16

D_MODEL = 1024
HEAD_DIM = 64
ROPE_THETA = 10000.0
NORM_EPS = 1e-6
FOX_HEADS = 6
FOX_WIDTH = FOX_HEADS * HEAD_DIM
DIL_HEADS = 6
DIL_WIDTH = DIL_HEADS * HEAD_DIM
DIL_PAIRS = ((128, 1), (512, 4), (2048, 16))
DIL_BLOCK = 128
MLA_HEADS = 4
MLA_Q_RANK = 256
MLA_KV_RANK = 256
MLA_NOPE_DIM = 64
MLA_ROPE_DIM = 32
MLA_V_DIM = 64
MLA_WIDTH = MLA_HEADS * MLA_V_DIM
N_BRANCHES = 3
D_FF = 2816
N_EXPERTS = 8
D_FF_EXPERT = 1408
PLE_DIM = 256

LANES = 128
GATE_WIDTH = N_BRANCHES * D_MODEL
COL_GATES = 0
COL_FOX = GATE_WIDTH
COL_DIL = COL_FOX + 3 * FOX_WIDTH
COL_MLA = COL_DIL + 3 * DIL_WIDTH
MAIN_WIDTH = COL_MLA + MLA_Q_RANK + MLA_KV_RANK
SMALL_WIDTH = 2 * LANES
VMEM_LIMIT = 56 * 1024 * 1024

NEG_INF = float("-inf")


def _cparams(sem):
    return pltpu.CompilerParams(dimension_semantics=sem, vmem_limit_bytes=VMEM_LIMIT)


def _rms(x, gain):
    return x * lax.rsqrt(jnp.mean(x * x, axis=-1, keepdims=True) + NORM_EPS) * gain


def _dot(a, b):
    return jnp.dot(a, b, preferred_element_type=F32)


def _dot_nt(a, b):
    return lax.dot_general(a, b, (((1,), (1,)), ((), ())), preferred_element_type=F32)


def _lane_iota():
    return lax.broadcasted_iota(jnp.int32, (1, LANES), 1)


def _rope_tab_kernel(pos_ref, fd_ref, fm_ref, cd_ref, sd_ref, cm_ref, sm_ref):
    pos = pos_ref[...].astype(F32)
    lane = _lane_iota()
    ang = pos * fd_ref[...]
    cd_ref[...] = jnp.cos(ang)
    sd_ref[...] = jnp.sin(ang) * jnp.where((lane % HEAD_DIM) < HEAD_DIM // 2, -1.0, 1.0)
    ang = pos * fm_ref[...]
    cm_ref[...] = jnp.cos(ang)
    sm_ref[...] = jnp.sin(ang) * jnp.where((lane % MLA_ROPE_DIM) < MLA_ROPE_DIM // 2, -1.0, 1.0)


def _rope_tables(positions):
    b, t = positions.shape
    half_d = HEAD_DIM // 2
    half_m = MLA_ROPE_DIM // 2
    inv_d = ROPE_THETA ** (-jnp.arange(half_d, dtype=F32) / half_d)
    inv_m = ROPE_THETA ** (-jnp.arange(half_m, dtype=F32) / half_m)
    fd = jnp.tile(inv_d, LANES // half_d)[None, :]
    fm = jnp.tile(inv_m, LANES // half_m)[None, :]
    tab = jax.ShapeDtypeStruct((b, t, LANES), F32)
    tab_spec = pl.BlockSpec((None, t, LANES), lambda i: (i, 0, 0))
    vec_spec = pl.BlockSpec((1, LANES), lambda i: (0, 0))
    return pl.pallas_call(
        _rope_tab_kernel,
        grid=(b,),
        in_specs=[pl.BlockSpec((None, t, 1), lambda i: (i, 0, 0)), vec_spec, vec_spec],
        out_specs=[tab_spec] * 4,
        out_shape=[tab] * 4,
        name="rope_tables",
        compiler_params=_cparams(("parallel",)),
    )(positions.reshape(b, t, 1), fd, fm)


def _swap_halves(x, width):
    lane = _lane_iota()
    half = width // 2
    return jnp.where((lane % width) < half,
                     pltpu.roll(x, LANES - half, axis=1),
                     pltpu.roll(x, half, axis=1))


def _rope(x, cos, sin_signed, width):
    return x * cos + _swap_halves(x, width) * sin_signed


def _col_chunks(width, step=512):
    return [(c, min(step, width - c)) for c in range(0, width, step)]


def _inproj_kernel(x_ref, g_ref, w_ref, ws_ref, o_ref, os_ref):
    h = _rms(x_ref[...], g_ref[...]).astype(BF16)
    for c, n in _col_chunks(MAIN_WIDTH):
        o_ref[:, c:c + n] = _dot(h, w_ref[:, c:c + n]).astype(BF16)
    os_ref[...] = _dot(h, ws_ref[...])


def _inproj(x2, gain, w_main, w_small, tm=512):
    n = x2.shape[0]
    return pl.pallas_call(
        _inproj_kernel,
        grid=(n // tm,),
        in_specs=[
            pl.BlockSpec((tm, D_MODEL), lambda i: (i, 0)),
            pl.BlockSpec((1, D_MODEL), lambda i: (0, 0)),
            pl.BlockSpec((D_MODEL, MAIN_WIDTH), lambda i: (0, 0)),
            pl.BlockSpec((D_MODEL, SMALL_WIDTH), lambda i: (0, 0)),
        ],
        out_specs=[
            pl.BlockSpec((tm, MAIN_WIDTH), lambda i: (i, 0)),
            pl.BlockSpec((tm, SMALL_WIDTH), lambda i: (i, 0)),
        ],
        out_shape=[
            jax.ShapeDtypeStruct((n, MAIN_WIDTH), BF16),
            jax.ShapeDtypeStruct((n, SMALL_WIDTH), F32),
        ],
        name="inproj",
        compiler_params=_cparams(("parallel",)),
    )(x2, gain, w_main, w_small)


CUMSUM_BLOCK = 256


def _decay_kernel(z_ref, b_ref, o_ref):
    t = z_ref.shape[0]
    row = lax.broadcasted_iota(jnp.int32, (CUMSUM_BLOCK, CUMSUM_BLOCK), 0)
    col = lax.broadcasted_iota(jnp.int32, (CUMSUM_BLOCK, CUMSUM_BLOCK), 1)
    tri = jnp.where(col <= row, 1.0, 0.0).astype(BF16)
    carry = jnp.zeros((1, LANES), F32)
    for blk in range(t // CUMSUM_BLOCK):
        sl = slice(blk * CUMSUM_BLOCK, (blk + 1) * CUMSUM_BLOCK)
        z = z_ref[sl, :] + b_ref[...]
        logf = -(jnp.maximum(-z, 0.0) + jnp.log1p(jnp.exp(-jnp.abs(z))))
        hi = logf.astype(BF16)
        r1 = logf - hi.astype(F32)
        mid = r1.astype(BF16)
        lo = (r1 - mid.astype(F32)).astype(BF16)
        cs = _dot(tri, hi) + _dot(tri, mid) + _dot(tri, lo) + carry
        o_ref[sl, :] = cs
        carry = cs[CUMSUM_BLOCK - 1:CUMSUM_BLOCK, :]


def _decay(small3, bias):
    b, t, _ = small3.shape
    return pl.pallas_call(
        _decay_kernel,
        grid=(b,),
        in_specs=[pl.BlockSpec((None, t, LANES), lambda i: (i, 0, 0)),
                  pl.BlockSpec((1, LANES), lambda i: (0, 0))],
        out_specs=pl.BlockSpec((None, t, LANES), lambda i: (i, 0, 0)),
        out_shape=jax.ShapeDtypeStruct((b, t, LANES), F32),
        name="decay",
        compiler_params=_cparams(("parallel",)),
    )(small3, bias)


def _flash_init(m_sc, l_sc, acc_sc):
    m_sc[...] = jnp.full(m_sc.shape, NEG_INF, F32)
    l_sc[...] = jnp.zeros(l_sc.shape, F32)
    acc_sc[...] = jnp.zeros(acc_sc.shape, F32)


def _flash_update(s_heads, v, m_sc, l_sc, acc_sc):
    tq = s_heads[0].shape[0]
    ps = []
    alphas = []
    for h, s in enumerate(s_heads):
        m_prev = m_sc[h]
        m_new = jnp.maximum(m_prev, jnp.max(s, axis=1, keepdims=True))
        alpha = jnp.exp(m_prev - m_new)
        p = jnp.exp(s - m_new)
        l_sc[h] = alpha * l_sc[h] + jnp.sum(p, axis=1, keepdims=True)
        m_sc[h] = m_new
        ps.append(p.astype(BF16))
        alphas.append(alpha)
    pv = _dot(jnp.concatenate(ps, axis=0), v)
    for h in range(2):
        acc_sc[h] = alphas[h] * acc_sc[h] + pv[h * tq:(h + 1) * tq]


def _flash_finish(o_ref, l_sc, acc_sc):
    lane = _lane_iota()
    o0 = acc_sc[0] / l_sc[0]
    o1 = acc_sc[1] / l_sc[1]
    o_ref[...] = jnp.where(lane < HEAD_DIM, o0, o1).astype(o_ref.dtype)


def _causal_ok(qi, ki, tq, tk):
    row = lax.broadcasted_iota(jnp.int32, (tq, tk), 0) + qi * tq
    col = lax.broadcasted_iota(jnp.int32, (tq, tk), 1) + ki * tk
    return col <= row


def _fox_kernel(q_ref, k_ref, v_ref, d_ref, o_ref, m_sc, l_sc, acc_sc, *, tq, tk):
    qi = pl.program_id(2)
    ki = pl.program_id(3)

    @pl.when(ki == 0)
    def _():
        _flash_init(m_sc, l_sc, acc_sc)

    @pl.when(ki <= qi)
    def _():
        q = q_ref[...]
        lane = _lane_iota()
        zero = jnp.zeros_like(q)
        qs = jnp.concatenate([jnp.where(lane < HEAD_DIM, q, zero),
                              jnp.where(lane >= HEAD_DIM, q, zero)], axis=0)
        s = _dot_nt(qs, k_ref[...])
        ok = _causal_ok(qi, ki, tq, tk)
        s_heads = [jnp.where(ok, s[h * tq:(h + 1) * tq] - d_ref[h:h + 1, :], NEG_INF)
                   for h in range(2)]
        _flash_update(s_heads, v_ref[...], m_sc, l_sc, acc_sc)

    @pl.when(ki == pl.num_programs(3) - 1)
    def _():
        _flash_finish(o_ref, l_sc, acc_sc)


def _flash_scratch(tq):
    return [pltpu.VMEM((2, tq, 1), F32), pltpu.VMEM((2, tq, 1), F32),
            pltpu.VMEM((2, tq, LANES), F32)]


def _fox_attention(main3, decay_pairs, tq=512):
    b, t, _ = main3.shape
    tk = tq
    nq = t // tq
    cb = COL_FOX // LANES
    npair = FOX_HEADS // 2

    def kv_map(off):
        return lambda bi, p, qi, ki: (bi, jnp.minimum(ki, qi), cb + off + p)

    return pl.pallas_call(
        functools.partial(_fox_kernel, tq=tq, tk=tk),
        grid=(b, npair, nq, nq),
        in_specs=[
            pl.BlockSpec((None, tq, LANES), lambda bi, p, qi, ki: (bi, qi, cb + p)),
            pl.BlockSpec((None, tk, LANES), kv_map(npair)),
            pl.BlockSpec((None, tk, LANES), kv_map(2 * npair)),
            pl.BlockSpec((None, None, 2, tk), lambda bi, p, qi, ki: (bi, p, 0, jnp.minimum(ki, qi))),
        ],
        out_specs=pl.BlockSpec((None, tq, LANES), lambda bi, p, qi, ki: (bi, qi, p)),
        out_shape=jax.ShapeDtypeStruct((b, t, FOX_WIDTH), BF16),
        name="fox_attention",
        scratch_shapes=_flash_scratch(tq),
        compiler_params=_cparams(("parallel", "parallel", "parallel", "arbitrary")),
    )(main3, main3, main3, decay_pairs)


def _mla_pre_kernel(cq_ref, ckv_ref, kr_ref, cos_ref, sin_ref, gq_ref, gkv_ref, wq_ref, wkv_ref,
                    qn_ref, qr_ref, kn_ref, kr_out_ref, v_ref, *, scale):
    cos = cos_ref[...]
    sin = sin_ref[...]
    nope = MLA_HEADS * MLA_NOPE_DIM
    hq = _rms(cq_ref[...].astype(F32), gq_ref[...]).astype(BF16)
    q = _dot(hq, wq_ref[...])
    qn_ref[...] = (q[:, :nope] * scale).astype(BF16)
    qr_ref[...] = (_rope(q[:, nope:], cos, sin, MLA_ROPE_DIM) * scale).astype(BF16)
    hkv = _rms(ckv_ref[...].astype(F32), gkv_ref[...]).astype(BF16)
    kv = _dot(hkv, wkv_ref[...])
    kn_ref[...] = kv[:, :nope].astype(BF16)
    v_ref[...] = kv[:, nope:].astype(BF16)
    kr_out_ref[...] = _rope(kr_ref[...], cos, sin, MLA_ROPE_DIM).astype(BF16)


def _mla_pre(main3, small3, cos_m, sin_m, gq, gkv, wq, wkv, tm=512):
    b, t, _ = main3.shape
    cq_blk = COL_MLA // MLA_Q_RANK
    nope = MLA_HEADS * MLA_NOPE_DIM
    vw = MLA_HEADS * MLA_V_DIM

    def tok(width, col=0):
        return pl.BlockSpec((None, tm, width), lambda bi, i: (bi, i, col))

    def full(shape):
        return pl.BlockSpec(shape, lambda bi, i: (0,) * len(shape))

    return pl.pallas_call(
        functools.partial(_mla_pre_kernel, scale=(MLA_NOPE_DIM + MLA_ROPE_DIM) ** -0.5),
        grid=(b, t // tm),
        in_specs=[tok(MLA_Q_RANK, cq_blk), tok(MLA_KV_RANK, cq_blk + 1), tok(LANES, 1),
                  tok(LANES), tok(LANES), full((1, MLA_Q_RANK)), full((1, MLA_KV_RANK)),
                  full(wq.shape), full(wkv.shape)],
        out_specs=[tok(nope), tok(LANES), tok(nope), tok(LANES), tok(vw)],
        out_shape=[jax.ShapeDtypeStruct((b, t, nope), BF16),
                   jax.ShapeDtypeStruct((b, t, LANES), BF16),
                   jax.ShapeDtypeStruct((b, t, nope), BF16),
                   jax.ShapeDtypeStruct((b, t, LANES), BF16),
                   jax.ShapeDtypeStruct((b, t, vw), BF16)],
        name="mla_pre",
        compiler_params=_cparams(("parallel", "parallel")),
    )(main3, main3, small3, cos_m, sin_m, gq, gkv, wq, wkv)


def _mla_kernel(qn_ref, qr_ref, kn_ref, kr_ref, v_ref, o_ref, m_sc, l_sc, acc_sc, *, tq, tk):
    p = pl.program_id(1)
    qi = pl.program_id(2)
    ki = pl.program_id(3)

    @pl.when(ki == 0)
    def _():
        _flash_init(m_sc, l_sc, acc_sc)

    @pl.when(ki <= qi)
    def _():
        qn = qn_ref[...]
        qr = qr_ref[...]
        lane = _lane_iota()
        zero = jnp.zeros_like(qn)
        rows = []
        for h in range(2):
            nope_h = jnp.where((lane // MLA_NOPE_DIM) == h, qn, zero)
            rope_h = jnp.where((lane // MLA_ROPE_DIM) == 2 * p + h, qr, zero)
            rows.append(jnp.concatenate([nope_h, rope_h], axis=1))
        qs = jnp.concatenate(rows, axis=0)
        kk = jnp.concatenate([kn_ref[...], kr_ref[...]], axis=1)
        s = _dot_nt(qs, kk)
        ok = _causal_ok(qi, ki, tq, tk)
        s_heads = [jnp.where(ok, s[h * tq:(h + 1) * tq], NEG_INF) for h in range(2)]
        _flash_update(s_heads, v_ref[...], m_sc, l_sc, acc_sc)

    @pl.when(ki == pl.num_programs(3) - 1)
    def _():
        _flash_finish(o_ref, l_sc, acc_sc)


def _mla_attention(qn, qr, kn, kr4, vv, tq=512):
    b, t, _ = qn.shape
    tk = tq
    nq = t // tq
    npair = MLA_HEADS // 2
    q_pair = pl.BlockSpec((None, tq, LANES), lambda bi, p, qi, ki: (bi, qi, p))
    q_all = pl.BlockSpec((None, tq, LANES), lambda bi, p, qi, ki: (bi, qi, 0))
    k_pair = pl.BlockSpec((None, tk, LANES), lambda bi, p, qi, ki: (bi, jnp.minimum(ki, qi), p))
    k_all = pl.BlockSpec((None, tk, LANES), lambda bi, p, qi, ki: (bi, jnp.minimum(ki, qi), 0))
    return pl.pallas_call(
        functools.partial(_mla_kernel, tq=tq, tk=tk),
        grid=(b, npair, nq, nq),
        in_specs=[q_pair, q_all, k_pair, k_all, k_pair],
        out_specs=q_pair,
        out_shape=jax.ShapeDtypeStruct((b, t, MLA_WIDTH), BF16),
        name="mla_attention",
        scratch_shapes=_flash_scratch(tq),
        compiler_params=_cparams(("parallel", "parallel", "parallel", "arbitrary")),
    )(qn, qr, kn, kr4, vv)


def _dil_kernel(q_ref, k_ref, v_ref, cos_ref, sin_ref, o_ref,
                qf, kf, vf, qb, kb, vb, accb, mb, lb, tacc, tmx, tl):
    t = q_ref.shape[0]
    blk = DIL_BLOCK
    lane = _lane_iota()
    head0 = lane < HEAD_DIM
    cos = cos_ref[...]
    sin = sin_ref[...]
    qf[...] = _rope(q_ref[...].astype(F32), cos, sin, HEAD_DIM)
    kf[...] = _rope(k_ref[...].astype(F32), cos, sin, HEAD_DIM)
    vf[...] = v_ref[...].astype(F32)

    for g, (_, rate) in enumerate(DIL_PAIRS):
        length = t // rate
        kb[g, 0:blk, :] = jnp.zeros((blk, LANES), BF16)
        vb[g, 0:blk, :] = jnp.zeros((blk, LANES), BF16)
        for res in range(rate):
            dst = slice(blk + res * length, blk + (res + 1) * length)
            if rate == 1:
                src = slice(None)
            else:
                src = pl.ds(res, length, stride=rate)
            qb[g, dst, :] = qf[src, :].astype(BF16)
            kb[g, dst, :] = kf[src, :].astype(BF16)
            vb[g, dst, :] = vf[src, :].astype(BF16)

    qi2 = lax.broadcasted_iota(jnp.int32, (blk, 2 * blk), 0)
    kj2 = lax.broadcasted_iota(jnp.int32, (blk, 2 * blk), 1)
    band = (kj2 >= qi2) & (kj2 <= qi2 + blk)
    bias_full = jnp.where(band, 0.0, NEG_INF)
    bias_first = jnp.where(band & (kj2 >= blk), 0.0, NEG_INF)
    qi1 = lax.broadcasted_iota(jnp.int32, (blk, blk), 0)
    kj1 = lax.broadcasted_iota(jnp.int32, (blk, blk), 1)
    bias_single = jnp.where(kj1 <= qi1, 0.0, NEG_INF)

    for g, (window, rate) in enumerate(DIL_PAIRS):
        assert window // rate == blk
        nb = t // rate // blk
        nblocks = t // blk

        def body(j, carry, g=g, nb=nb):
            base = pl.multiple_of(j * blk, blk)
            q = qb[g, pl.ds(base + blk, blk), :]
            zero = jnp.zeros_like(q)
            qs = jnp.concatenate([jnp.where(head0, q, zero), jnp.where(head0, zero, q)], axis=0)
            if nb == 1:
                keys = kb[g, pl.ds(base + blk, blk), :]
                vals = vb[g, pl.ds(base + blk, blk), :]
                bias = bias_single
            else:
                keys = kb[g, pl.ds(base, 2 * blk), :]
                vals = vb[g, pl.ds(base, 2 * blk), :]
                bias = jnp.where((j % nb) != 0, bias_full, bias_first)
            s = _dot_nt(qs, keys)
            ps, ms, ls = [], [], []
            for h in range(2):
                sh = s[h * blk:(h + 1) * blk] + bias
                m = jnp.max(sh, axis=1, keepdims=True)
                p = jnp.exp(sh - m)
                ps.append(p.astype(BF16))
                ms.append(m)
                ls.append(jnp.sum(p, axis=1, keepdims=True))
            pv = _dot(jnp.concatenate(ps, axis=0), vals)
            rows = pl.ds(base, blk)
            accb[g, rows, :] = jnp.where(head0, pv[:blk], pv[blk:])
            mb[g, rows, :] = jnp.where(head0, ms[0], ms[1])
            lb[g, rows, :] = jnp.where(head0, ls[0], ls[1])
            return carry

        lax.fori_loop(0, nblocks, body, 0)

    for g, (_, rate) in enumerate(DIL_PAIRS):
        if rate == 1:
            continue
        length = t // rate
        for res in range(rate):
            src = slice(res * length, (res + 1) * length)
            dst = pl.ds(res, length, stride=rate)
            tacc[g - 1, dst, :] = accb[g, src, :]
            tmx[g - 1, dst, :] = mb[g, src, :]
            tl[g - 1, dst, :] = lb[g, src, :]

    chunk = 256
    for c in range(t // chunk):
        sl = slice(c * chunk, (c + 1) * chunk)
        m_all = [mb[0, sl, :], tmx[0, sl, :], tmx[1, sl, :]]
        l_all = [lb[0, sl, :], tl[0, sl, :], tl[1, sl, :]]
        a_all = [accb[0, sl, :], tacc[0, sl, :], tacc[1, sl, :]]
        m_max = jnp.maximum(jnp.maximum(m_all[0], m_all[1]), m_all[2])
        ws = [jnp.exp(m - m_max) for m in m_all]
        num = ws[0] * a_all[0] + ws[1] * a_all[1] + ws[2] * a_all[2]
        den = ws[0] * l_all[0] + ws[1] * l_all[1] + ws[2] * l_all[2]
        o_ref[sl, :] = (num / den).astype(o_ref.dtype)


def _dil_attention(main3, cos_d, sin_d):
    b, t, _ = main3.shape
    assert all(t % (rate * DIL_BLOCK) == 0 for _, rate in DIL_PAIRS)
    cb = COL_DIL // LANES
    npair = DIL_HEADS // 2
    nbr = len(DIL_PAIRS)

    def col(off):
        return pl.BlockSpec((None, t, LANES), lambda bi, p: (bi, 0, cb + off + p))

    tab = pl.BlockSpec((None, t, LANES), lambda bi, p: (bi, 0, 0))
    return pl.pallas_call(
        _dil_kernel,
        grid=(b, npair),
        in_specs=[col(0), col(npair), col(2 * npair), tab, tab],
        out_specs=pl.BlockSpec((None, t, LANES), lambda bi, p: (bi, 0, p)),
        out_shape=jax.ShapeDtypeStruct((b, t, DIL_WIDTH), BF16),
        name="dilated_attention",
        scratch_shapes=[pltpu.VMEM((t, LANES), F32)] * 3
        + [pltpu.VMEM((nbr, t + DIL_BLOCK, LANES), BF16)] * 3
        + [pltpu.VMEM((nbr, t, LANES), F32)] * 3
        + [pltpu.VMEM((nbr - 1, t, LANES), F32)] * 3,
        compiler_params=_cparams(("parallel", "parallel")),
    )(main3, main3, main3, cos_d, sin_d)


def _merge_kernel(x_ref, gf_ref, gd_ref, gm_ref, of_ref, od_ref, om_ref,
                  wf_ref, wd_ref, wm_ref, wo_ref, o_ref):
    merged = (jax.nn.sigmoid(gf_ref[...].astype(F32)) * _dot(of_ref[...], wf_ref[...])
              + jax.nn.sigmoid(gd_ref[...].astype(F32)) * _dot(od_ref[...], wd_ref[...])
              + jax.nn.sigmoid(gm_ref[...].astype(F32)) * _dot(om_ref[...], wm_ref[...]))
    o_ref[...] = x_ref[...] + _dot(merged.astype(BF16), wo_ref[...])


def _merge(x2, main2, o_fox, o_dil, o_mla, wf, wd, wm, wo, tm=512):
    n = x2.shape[0]

    def tok(width, col=0):
        return pl.BlockSpec((tm, width), lambda i: (i, col))

    def full(w):
        return pl.BlockSpec(w.shape, lambda i: (0, 0))

    return pl.pallas_call(
        _merge_kernel,
        grid=(n // tm,),
        in_specs=[tok(D_MODEL), tok(D_MODEL, 0), tok(D_MODEL, 1), tok(D_MODEL, 2),
                  tok(FOX_WIDTH), tok(DIL_WIDTH), tok(MLA_WIDTH),
                  full(wf), full(wd), full(wm), full(wo)],
        out_specs=tok(D_MODEL),
        out_shape=jax.ShapeDtypeStruct((n, D_MODEL), F32),
        name="merge_outproj",
        compiler_params=_cparams(("parallel",)),
    )(x2, main2, main2, main2, o_fox, o_dil, o_mla, wf, wd, wm, wo)


def _ffn_kernel(*refs, use_combine):
    if use_combine:
        x_ref, g_ref, wg_ref, wu_ref, wd_ref, cmb_ref, o_ref, h_sc, acc_sc = refs
    else:
        x_ref, g_ref, wg_ref, wu_ref, wd_ref, o_ref, h_sc, acc_sc = refs
    c = pl.program_id(1)

    @pl.when(c == 0)
    def _():
        x = x_ref[...]
        h_sc[...] = _rms(x, g_ref[...]).astype(BF16)
        acc_sc[...] = x

    h = h_sc[...]
    gate = _dot(h, wg_ref[...])
    up = _dot(h, wu_ref[...])
    act = gate * jax.nn.sigmoid(gate) * up
    if use_combine:
        lane = _lane_iota()
        act = act * jnp.sum(jnp.where(lane == c, cmb_ref[...], 0.0), axis=1, keepdims=True)
    acc_sc[...] += _dot(act.astype(BF16), wd_ref[...])

    @pl.when(c == pl.num_programs(1) - 1)
    def _():
        o_ref[...] = acc_sc[...]


def _ffn(x2, gain, wg, wu, wd, combine=None, tm=512):
    n = x2.shape[0]
    tok = pl.BlockSpec((tm, D_MODEL), lambda i, c: (i, 0))
    if combine is None:
        nchunk = D_FF // D_FF_EXPERT
        w_specs = [pl.BlockSpec((D_MODEL, D_FF_EXPERT), lambda i, c: (0, c)),
                   pl.BlockSpec((D_MODEL, D_FF_EXPERT), lambda i, c: (0, c)),
                   pl.BlockSpec((D_FF_EXPERT, D_MODEL), lambda i, c: (c, 0))]
        extra_specs, extra = [], []
    else:
        nchunk = N_EXPERTS
        w_specs = [pl.BlockSpec((None, D_MODEL, D_FF_EXPERT), lambda i, c: (c, 0, 0)),
                   pl.BlockSpec((None, D_MODEL, D_FF_EXPERT), lambda i, c: (c, 0, 0)),
                   pl.BlockSpec((None, D_FF_EXPERT, D_MODEL), lambda i, c: (c, 0, 0))]
        extra_specs = [pl.BlockSpec((tm, LANES), lambda i, c: (i, 0))]
        extra = [combine]
    return pl.pallas_call(
        functools.partial(_ffn_kernel, use_combine=combine is not None),
        grid=(n // tm, nchunk),
        in_specs=[tok, pl.BlockSpec((1, D_MODEL), lambda i, c: (0, 0))] + w_specs + extra_specs,
        out_specs=tok,
        out_shape=jax.ShapeDtypeStruct((n, D_MODEL), F32),
        name="ffn_dense" if combine is None else "ffn_experts",
        scratch_shapes=[pltpu.VMEM((tm, D_MODEL), BF16), pltpu.VMEM((tm, D_MODEL), F32)],
        compiler_params=_cparams(("parallel", "arbitrary")),
    )(x2, gain, wg, wu, wd, *extra)


def _split3(a):
    hi = a.astype(BF16)
    r1 = a - hi.astype(F32)
    mid = r1.astype(BF16)
    lo = (r1 - mid.astype(F32)).astype(BF16)
    return hi, mid, lo


def _router_kernel(x_ref, g_ref, wh_ref, wm_ref, wl_ref, cmb_ref):
    h = _rms(x_ref[...], g_ref[...])
    h_hi, h_mid, h_lo = _split3(h)
    w_hi, w_mid, w_lo = wh_ref[...], wm_ref[...], wl_ref[...]
    logits = (_dot(h_hi, w_hi) + (_dot(h_hi, w_mid) + _dot(h_mid, w_hi))
              + (_dot(h_hi, w_lo) + _dot(h_mid, w_mid) + _dot(h_lo, w_hi)))
    lane = _lane_iota()
    logits = jnp.where(lane < N_EXPERTS, logits, NEG_INF)
    v1 = jnp.max(logits, axis=1, keepdims=True)
    i1 = jnp.min(jnp.where(logits == v1, lane, LANES), axis=1, keepdims=True)
    first = lane == i1
    rest = jnp.where(first, NEG_INF, logits)
    v2 = jnp.max(rest, axis=1, keepdims=True)
    i2 = jnp.min(jnp.where(rest == v2, lane, LANES), axis=1, keepdims=True)
    second = lane == i2
    e2 = jnp.exp(v2 - v1)
    w1 = 1.0 / (1.0 + e2)
    w2 = e2 / (1.0 + e2)
    cmb_ref[...] = jnp.where(first, w1, 0.0) + jnp.where(second, w2, 0.0)


def _router(x2, gain, w_router, tm=512):
    n = x2.shape[0]
    w_pad = jnp.zeros((D_MODEL, LANES), F32).at[:, :N_EXPERTS].set(w_router)
    w_hi = w_pad.astype(BF16)
    r1 = w_pad - w_hi.astype(F32)
    w_mid = r1.astype(BF16)
    w_lo = (r1 - w_mid.astype(F32)).astype(BF16)
    wspec = pl.BlockSpec((D_MODEL, LANES), lambda i: (0, 0))
    return pl.pallas_call(
        _router_kernel,
        grid=(n // tm,),
        in_specs=[pl.BlockSpec((tm, D_MODEL), lambda i: (i, 0)),
                  pl.BlockSpec((1, D_MODEL), lambda i: (0, 0)), wspec, wspec, wspec],
        out_specs=pl.BlockSpec((tm, LANES), lambda i: (i, 0)),
        out_shape=jax.ShapeDtypeStruct((n, LANES), F32),
        name="router",
        compiler_params=_cparams(("parallel",)),
    )(x2, gain, w_hi, w_mid, w_lo)


def _ple_kernel(x_ref, p_ref, g_ref, wg_ref, wp_ref, fg_ref, o_ref, *, final):
    x = x_ref[...]
    h = _rms(x, g_ref[...]).astype(BF16)
    gate = jax.nn.sigmoid(_dot(h, wg_ref[...]))
    y = x + gate * _dot(p_ref[...].astype(BF16), wp_ref[...])
    if final:
        y = _rms(y, fg_ref[...])
    o_ref[...] = y


def _ple(x2, p2, gain, wg, wp, final_gain, final, tm=512):
    n = x2.shape[0]
    vec = pl.BlockSpec((1, D_MODEL), lambda i: (0, 0))
    return pl.pallas_call(
        functools.partial(_ple_kernel, final=final),
        grid=(n // tm,),
        in_specs=[pl.BlockSpec((tm, D_MODEL), lambda i: (i, 0)),
                  pl.BlockSpec((tm, PLE_DIM), lambda i: (i, 0)), vec,
                  pl.BlockSpec(wg.shape, lambda i: (0, 0)),
                  pl.BlockSpec(wp.shape, lambda i: (0, 0)), vec],
        out_specs=pl.BlockSpec((tm, D_MODEL), lambda i: (i, 0)),
        out_shape=jax.ShapeDtypeStruct((n, D_MODEL), F32),
        name="ple_final" if final else "ple",
        compiler_params=_cparams(("parallel",)),
    )(x2, p2, gain, wg, wp, final_gain)


def _prep_in_weights(w_in):
    sizes = (FOX_WIDTH, FOX_WIDTH, FOX_WIDTH, FOX_HEADS, DIL_WIDTH, DIL_WIDTH, DIL_WIDTH,
             MLA_Q_RANK, MLA_KV_RANK, MLA_ROPE_DIM, GATE_WIDTH)
    offs = [0]
    for s in sizes:
        offs.append(offs[-1] + s)
    fq, fk, fv, fz, dq, dk, dv, cq, ckv, kr, gates = [w_in[..., offs[i]:offs[i + 1]]
                                                      for i in range(len(sizes))]
    scale = HEAD_DIM ** -0.5
    main = jnp.concatenate([gates, fq * scale, fk, fv, dq * scale, dk, dv, cq, ckv], axis=-1)
    pad = jnp.zeros(w_in.shape[:-1] + (LANES - FOX_HEADS,), w_in.dtype)
    small = jnp.concatenate([fz, pad] + [kr] * MLA_HEADS, axis=-1)
    return main.astype(BF16), small.astype(BF16)


def _prep_mla_weights(w_uq, w_ukv):
    depth = w_uq.shape[0]
    uq = w_uq.reshape(depth, MLA_Q_RANK, MLA_HEADS, MLA_NOPE_DIM + MLA_ROPE_DIM)
    wq = jnp.concatenate([uq[..., :MLA_NOPE_DIM].reshape(depth, MLA_Q_RANK, -1),
                          uq[..., MLA_NOPE_DIM:].reshape(depth, MLA_Q_RANK, -1)], axis=-1)
    ukv = w_ukv.reshape(depth, MLA_KV_RANK, MLA_HEADS, MLA_NOPE_DIM + MLA_V_DIM)
    wkv = jnp.concatenate([ukv[..., :MLA_NOPE_DIM].reshape(depth, MLA_KV_RANK, -1),
                           ukv[..., MLA_NOPE_DIM:].reshape(depth, MLA_KV_RANK, -1)], axis=-1)
    return wq.astype(BF16), wkv.astype(BF16)


def kernel(x, p, positions, mix_norm, w_in, b_forget, mla_q_norm, mla_kv_norm, w_uq, w_ukv,
           w_br_fox, w_br_dil, w_br_mla, w_out, ffn_norm, w_ffn_gate, w_ffn_up, w_ffn_down,
           w_router, w_exp_gate, w_exp_up, w_exp_down, ple_norm, w_ple_gate, w_ple_proj,
           final_norm):
    b, t, _ = x.shape
    n = b * t
    depth = w_in.shape[0]

    w_main, w_small = _prep_in_weights(w_in)
    wq_all, wkv_all = _prep_mla_weights(w_uq, w_ukv)
    bias_all = jnp.concatenate(
        [b_forget.astype(F32), jnp.zeros((depth, LANES - FOX_HEADS), F32)], axis=-1)
    cos_d, sin_d, cos_m, sin_m = _rope_tables(positions)
    final_gain = final_norm.reshape(1, D_MODEL)

    x2 = x.reshape(n, D_MODEL)
    for i in range(depth):
        main2, small2 = _inproj(x2, mix_norm[i].reshape(1, D_MODEL), w_main[i], w_small[i])
        main3 = main2.reshape(b, t, MAIN_WIDTH)
        small3 = small2.reshape(b, t, SMALL_WIDTH)

        decay = _decay(small3, bias_all[i].reshape(1, LANES))
        decay_pairs = decay[:, :, :FOX_HEADS].transpose(0, 2, 1).reshape(b, FOX_HEADS // 2, 2, t)
        o_fox = _fox_attention(main3, decay_pairs)

        o_dil = _dil_attention(main3, cos_d, sin_d)

        qn, qr, kn, kr4, vv = _mla_pre(main3, small3, cos_m, sin_m,
                                       mla_q_norm[i].reshape(1, -1), mla_kv_norm[i].reshape(1, -1),
                                       wq_all[i], wkv_all[i])
        o_mla = _mla_attention(qn, qr, kn, kr4, vv)

        x2 = _merge(x2, main2, o_fox.reshape(n, -1), o_dil.reshape(n, -1), o_mla.reshape(n, -1),
                    w_br_fox[i].astype(BF16), w_br_dil[i].astype(BF16), w_br_mla[i].astype(BF16),
                    w_out[i].astype(BF16))

        j = i // 2
        gain = ffn_norm[i].reshape(1, D_MODEL)
        if i % 2 == 0:
            x2 = _ffn(x2, gain, w_ffn_gate[j].astype(BF16), w_ffn_up[j].astype(BF16),
                      w_ffn_down[j].astype(BF16))
        else:
            combine = _router(x2, gain, w_router[j])
            x2 = _ffn(x2, gain, w_exp_gate[j].astype(BF16), w_exp_up[j].astype(BF16),
                      w_exp_down[j].astype(BF16), combine)

        x2 = _ple(x2, p[i].reshape(n, PLE_DIM), ple_norm[i].reshape(1, D_MODEL),
                  w_ple_gate[i].astype(BF16), w_ple_proj[i].astype(BF16), final_gain,
                  final=(i == depth - 1))
    return x2.reshape(b, t, D_MODEL)
```

```python
import functools

import jax
import jax.numpy as jnp
from jax import lax
from jax.experimental import pallas as pl
from jax.experimental.pallas import tpu as pltpu

F32 = jnp.float32
BF16 = jnp.bfloat16

D_MODEL = 1024
HEAD_DIM = 64
ROPE_THETA = 10000.0
NORM_EPS = 1e-6
FOX_HEADS = 6
FOX_WIDTH = FOX_HEADS * HEAD_DIM
DIL_HEADS = 6
DIL_WIDTH = DIL_HEADS * HEAD_DIM
DIL_PAIRS = ((128, 1), (512, 4), (2048, 16))
DIL_BLOCK = 128
MLA_HEADS = 4
MLA_Q_RANK = 256
MLA_KV_RANK = 256
MLA_NOPE_DIM = 64
MLA_ROPE_DIM = 32
MLA_V_DIM = 64
MLA_WIDTH = MLA_HEADS * MLA_V_DIM
N_BRANCHES = 3
D_FF = 2816
N_EXPERTS = 8
D_FF_EXPERT = 1408
PLE_DIM = 256

LANES = 128
GATE_WIDTH = N_BRANCHES * D_MODEL
COL_GATES = 0
COL_FOX = GATE_WIDTH
COL_DIL = COL_FOX + 3 * FOX_WIDTH
COL_MLA = COL_DIL + 3 * DIL_WIDTH
MAIN_WIDTH = COL_MLA + MLA_Q_RANK + MLA_KV_RANK
SMALL_WIDTH = 2 * LANES
VMEM_LIMIT = 56 * 1024 * 1024

NEG_INF = float("-inf")


def _cparams(sem):
    return pltpu.CompilerParams(dimension_semantics=sem, vmem_limit_bytes=VMEM_LIMIT)


def _rms(x, gain):
    return x * lax.rsqrt(jnp.mean(x * x, axis=-1, keepdims=True) + NORM_EPS) * gain


def _dot(a, b):
    return jnp.dot(a, b, preferred_element_type=F32)


def _dot_nt(a, b):
    return lax.dot_general(a, b, (((1,), (1,)), ((), ())), preferred_element_type=F32)


def _lane_iota():
    return lax.broadcasted_iota(jnp.int32, (1, LANES), 1)


def _rope_tab_kernel(pos_ref, fd_ref, fm_ref, cd_ref, sd_ref, cm_ref, sm_ref):
    pos = pos_ref[...].astype(F32)
    lane = _lane_iota()
    ang = pos * fd_ref[...]
    cd_ref[...] = jnp.cos(ang)
    sd_ref[...] = jnp.sin(ang) * jnp.where((lane % HEAD_DIM) < HEAD_DIM // 2, -1.0, 1.0)
    ang = pos * fm_ref[...]
    cm_ref[...] = jnp.cos(ang)
    sm_ref[...] = jnp.sin(ang) * jnp.where((lane % MLA_ROPE_DIM) < MLA_ROPE_DIM // 2, -1.0, 1.0)


def _rope_tables(positions):
    b, t = positions.shape
    half_d = HEAD_DIM // 2
    half_m = MLA_ROPE_DIM // 2
    inv_d = ROPE_THETA ** (-jnp.arange(half_d, dtype=F32) / half_d)
    inv_m = ROPE_THETA ** (-jnp.arange(half_m, dtype=F32) / half_m)
    fd = jnp.tile(inv_d, LANES // half_d)[None, :]
    fm = jnp.tile(inv_m, LANES // half_m)[None, :]
    tab = jax.ShapeDtypeStruct((b, t, LANES), F32)
    tab_spec = pl.BlockSpec((None, t, LANES), lambda i: (i, 0, 0))
    vec_spec = pl.BlockSpec((1, LANES), lambda i: (0, 0))
    return pl.pallas_call(
        _rope_tab_kernel,
        grid=(b,),
        in_specs=[pl.BlockSpec((None, t, 1), lambda i: (i, 0, 0)), vec_spec, vec_spec],
        out_specs=[tab_spec] * 4,
        out_shape=[tab] * 4,
        name="rope_tables",
        compiler_params=_cparams(("parallel",)),
    )(positions.reshape(b, t, 1), fd, fm)


def _swap_halves(x, width):
    lane = _lane_iota()
    half = width // 2
    return jnp.where((lane % width) < half,
                     pltpu.roll(x, LANES - half, axis=1),
                     pltpu.roll(x, half, axis=1))


def _rope(x, cos, sin_signed, width):
    return x * cos + _swap_halves(x, width) * sin_signed


def _col_chunks(width, step=512):
    return [(c, min(step, width - c)) for c in range(0, width, step)]


def _inproj_kernel(x_ref, g_ref, w_ref, ws_ref, o_ref, os_ref):
    h = _rms(x_ref[...], g_ref[...]).astype(BF16)
    for c, n in _col_chunks(MAIN_WIDTH):
        o_ref[:, c:c + n] = _dot(h, w_ref[:, c:c + n]).astype(BF16)
    os_ref[...] = _dot(h, ws_ref[...])


def _inproj(x2, gain, w_main, w_small, tm=512):
    n = x2.shape[0]
    return pl.pallas_call(
        _inproj_kernel,
        grid=(n // tm,),
        in_specs=[
            pl.BlockSpec((tm, D_MODEL), lambda i: (i, 0)),
            pl.BlockSpec((1, D_MODEL), lambda i: (0, 0)),
            pl.BlockSpec((D_MODEL, MAIN_WIDTH), lambda i: (0, 0)),
            pl.BlockSpec((D_MODEL, SMALL_WIDTH), lambda i: (0, 0)),
        ],
        out_specs=[
            pl.BlockSpec((tm, MAIN_WIDTH), lambda i: (i, 0)),
            pl.BlockSpec((tm, SMALL_WIDTH), lambda i: (i, 0)),
        ],
        out_shape=[
            jax.ShapeDtypeStruct((n, MAIN_WIDTH), BF16),
            jax.ShapeDtypeStruct((n, SMALL_WIDTH), F32),
        ],
        name="inproj",
        compiler_params=_cparams(("parallel",)),
    )(x2, gain, w_main, w_small)


CUMSUM_BLOCK = 256
DECAY_PART_STRIDE = 8
DECAY_PARTS = 3


def _split3(a):
    hi = a.astype(BF16)
    r1 = a - hi.astype(F32)
    mid = r1.astype(BF16)
    lo = (r1 - mid.astype(F32)).astype(BF16)
    return hi, mid, lo


def _decay_kernel(z_ref, b_ref, o_ref):
    t = z_ref.shape[0]
    row = lax.broadcasted_iota(jnp.int32, (CUMSUM_BLOCK, CUMSUM_BLOCK), 0)
    col = lax.broadcasted_iota(jnp.int32, (CUMSUM_BLOCK, CUMSUM_BLOCK), 1)
    tri = jnp.where(col <= row, 1.0, 0.0).astype(BF16)
    is_head = _lane_iota() < FOX_HEADS
    carry = jnp.zeros((1, LANES), F32)
    for blk in range(t // CUMSUM_BLOCK):
        sl = slice(blk * CUMSUM_BLOCK, (blk + 1) * CUMSUM_BLOCK)
        z = z_ref[sl, :] + b_ref[...]
        logf = -(jnp.maximum(-z, 0.0) + jnp.log1p(jnp.exp(-jnp.abs(z))))
        hi, mid, lo = _split3(logf)
        cs = _dot(tri, hi) + _dot(tri, mid) + _dot(tri, lo) + carry
        carry = cs[CUMSUM_BLOCK - 1:CUMSUM_BLOCK, :]
        parts = _split3(jnp.where(is_head, cs, 0.0))
        packed = parts[0].astype(F32)
        for j in range(1, DECAY_PARTS):
            packed = packed + pltpu.roll(parts[j].astype(F32), j * DECAY_PART_STRIDE, axis=1)
        o_ref[sl, :] = packed.astype(BF16)


def _decay(small3, bias):
    b, t, _ = small3.shape
    return pl.pallas_call(
        _decay_kernel,
        grid=(b,),
        in_specs=[pl.BlockSpec((None, t, LANES), lambda i: (i, 0, 0)),
                  pl.BlockSpec((1, LANES), lambda i: (0, 0))],
        out_specs=pl.BlockSpec((None, t, LANES), lambda i: (i, 0, 0)),
        out_shape=jax.ShapeDtypeStruct((b, t, LANES), BF16),
        name="decay",
        compiler_params=_cparams(("parallel",)),
    )(small3, bias)


def _pair_softmax(s, m_prev, ok_cols):
    nc = s.shape[1] // LANES
    cols = [s[:, c * LANES:(c + 1) * LANES] for c in range(nc)]
    if ok_cols is not None:
        cols = [jnp.where(ok, col, NEG_INF) for ok, col in zip(ok_cols, cols)]
    cmax = cols[0]
    for col in cols[1:]:
        cmax = jnp.maximum(cmax, col)
    m_new = jnp.maximum(m_prev, jnp.max(cmax, axis=1, keepdims=True))
    alpha = jnp.exp(m_prev - m_new)
    p = jnp.concatenate([jnp.exp(col - m_new).astype(BF16) for col in cols], axis=1)
    return m_new, alpha, p


def _causal_pair_attention(qs, keys_at, vals_at, qi, o_ref, tq):
    lane = _lane_iota()
    head0 = lane < HEAD_DIM
    rows = 2 * tq

    def scores(j):
        return _dot_nt(qs, keys_at(j))

    def weighted_values(p, j):
        v = vals_at(j)
        one = jnp.ones_like(v)
        pv0 = _dot(p[:tq], jnp.where(head0, v, one))
        pv1 = _dot(p[tq:], jnp.where(head0, one, v))
        return jnp.concatenate([pv0, pv1], axis=0)

    def body(j, carry):
        s, m_prev, acc = carry
        s_next = scores(j + 1)
        m_new, alpha, p = _pair_softmax(s, m_prev, None)
        return s_next, m_new, alpha * acc + weighted_values(p, j)

    init = (scores(0), jnp.full((rows, LANES), NEG_INF, F32), jnp.zeros((rows, LANES), F32))
    s, m_prev, acc = lax.fori_loop(0, qi, body, init)

    r_i = lax.broadcasted_iota(jnp.int32, (tq, LANES), 0)
    c_i = lax.broadcasted_iota(jnp.int32, (tq, LANES), 1)
    ok_cols = []
    for c in range(tq // LANES):
        ok = c_i + c * LANES <= r_i
        ok_cols.append(jnp.concatenate([ok, ok], axis=0))
    _, alpha, p = _pair_softmax(s, m_prev, ok_cols)
    acc = alpha * acc + weighted_values(p, qi)
    out = acc / pltpu.roll(acc, HEAD_DIM, axis=1)
    o_ref[...] = jnp.where(head0, out[:tq], out[tq:]).astype(o_ref.dtype)


def _block_rows(j, tq):
    return pl.ds(pl.multiple_of(j * tq, tq), tq)


def _fox_kernel(q_ref, k_ref, v_ref, d_ref, o_ref, *, tq):
    p = pl.program_id(1)
    qi = pl.program_id(2)
    q = q_ref[...]
    lane = _lane_iota()
    zero = jnp.zeros_like(q)
    halves = []
    for h in range(2):
        head = 2 * p + h
        pick = (lane % DECAY_PART_STRIDE == head) & (lane < DECAY_PARTS * DECAY_PART_STRIDE)
        neg = jnp.broadcast_to(jnp.where(pick, -1.0, 0.0).astype(BF16), q.shape)
        q_h = jnp.where((lane // HEAD_DIM) == h, q, zero)
        halves.append(jnp.concatenate([q_h, neg], axis=1))
    qs = jnp.concatenate(halves, axis=0)

    def keys_at(j):
        rows = _block_rows(j, tq)
        return jnp.concatenate([k_ref[rows, :], d_ref[rows, :]], axis=1)

    def vals_at(j):
        return v_ref[_block_rows(j, tq), :]

    _causal_pair_attention(qs, keys_at, vals_at, qi, o_ref, tq)


def _fox_attention(main3, dcols, tq=512):
    b, t, _ = main3.shape
    cb = COL_FOX // LANES
    npair = FOX_HEADS // 2

    def seq(col):
        return pl.BlockSpec((None, t, LANES), lambda bi, p, qi: (bi, 0, col(p)))

    return pl.pallas_call(
        functools.partial(_fox_kernel, tq=tq),
        grid=(b, npair, t // tq),
        in_specs=[
            pl.BlockSpec((None, tq, LANES), lambda bi, p, qi: (bi, qi, cb + p)),
            seq(lambda p: cb + npair + p),
            seq(lambda p: cb + 2 * npair + p),
            seq(lambda p: 0),
        ],
        out_specs=pl.BlockSpec((None, tq, LANES), lambda bi, p, qi: (bi, qi, p)),
        out_shape=jax.ShapeDtypeStruct((b, t, FOX_WIDTH), BF16),
        name="fox_attention",
        compiler_params=_cparams(("parallel", "parallel", "arbitrary")),
    )(main3, main3, main3, dcols)


def _mla_pre_kernel(cq_ref, ckv_ref, kr_ref, cos_ref, sin_ref, gq_ref, gkv_ref, wq_ref, wkv_ref,
                    qn_ref, qr_ref, kn_ref, kr_out_ref, v_ref, *, scale):
    cos = cos_ref[...]
    sin = sin_ref[...]
    nope = MLA_HEADS * MLA_NOPE_DIM
    hq = _rms(cq_ref[...].astype(F32), gq_ref[...]).astype(BF16)
    q = _dot(hq, wq_ref[...])
    qn_ref[...] = (q[:, :nope] * scale).astype(BF16)
    qr_ref[...] = (_rope(q[:, nope:], cos, sin, MLA_ROPE_DIM) * scale).astype(BF16)
    hkv = _rms(ckv_ref[...].astype(F32), gkv_ref[...]).astype(BF16)
    kv = _dot(hkv, wkv_ref[...])
    kn_ref[...] = kv[:, :nope].astype(BF16)
    v_ref[...] = kv[:, nope:].astype(BF16)
    kr_out_ref[...] = _rope(kr_ref[...], cos, sin, MLA_ROPE_DIM).astype(BF16)


def _mla_pre(main3, small3, cos_m, sin_m, gq, gkv, wq, wkv, tm=512):
    b, t, _ = main3.shape
    cq_blk = COL_MLA // MLA_Q_RANK
    nope = MLA_HEADS * MLA_NOPE_DIM
    vw = MLA_HEADS * MLA_V_DIM

    def tok(width, col=0):
        return pl.BlockSpec((None, tm, width), lambda bi, i: (bi, i, col))

    def full(shape):
        return pl.BlockSpec(shape, lambda bi, i: (0,) * len(shape))

    return pl.pallas_call(
        functools.partial(_mla_pre_kernel, scale=(MLA_NOPE_DIM + MLA_ROPE_DIM) ** -0.5),
        grid=(b, t // tm),
        in_specs=[tok(MLA_Q_RANK, cq_blk), tok(MLA_KV_RANK, cq_blk + 1), tok(LANES, 1),
                  tok(LANES), tok(LANES), full((1, MLA_Q_RANK)), full((1, MLA_KV_RANK)),
                  full(wq.shape), full(wkv.shape)],
        out_specs=[tok(nope), tok(LANES), tok(nope), tok(LANES), tok(vw)],
        out_shape=[jax.ShapeDtypeStruct((b, t, nope), BF16),
                   jax.ShapeDtypeStruct((b, t, LANES), BF16),
                   jax.ShapeDtypeStruct((b, t, nope), BF16),
                   jax.ShapeDtypeStruct((b, t, LANES), BF16),
                   jax.ShapeDtypeStruct((b, t, vw), BF16)],
        name="mla_pre",
        compiler_params=_cparams(("parallel", "parallel")),
    )(main3, main3, small3, cos_m, sin_m, gq, gkv, wq, wkv)


def _mla_kernel(qn_ref, qr_ref, kn_ref, kr_ref, v_ref, o_ref, *, tq):
    p = pl.program_id(1)
    qi = pl.program_id(2)
    qn = qn_ref[...]
    qr = qr_ref[...]
    lane = _lane_iota()
    zero = jnp.zeros_like(qn)
    halves = []
    for h in range(2):
        nope_h = jnp.where((lane // MLA_NOPE_DIM) == h, qn, zero)
        rope_h = jnp.where((lane // MLA_ROPE_DIM) == 2 * p + h, qr, zero)
        halves.append(jnp.concatenate([nope_h, rope_h], axis=1))
    qs = jnp.concatenate(halves, axis=0)

    def keys_at(j):
        rows = _block_rows(j, tq)
        return jnp.concatenate([kn_ref[rows, :], kr_ref[rows, :]], axis=1)

    def vals_at(j):
        return v_ref[_block_rows(j, tq), :]

    _causal_pair_attention(qs, keys_at, vals_at, qi, o_ref, tq)


def _mla_attention(qn, qr, kn, kr4, vv, tq=512):
    b, t, _ = qn.shape
    npair = MLA_HEADS // 2
    q_pair = pl.BlockSpec((None, tq, LANES), lambda bi, p, qi: (bi, qi, p))
    q_all = pl.BlockSpec((None, tq, LANES), lambda bi, p, qi: (bi, qi, 0))
    k_pair = pl.BlockSpec((None, t, LANES), lambda bi, p, qi: (bi, 0, p))
    k_all = pl.BlockSpec((None, t, LANES), lambda bi, p, qi: (bi, 0, 0))
    return pl.pallas_call(
        functools.partial(_mla_kernel, tq=tq),
        grid=(b, npair, t // tq),
        in_specs=[q_pair, q_all, k_pair, k_all, k_pair],
        out_specs=q_pair,
        out_shape=jax.ShapeDtypeStruct((b, t, MLA_WIDTH), BF16),
        name="mla_attention",
        compiler_params=_cparams(("parallel", "parallel", "arbitrary")),
    )(qn, qr, kn, kr4, vv)


def _dil_kernel(q_ref, k_ref, v_ref, cos_ref, sin_ref, o_ref,
                qf, kf, vf, qb, kb, vb, accb, mb, lb, tacc, tmx, tl):
    t = q_ref.shape[0]
    blk = DIL_BLOCK
    lane = _lane_iota()
    head0 = lane < HEAD_DIM
    cos = cos_ref[...]
    sin = sin_ref[...]
    qf[...] = _rope(q_ref[...].astype(F32), cos, sin, HEAD_DIM)
    kf[...] = _rope(k_ref[...].astype(F32), cos, sin, HEAD_DIM)
    vf[...] = v_ref[...].astype(F32)

    for g, (_, rate) in enumerate(DIL_PAIRS):
        length = t // rate
        kb[g, 0:blk, :] = jnp.zeros((blk, LANES), BF16)
        vb[g, 0:blk, :] = jnp.zeros((blk, LANES), BF16)
        for res in range(rate):
            dst = slice(blk + res * length, blk + (res + 1) * length)
            if rate == 1:
                src = slice(None)
            else:
                src = pl.ds(res, length, stride=rate)
            qb[g, dst, :] = qf[src, :].astype(BF16)
            kb[g, dst, :] = kf[src, :].astype(BF16)
            vb[g, dst, :] = vf[src, :].astype(BF16)

    qi2 = lax.broadcasted_iota(jnp.int32, (blk, 2 * blk), 0)
    kj2 = lax.broadcasted_iota(jnp.int32, (blk, 2 * blk), 1)
    band = (kj2 >= qi2) & (kj2 <= qi2 + blk)
    bias_full = jnp.where(band, 0.0, NEG_INF)
    bias_first = jnp.where(band & (kj2 >= blk), 0.0, NEG_INF)
    qi1 = lax.broadcasted_iota(jnp.int32, (blk, blk), 0)
    kj1 = lax.broadcasted_iota(jnp.int32, (blk, blk), 1)
    bias_single = jnp.where(kj1 <= qi1, 0.0, NEG_INF)

    for g, (window, rate) in enumerate(DIL_PAIRS):
        assert window // rate == blk
        nb = t // rate // blk
        nblocks = t // blk

        def body(j, carry, g=g, nb=nb):
            base = pl.multiple_of(j * blk, blk)
            q = qb[g, pl.ds(base + blk, blk), :]
            zero = jnp.zeros_like(q)
            qs = jnp.concatenate([jnp.where(head0, q, zero), jnp.where(head0, zero, q)], axis=0)
            if nb == 1:
                keys = kb[g, pl.ds(base + blk, blk), :]
                vals = vb[g, pl.ds(base + blk, blk), :]
                bias = bias_single
            else:
                keys = kb[g, pl.ds(base, 2 * blk), :]
                vals = vb[g, pl.ds(base, 2 * blk), :]
                bias = jnp.where((j % nb) != 0, bias_full, bias_first)
            s = _dot_nt(qs, keys)
            ps, ms, ls = [], [], []
            for h in range(2):
                sh = s[h * blk:(h + 1) * blk] + bias
                m = jnp.max(sh, axis=1, keepdims=True)
                p = jnp.exp(sh - m)
                ps.append(p.astype(BF16))
                ms.append(m)
                ls.append(jnp.sum(p, axis=1, keepdims=True))
            pv = _dot(jnp.concatenate(ps, axis=0), vals)
            rows = pl.ds(base, blk)
            accb[g, rows, :] = jnp.where(head0, pv[:blk], pv[blk:])
            mb[g, rows, :] = jnp.where(head0, ms[0], ms[1])
            lb[g, rows, :] = jnp.where(head0, ls[0], ls[1])
            return carry

        lax.fori_loop(0, nblocks, body, 0)

    for g, (_, rate) in enumerate(DIL_PAIRS):
        if rate == 1:
            continue
        length = t // rate
        for res in range(rate):
            src = slice(res * length, (res + 1) * length)
            dst = pl.ds(res, length, stride=rate)
            tacc[g - 1, dst, :] = accb[g, src, :]
            tmx[g - 1, dst, :] = mb[g, src, :]
            tl[g - 1, dst, :] = lb[g, src, :]

    chunk = 256
    for c in range(t // chunk):
        sl = slice(c * chunk, (c + 1) * chunk)
        m_all = [mb[0, sl, :], tmx[0, sl, :], tmx[1, sl, :]]
        l_all = [lb[0, sl, :], tl[0, sl, :], tl[1, sl, :]]
        a_all = [accb[0, sl, :], tacc[0, sl, :], tacc[1, sl, :]]
        m_max = jnp.maximum(jnp.maximum(m_all[0], m_all[1]), m_all[2])
        ws = [jnp.exp(m - m_max) for m in m_all]
        num = ws[0] * a_all[0] + ws[1] * a_all[1] + ws[2] * a_all[2]
        den = ws[0] * l_all[0] + ws[1] * l_all[1] + ws[2] * l_all[2]
        o_ref[sl, :] = (num / den).astype(o_ref.dtype)


def _dil_attention(main3, cos_d, sin_d):
    b, t, _ = main3.shape
    assert all(t % (rate * DIL_BLOCK) == 0 for _, rate in DIL_PAIRS)
    cb = COL_DIL // LANES
    npair = DIL_HEADS // 2
    nbr = len(DIL_PAIRS)

    def col(off):
        return pl.BlockSpec((None, t, LANES), lambda bi, p: (bi, 0, cb + off + p))

    tab = pl.BlockSpec((None, t, LANES), lambda bi, p: (bi, 0, 0))
    return pl.pallas_call(
        _dil_kernel,
        grid=(b, npair),
        in_specs=[col(0), col(npair), col(2 * npair), tab, tab],
        out_specs=pl.BlockSpec((None, t, LANES), lambda bi, p: (bi, 0, p)),
        out_shape=jax.ShapeDtypeStruct((b, t, DIL_WIDTH), BF16),
        name="dilated_attention",
        scratch_shapes=[pltpu.VMEM((t, LANES), F32)] * 3
        + [pltpu.VMEM((nbr, t + DIL_BLOCK, LANES), BF16)] * 3
        + [pltpu.VMEM((nbr, t, LANES), F32)] * 3
        + [pltpu.VMEM((nbr - 1, t, LANES), F32)] * 3,
        compiler_params=_cparams(("parallel", "parallel")),
    )(main3, main3, main3, cos_d, sin_d)


def _merge_kernel(x_ref, gf_ref, gd_ref, gm_ref, of_ref, od_ref, om_ref,
                  wf_ref, wd_ref, wm_ref, wo_ref, o_ref):
    merged = (jax.nn.sigmoid(gf_ref[...].astype(F32)) * _dot(of_ref[...], wf_ref[...])
              + jax.nn.sigmoid(gd_ref[...].astype(F32)) * _dot(od_ref[...], wd_ref[...])
              + jax.nn.sigmoid(gm_ref[...].astype(F32)) * _dot(om_ref[...], wm_ref[...]))
    o_ref[...] = x_ref[...] + _dot(merged.astype(BF16), wo_ref[...])


def _merge(x2, main2, o_fox, o_dil, o_mla, wf, wd, wm, wo, tm=512):
    n = x2.shape[0]

    def tok(width, col=0):
        return pl.BlockSpec((tm, width), lambda i: (i, col))

    def full(w):
        return pl.BlockSpec(w.shape, lambda i: (0, 0))

    return pl.pallas_call(
        _merge_kernel,
        grid=(n // tm,),
        in_specs=[tok(D_MODEL), tok(D_MODEL, 0), tok(D_MODEL, 1), tok(D_MODEL, 2),
                  tok(FOX_WIDTH), tok(DIL_WIDTH), tok(MLA_WIDTH),
                  full(wf), full(wd), full(wm), full(wo)],
        out_specs=tok(D_MODEL),
        out_shape=jax.ShapeDtypeStruct((n, D_MODEL), F32),
        name="merge_outproj",
        compiler_params=_cparams(("parallel",)),
    )(x2, main2, main2, main2, o_fox, o_dil, o_mla, wf, wd, wm, wo)


def _ffn_kernel(*refs, use_combine):
    if use_combine:
        x_ref, g_ref, wg_ref, wu_ref, wd_ref, cmb_ref, o_ref, h_sc, acc_sc = refs
    else:
        x_ref, g_ref, wg_ref, wu_ref, wd_ref, o_ref, h_sc, acc_sc = refs
    c = pl.program_id(1)

    @pl.when(c == 0)
    def _():
        x = x_ref[...]
        h_sc[...] = _rms(x, g_ref[...]).astype(BF16)
        acc_sc[...] = x

    h = h_sc[...]
    gate = _dot(h, wg_ref[...])
    up = _dot(h, wu_ref[...])
    act = gate * jax.nn.sigmoid(gate) * up
    if use_combine:
        lane = _lane_iota()
        act = act * jnp.sum(jnp.where(lane == c, cmb_ref[...], 0.0), axis=1, keepdims=True)
    acc_sc[...] += _dot(act.astype(BF16), wd_ref[...])

    @pl.when(c == pl.num_programs(1) - 1)
    def _():
        o_ref[...] = acc_sc[...]


def _ffn(x2, gain, wg, wu, wd, combine=None, tm=512):
    n = x2.shape[0]
    tok = pl.BlockSpec((tm, D_MODEL), lambda i, c: (i, 0))
    if combine is None:
        nchunk = D_FF // D_FF_EXPERT
        w_specs = [pl.BlockSpec((D_MODEL, D_FF_EXPERT), lambda i, c: (0, c)),
                   pl.BlockSpec((D_MODEL, D_FF_EXPERT), lambda i, c: (0, c)),
                   pl.BlockSpec((D_FF_EXPERT, D_MODEL), lambda i, c: (c, 0))]
        extra_specs, extra = [], []
    else:
        nchunk = N_EXPERTS
        w_specs = [pl.BlockSpec((None, D_MODEL, D_FF_EXPERT), lambda i, c: (c, 0, 0)),
                   pl.BlockSpec((None, D_MODEL, D_FF_EXPERT), lambda i, c: (c, 0, 0)),
                   pl.BlockSpec((None, D_FF_EXPERT, D_MODEL), lambda i, c: (c, 0, 0))]
        extra_specs = [pl.BlockSpec((tm, LANES), lambda i, c: (i, 0))]
        extra = [combine]
    return pl.pallas_call(
        functools.partial(_ffn_kernel, use_combine=combine is not None),
        grid=(n // tm, nchunk),
        in_specs=[tok, pl.BlockSpec((1, D_MODEL), lambda i, c: (0, 0))] + w_specs + extra_specs,
        out_specs=tok,
        out_shape=jax.ShapeDtypeStruct((n, D_MODEL), F32),
        name="ffn_dense" if combine is None else "ffn_experts",
        scratch_shapes=[pltpu.VMEM((tm, D_MODEL), BF16), pltpu.VMEM((tm, D_MODEL), F32)],
        compiler_params=_cparams(("parallel", "arbitrary")),
    )(x2, gain, wg, wu, wd, *extra)


def _router_kernel(x_ref, g_ref, wh_ref, wm_ref, wl_ref, cmb_ref):
    h = _rms(x_ref[...], g_ref[...])
    h_hi, h_mid, h_lo = _split3(h)
    w_hi, w_mid, w_lo = wh_ref[...], wm_ref[...], wl_ref[...]
    logits = (_dot(h_hi, w_hi) + (_dot(h_hi, w_mid) + _dot(h_mid, w_hi))
              + (_dot(h_hi, w_lo) + _dot(h_mid, w_mid) + _dot(h_lo, w_hi)))
    lane = _lane_iota()
    logits = jnp.where(lane < N_EXPERTS, logits, NEG_INF)
    v1 = jnp.max(logits, axis=1, keepdims=True)
    i1 = jnp.min(jnp.where(logits == v1, lane, LANES), axis=1, keepdims=True)
    first = lane == i1
    rest = jnp.where(first, NEG_INF, logits)
    v2 = jnp.max(rest, axis=1, keepdims=True)
    i2 = jnp.min(jnp.where(rest == v2, lane, LANES), axis=1, keepdims=True)
    second = lane == i2
    e2 = jnp.exp(v2 - v1)
    w1 = 1.0 / (1.0 + e2)
    w2 = e2 / (1.0 + e2)
    cmb_ref[...] = jnp.where(first, w1, 0.0) + jnp.where(second, w2, 0.0)


def _router(x2, gain, w_router, tm=512):
    n = x2.shape[0]
    w_pad = jnp.zeros((D_MODEL, LANES), F32).at[:, :N_EXPERTS].set(w_router)
    w_hi = w_pad.astype(BF16)
    r1 = w_pad - w_hi.astype(F32)
    w_mid = r1.astype(BF16)
    w_lo = (r1 - w_mid.astype(F32)).astype(BF16)
    wspec = pl.BlockSpec((D_MODEL, LANES), lambda i: (0, 0))
    return pl.pallas_call(
        _router_kernel,
        grid=(n // tm,),
        in_specs=[pl.BlockSpec((tm, D_MODEL), lambda i: (i, 0)),
                  pl.BlockSpec((1, D_MODEL), lambda i: (0, 0)), wspec, wspec, wspec],
        out_specs=pl.BlockSpec((tm, LANES), lambda i: (i, 0)),
        out_shape=jax.ShapeDtypeStruct((n, LANES), F32),
        name="router",
        compiler_params=_cparams(("parallel",)),
    )(x2, gain, w_hi, w_mid, w_lo)


def _ple_kernel(x_ref, p_ref, g_ref, wg_ref, wp_ref, fg_ref, o_ref, *, final):
    x = x_ref[...]
    h = _rms(x, g_ref[...]).astype(BF16)
    gate = jax.nn.sigmoid(_dot(h, wg_ref[...]))
    y = x + gate * _dot(p_ref[...].astype(BF16), wp_ref[...])
    if final:
        y = _rms(y, fg_ref[...])
    o_ref[...] = y


def _ple(x2, p2, gain, wg, wp, final_gain, final, tm=512):
    n = x2.shape[0]
    vec = pl.BlockSpec((1, D_MODEL), lambda i: (0, 0))
    return pl.pallas_call(
        functools.partial(_ple_kernel, final=final),
        grid=(n // tm,),
        in_specs=[pl.BlockSpec((tm, D_MODEL), lambda i: (i, 0)),
                  pl.BlockSpec((tm, PLE_DIM), lambda i: (i, 0)), vec,
                  pl.BlockSpec(wg.shape, lambda i: (0, 0)),
                  pl.BlockSpec(wp.shape, lambda i: (0, 0)), vec],
        out_specs=pl.BlockSpec((tm, D_MODEL), lambda i: (i, 0)),
        out_shape=jax.ShapeDtypeStruct((n, D_MODEL), F32),
        name="ple_final" if final else "ple",
        compiler_params=_cparams(("parallel",)),
    )(x2, p2, gain, wg, wp, final_gain)


def _prep_in_weights(w_in):
    sizes = (FOX_WIDTH, FOX_WIDTH, FOX_WIDTH, FOX_HEADS, DIL_WIDTH, DIL_WIDTH, DIL_WIDTH,
             MLA_Q_RANK, MLA_KV_RANK, MLA_ROPE_DIM, GATE_WIDTH)
    offs = [0]
    for s in sizes:
        offs.append(offs[-1] + s)
    fq, fk, fv, fz, dq, dk, dv, cq, ckv, kr, gates = [w_in[..., offs[i]:offs[i + 1]]
                                                      for i in range(len(sizes))]
    scale = HEAD_DIM ** -0.5
    main = jnp.concatenate([gates, fq * scale, fk, fv, dq * scale, dk, dv, cq, ckv], axis=-1)
    pad = jnp.zeros(w_in.shape[:-1] + (LANES - FOX_HEADS,), w_in.dtype)
    small = jnp.concatenate([fz, pad] + [kr] * MLA_HEADS, axis=-1)
    return main.astype(BF16), small.astype(BF16)


def _prep_mla_weights(w_uq, w_ukv):
    depth = w_uq.shape[0]
    uq = w_uq.reshape(depth, MLA_Q_RANK, MLA_HEADS, MLA_NOPE_DIM + MLA_ROPE_DIM)
    wq = jnp.concatenate([uq[..., :MLA_NOPE_DIM].reshape(depth, MLA_Q_RANK, -1),
                          uq[..., MLA_NOPE_DIM:].reshape(depth, MLA_Q_RANK, -1)], axis=-1)
    ukv = w_ukv.reshape(depth, MLA_KV_RANK, MLA_HEADS, MLA_NOPE_DIM + MLA_V_DIM)
    wkv = jnp.concatenate([ukv[..., :MLA_NOPE_DIM].reshape(depth, MLA_KV_RANK, -1),
                           ukv[..., MLA_NOPE_DIM:].reshape(depth, MLA_KV_RANK, -1)], axis=-1)
    return wq.astype(BF16), wkv.astype(BF16)


def kernel(x, p, positions, mix_norm, w_in, b_forget, mla_q_norm, mla_kv_norm, w_uq, w_ukv,
           w_br_fox, w_br_dil, w_br_mla, w_out, ffn_norm, w_ffn_gate, w_ffn_up, w_ffn_down,
           w_router, w_exp_gate, w_exp_up, w_exp_down, ple_norm, w_ple_gate, w_ple_proj,
           final_norm):
    b, t, _ = x.shape
    n = b * t
    depth = w_in.shape[0]

    w_main, w_small = _prep_in_weights(w_in)
    wq_all, wkv_all = _prep_mla_weights(w_uq, w_ukv)
    bias_all = jnp.concatenate(
        [b_forget.astype(F32), jnp.zeros((depth, LANES - FOX_HEADS), F32)], axis=-1)
    cos_d, sin_d, cos_m, sin_m = _rope_tables(positions)
    final_gain = final_norm.reshape(1, D_MODEL)

    x2 = x.reshape(n, D_MODEL)
    for i in range(depth):
        main2, small2 = _inproj(x2, mix_norm[i].reshape(1, D_MODEL), w_main[i], w_small[i])
        main3 = main2.reshape(b, t, MAIN_WIDTH)
        small3 = small2.reshape(b, t, SMALL_WIDTH)

        dcols = _decay(small3, bias_all[i].reshape(1, LANES))
        o_fox = _fox_attention(main3, dcols)

        o_dil = _dil_attention(main3, cos_d, sin_d)

        qn, qr, kn, kr4, vv = _mla_pre(main3, small3, cos_m, sin_m,
                                       mla_q_norm[i].reshape(1, -1), mla_kv_norm[i].reshape(1, -1),
                                       wq_all[i], wkv_all[i])
        o_mla = _mla_attention(qn, qr, kn, kr4, vv)

        x2 = _merge(x2, main2, o_fox.reshape(n, -1), o_dil.reshape(n, -1), o_mla.reshape(n, -1),
                    w_br_fox[i].astype(BF16), w_br_dil[i].astype(BF16), w_br_mla[i].astype(BF16),
                    w_out[i].astype(BF16))

        j = i // 2
        gain = ffn_norm[i].reshape(1, D_MODEL)
        if i % 2 == 0:
            x2 = _ffn(x2, gain, w_ffn_gate[j].astype(BF16), w_ffn_up[j].astype(BF16),
                      w_ffn_down[j].astype(BF16))
        else:
            combine = _router(x2, gain, w_router[j])
            x2 = _ffn(x2, gain, w_exp_gate[j].astype(BF16), w_exp_up[j].astype(BF16),
                      w_exp_down[j].astype(BF16), combine)

        x2 = _ple(x2, p[i].reshape(n, PLE_DIM), ple_norm[i].reshape(1, D_MODEL),
                  w_ple_gate[i].astype(BF16), w_ple_proj[i].astype(BF16), final_gain,
                  final=(i == depth - 1))
    return x2.reshape(b, t, D_MODEL)
```

```python
import functools

import jax
import jax.numpy as jnp
from jax import lax
from jax.experimental import pallas as pl
from jax.experimental.pallas import tpu as pltpu

F32 = jnp.float32
BF16 = jnp.bfloat16

D_MODEL = 1024
HEAD_DIM = 64
ROPE_THETA = 10000.0
NORM_EPS = 1e-6
FOX_HEADS = 6
FOX_WIDTH = FOX_HEADS * HEAD_DIM
DIL_HEADS = 6
DIL_WIDTH = DIL_HEADS * HEAD_DIM
DIL_PAIRS = ((128, 1), (512, 4), (2048, 16))
DIL_BLOCK = 128
MLA_HEADS = 4
MLA_Q_RANK = 256
MLA_KV_RANK = 256
MLA_NOPE_DIM = 64
MLA_ROPE_DIM = 32
MLA_V_DIM = 64
MLA_WIDTH = MLA_HEADS * MLA_V_DIM
N_BRANCHES = 3
D_FF = 2816
N_EXPERTS = 8
D_FF_EXPERT = 1408
PLE_DIM = 256

LANES = 128
GATE_WIDTH = N_BRANCHES * D_MODEL
COL_GATES = 0
COL_FOX = GATE_WIDTH
COL_DIL = COL_FOX + 3 * FOX_WIDTH
COL_MLA = COL_DIL + 3 * DIL_WIDTH
MAIN_WIDTH = COL_MLA + MLA_Q_RANK + MLA_KV_RANK
SMALL_WIDTH = 2 * LANES
VMEM_LIMIT = 56 * 1024 * 1024

NEG_INF = float("-inf")


def _cparams(sem):
    return pltpu.CompilerParams(dimension_semantics=sem, vmem_limit_bytes=VMEM_LIMIT)


def _rms(x, gain):
    return x * lax.rsqrt(jnp.mean(x * x, axis=-1, keepdims=True) + NORM_EPS) * gain


def _dot(a, b):
    return jnp.dot(a, b, preferred_element_type=F32)


def _dot_nt(a, b):
    return lax.dot_general(a, b, (((1,), (1,)), ((), ())), preferred_element_type=F32)


def _lane_iota():
    return lax.broadcasted_iota(jnp.int32, (1, LANES), 1)


def _rope_tab_kernel(pos_ref, fd_ref, fm_ref, cd_ref, sd_ref, cm_ref, sm_ref):
    pos = pos_ref[...].astype(F32)
    lane = _lane_iota()
    ang = pos * fd_ref[...]
    cd_ref[...] = jnp.cos(ang)
    sd_ref[...] = jnp.sin(ang) * jnp.where((lane % HEAD_DIM) < HEAD_DIM // 2, -1.0, 1.0)
    ang = pos * fm_ref[...]
    cm_ref[...] = jnp.cos(ang)
    sm_ref[...] = jnp.sin(ang) * jnp.where((lane % MLA_ROPE_DIM) < MLA_ROPE_DIM // 2, -1.0, 1.0)


def _rope_tables(positions):
    b, t = positions.shape
    half_d = HEAD_DIM // 2
    half_m = MLA_ROPE_DIM // 2
    inv_d = ROPE_THETA ** (-jnp.arange(half_d, dtype=F32) / half_d)
    inv_m = ROPE_THETA ** (-jnp.arange(half_m, dtype=F32) / half_m)
    fd = jnp.tile(inv_d, LANES // half_d)[None, :]
    fm = jnp.tile(inv_m, LANES // half_m)[None, :]
    tab = jax.ShapeDtypeStruct((b, t, LANES), F32)
    tab_spec = pl.BlockSpec((None, t, LANES), lambda i: (i, 0, 0))
    vec_spec = pl.BlockSpec((1, LANES), lambda i: (0, 0))
    return pl.pallas_call(
        _rope_tab_kernel,
        grid=(b,),
        in_specs=[pl.BlockSpec((None, t, 1), lambda i: (i, 0, 0)), vec_spec, vec_spec],
        out_specs=[tab_spec] * 4,
        out_shape=[tab] * 4,
        name="rope_tables",
        compiler_params=_cparams(("parallel",)),
    )(positions.reshape(b, t, 1), fd, fm)


def _swap_halves(x, width):
    lane = _lane_iota()
    half = width // 2
    return jnp.where((lane % width) < half,
                     pltpu.roll(x, LANES - half, axis=1),
                     pltpu.roll(x, half, axis=1))


def _rope(x, cos, sin_signed, width):
    return x * cos + _swap_halves(x, width) * sin_signed


def _col_chunks(width, step=512):
    return [(c, min(step, width - c)) for c in range(0, width, step)]


def _inproj_kernel(x_ref, g_ref, w_ref, ws_ref, o_ref, os_ref):
    h = _rms(x_ref[...], g_ref[...]).astype(BF16)
    for c, n in _col_chunks(MAIN_WIDTH):
        o_ref[:, c:c + n] = _dot(h, w_ref[:, c:c + n]).astype(BF16)
    os_ref[...] = _dot(h, ws_ref[...])


def _inproj(x2, gain, w_main, w_small, tm=512):
    n = x2.shape[0]
    return pl.pallas_call(
        _inproj_kernel,
        grid=(n // tm,),
        in_specs=[
            pl.BlockSpec((tm, D_MODEL), lambda i: (i, 0)),
            pl.BlockSpec((1, D_MODEL), lambda i: (0, 0)),
            pl.BlockSpec((D_MODEL, MAIN_WIDTH), lambda i: (0, 0)),
            pl.BlockSpec((D_MODEL, SMALL_WIDTH), lambda i: (0, 0)),
        ],
        out_specs=[
            pl.BlockSpec((tm, MAIN_WIDTH), lambda i: (i, 0)),
            pl.BlockSpec((tm, SMALL_WIDTH), lambda i: (i, 0)),
        ],
        out_shape=[
            jax.ShapeDtypeStruct((n, MAIN_WIDTH), BF16),
            jax.ShapeDtypeStruct((n, SMALL_WIDTH), F32),
        ],
        name="inproj",
        compiler_params=_cparams(("parallel",)),
    )(x2, gain, w_main, w_small)


CUMSUM_BLOCK = 256
DECAY_PART_STRIDE = 8
DECAY_PARTS = 3


def _split3(a):
    hi = a.astype(BF16)
    r1 = a - hi.astype(F32)
    mid = r1.astype(BF16)
    lo = (r1 - mid.astype(F32)).astype(BF16)
    return hi, mid, lo


def _decay_kernel(z_ref, b_ref, o_ref):
    t = z_ref.shape[0]
    row = lax.broadcasted_iota(jnp.int32, (CUMSUM_BLOCK, CUMSUM_BLOCK), 0)
    col = lax.broadcasted_iota(jnp.int32, (CUMSUM_BLOCK, CUMSUM_BLOCK), 1)
    tri = jnp.where(col <= row, 1.0, 0.0).astype(BF16)
    is_head = _lane_iota() < FOX_HEADS
    carry = jnp.zeros((1, LANES), F32)
    for blk in range(t // CUMSUM_BLOCK):
        sl = slice(blk * CUMSUM_BLOCK, (blk + 1) * CUMSUM_BLOCK)
        z = z_ref[sl, :] + b_ref[...]
        logf = -(jnp.maximum(-z, 0.0) + jnp.log1p(jnp.exp(-jnp.abs(z))))
        hi, mid, lo = _split3(logf)
        cs = _dot(tri, hi) + _dot(tri, mid) + _dot(tri, lo) + carry
        carry = cs[CUMSUM_BLOCK - 1:CUMSUM_BLOCK, :]
        parts = _split3(jnp.where(is_head, cs, 0.0))
        packed = parts[0].astype(F32)
        for j in range(1, DECAY_PARTS):
            packed = packed + pltpu.roll(parts[j].astype(F32), j * DECAY_PART_STRIDE, axis=1)
        o_ref[sl, :] = packed.astype(BF16)


def _decay(small3, bias):
    b, t, _ = small3.shape
    return pl.pallas_call(
        _decay_kernel,
        grid=(b,),
        in_specs=[pl.BlockSpec((None, t, LANES), lambda i: (i, 0, 0)),
                  pl.BlockSpec((1, LANES), lambda i: (0, 0))],
        out_specs=pl.BlockSpec((None, t, LANES), lambda i: (i, 0, 0)),
        out_shape=jax.ShapeDtypeStruct((b, t, LANES), BF16),
        name="decay",
        compiler_params=_cparams(("parallel",)),
    )(small3, bias)


def _pair_softmax(s, m_prev, ok_cols):
    nc = s.shape[1] // LANES
    cols = [s[:, c * LANES:(c + 1) * LANES] for c in range(nc)]
    if ok_cols is not None:
        cols = [jnp.where(ok, col, NEG_INF) for ok, col in zip(ok_cols, cols)]
    cmax = cols[0]
    for col in cols[1:]:
        cmax = jnp.maximum(cmax, col)
    m_new = jnp.maximum(m_prev, jnp.max(cmax, axis=1, keepdims=True))
    alpha = jnp.exp(m_prev - m_new)
    p = jnp.concatenate([jnp.exp(col - m_new).astype(BF16) for col in cols], axis=1)
    return m_new, alpha, p


def _causal_pair_attention(qs, keys_at, vals_at, qi, o_ref, tq):
    lane = _lane_iota()
    head0 = lane < HEAD_DIM
    rows = 2 * tq

    def scores(j):
        return _dot_nt(qs, keys_at(j))

    def weighted_values(p, j):
        v = vals_at(j)
        one = jnp.ones_like(v)
        pv0 = _dot(p[:tq], jnp.where(head0, v, one))
        pv1 = _dot(p[tq:], jnp.where(head0, one, v))
        return jnp.concatenate([pv0, pv1], axis=0)

    def body(j, carry):
        s, m_prev, acc = carry
        s_next = scores(j + 1)
        m_new, alpha, p = _pair_softmax(s, m_prev, None)
        return s_next, m_new, alpha * acc + weighted_values(p, j)

    init = (scores(0), jnp.full((rows, LANES), NEG_INF, F32), jnp.zeros((rows, LANES), F32))
    s, m_prev, acc = lax.fori_loop(0, qi, body, init)

    r_i = lax.broadcasted_iota(jnp.int32, (tq, LANES), 0)
    c_i = lax.broadcasted_iota(jnp.int32, (tq, LANES), 1)
    ok_cols = []
    for c in range(tq // LANES):
        ok = c_i + c * LANES <= r_i
        ok_cols.append(jnp.concatenate([ok, ok], axis=0))
    _, alpha, p = _pair_softmax(s, m_prev, ok_cols)
    acc = alpha * acc + weighted_values(p, qi)
    out = acc / pltpu.roll(acc, HEAD_DIM, axis=1)
    o_ref[...] = jnp.where(head0, out[:tq], out[tq:]).astype(o_ref.dtype)


def _block_rows(j, tq):
    return pl.ds(pl.multiple_of(j * tq, tq), tq)


def _fox_kernel(q_ref, k_ref, v_ref, d_ref, o_ref, *, tq):
    p = pl.program_id(1)
    qi = pl.program_id(2)
    q = q_ref[...]
    lane = _lane_iota()
    zero = jnp.zeros_like(q)
    halves = []
    for h in range(2):
        head = 2 * p + h
        pick = (lane % DECAY_PART_STRIDE == head) & (lane < DECAY_PARTS * DECAY_PART_STRIDE)
        neg = jnp.broadcast_to(jnp.where(pick, -1.0, 0.0).astype(BF16), q.shape)
        q_h = jnp.where((lane // HEAD_DIM) == h, q, zero)
        halves.append(jnp.concatenate([q_h, neg], axis=1))
    qs = jnp.concatenate(halves, axis=0)

    def keys_at(j):
        rows = _block_rows(j, tq)
        return jnp.concatenate([k_ref[rows, :], d_ref[rows, :]], axis=1)

    def vals_at(j):
        return v_ref[_block_rows(j, tq), :]

    _causal_pair_attention(qs, keys_at, vals_at, qi, o_ref, tq)


def _fox_attention(main3, dcols, tq=512):
    b, t, _ = main3.shape
    cb = COL_FOX // LANES
    npair = FOX_HEADS // 2

    def seq(col):
        return pl.BlockSpec((None, t, LANES), lambda bi, p, qi: (bi, 0, col(p)))

    return pl.pallas_call(
        functools.partial(_fox_kernel, tq=tq),
        grid=(b, npair, t // tq),
        in_specs=[
            pl.BlockSpec((None, tq, LANES), lambda bi, p, qi: (bi, qi, cb + p)),
            seq(lambda p: cb + npair + p),
            seq(lambda p: cb + 2 * npair + p),
            seq(lambda p: 0),
        ],
        out_specs=pl.BlockSpec((None, tq, LANES), lambda bi, p, qi: (bi, qi, p)),
        out_shape=jax.ShapeDtypeStruct((b, t, FOX_WIDTH), BF16),
        name="fox_attention",
        compiler_params=_cparams(("parallel", "parallel", "arbitrary")),
    )(main3, main3, main3, dcols)


def _mla_pre_kernel(cq_ref, ckv_ref, kr_ref, cos_ref, sin_ref, gq_ref, gkv_ref, wq_ref, wkv_ref,
                    qn_ref, qr_ref, kn_ref, kr_out_ref, v_ref, *, scale):
    cos = cos_ref[...]
    sin = sin_ref[...]
    nope = MLA_HEADS * MLA_NOPE_DIM
    hq = _rms(cq_ref[...].astype(F32), gq_ref[...]).astype(BF16)
    q = _dot(hq, wq_ref[...])
    qn_ref[...] = (q[:, :nope] * scale).astype(BF16)
    qr_ref[...] = (_rope(q[:, nope:], cos, sin, MLA_ROPE_DIM) * scale).astype(BF16)
    hkv = _rms(ckv_ref[...].astype(F32), gkv_ref[...]).astype(BF16)
    kv = _dot(hkv, wkv_ref[...])
    kn_ref[...] = kv[:, :nope].astype(BF16)
    v_ref[...] = kv[:, nope:].astype(BF16)
    kr_out_ref[...] = _rope(kr_ref[...], cos, sin, MLA_ROPE_DIM).astype(BF16)


def _mla_pre(main3, small3, cos_m, sin_m, gq, gkv, wq, wkv, tm=512):
    b, t, _ = main3.shape
    cq_blk = COL_MLA // MLA_Q_RANK
    nope = MLA_HEADS * MLA_NOPE_DIM
    vw = MLA_HEADS * MLA_V_DIM

    def tok(width, col=0):
        return pl.BlockSpec((None, tm, width), lambda bi, i: (bi, i, col))

    def full(shape):
        return pl.BlockSpec(shape, lambda bi, i: (0,) * len(shape))

    return pl.pallas_call(
        functools.partial(_mla_pre_kernel, scale=(MLA_NOPE_DIM + MLA_ROPE_DIM) ** -0.5),
        grid=(b, t // tm),
        in_specs=[tok(MLA_Q_RANK, cq_blk), tok(MLA_KV_RANK, cq_blk + 1), tok(LANES, 1),
                  tok(LANES), tok(LANES), full((1, MLA_Q_RANK)), full((1, MLA_KV_RANK)),
                  full(wq.shape), full(wkv.shape)],
        out_specs=[tok(nope), tok(LANES), tok(nope), tok(LANES), tok(vw)],
        out_shape=[jax.ShapeDtypeStruct((b, t, nope), BF16),
                   jax.ShapeDtypeStruct((b, t, LANES), BF16),
                   jax.ShapeDtypeStruct((b, t, nope), BF16),
                   jax.ShapeDtypeStruct((b, t, LANES), BF16),
                   jax.ShapeDtypeStruct((b, t, vw), BF16)],
        name="mla_pre",
        compiler_params=_cparams(("parallel", "parallel")),
    )(main3, main3, small3, cos_m, sin_m, gq, gkv, wq, wkv)


def _mla_kernel(qn_ref, qr_ref, kn_ref, kr_ref, v_ref, o_ref, *, tq):
    p = pl.program_id(1)
    qi = pl.program_id(2)
    qn = qn_ref[...]
    qr = qr_ref[...]
    lane = _lane_iota()
    zero = jnp.zeros_like(qn)
    halves = []
    for h in range(2):
        nope_h = jnp.where((lane // MLA_NOPE_DIM) == h, qn, zero)
        rope_h = jnp.where((lane // MLA_ROPE_DIM) == 2 * p + h, qr, zero)
        halves.append(jnp.concatenate([nope_h, rope_h], axis=1))
    qs = jnp.concatenate(halves, axis=0)

    def keys_at(j):
        rows = _block_rows(j, tq)
        return jnp.concatenate([kn_ref[rows, :], kr_ref[rows, :]], axis=1)

    def vals_at(j):
        return v_ref[_block_rows(j, tq), :]

    _causal_pair_attention(qs, keys_at, vals_at, qi, o_ref, tq)


def _mla_attention(qn, qr, kn, kr4, vv, tq=512):
    b, t, _ = qn.shape
    npair = MLA_HEADS // 2
    q_pair = pl.BlockSpec((None, tq, LANES), lambda bi, p, qi: (bi, qi, p))
    q_all = pl.BlockSpec((None, tq, LANES), lambda bi, p, qi: (bi, qi, 0))
    k_pair = pl.BlockSpec((None, t, LANES), lambda bi, p, qi: (bi, 0, p))
    k_all = pl.BlockSpec((None, t, LANES), lambda bi, p, qi: (bi, 0, 0))
    return pl.pallas_call(
        functools.partial(_mla_kernel, tq=tq),
        grid=(b, npair, t // tq),
        in_specs=[q_pair, q_all, k_pair, k_all, k_pair],
        out_specs=q_pair,
        out_shape=jax.ShapeDtypeStruct((b, t, MLA_WIDTH), BF16),
        name="mla_attention",
        compiler_params=_cparams(("parallel", "parallel", "arbitrary")),
    )(qn, qr, kn, kr4, vv)


def _dil_kernel(q_ref, k_ref, v_ref, cos_ref, sin_ref, o_ref,
                qf, kf, vf, qb, kb, vb, accb, mb, lb, tacc, tmx, tl):
    t = q_ref.shape[0]
    blk = DIL_BLOCK
    lane = _lane_iota()
    head0 = lane < HEAD_DIM
    cos = cos_ref[...]
    sin = sin_ref[...]
    qf[...] = _rope(q_ref[...].astype(F32), cos, sin, HEAD_DIM)
    kf[...] = _rope(k_ref[...].astype(F32), cos, sin, HEAD_DIM)
    vf[...] = v_ref[...].astype(F32)

    for g, (_, rate) in enumerate(DIL_PAIRS):
        length = t // rate
        kb[g, 0:blk, :] = jnp.zeros((blk, LANES), BF16)
        vb[g, 0:blk, :] = jnp.zeros((blk, LANES), BF16)
        for res in range(rate):
            dst = slice(blk + res * length, blk + (res + 1) * length)
            if rate == 1:
                src = slice(None)
            else:
                src = pl.ds(res, length, stride=rate)
            qb[g, dst, :] = qf[src, :].astype(BF16)
            kb[g, dst, :] = kf[src, :].astype(BF16)
            vb[g, dst, :] = vf[src, :].astype(BF16)

    qi2 = lax.broadcasted_iota(jnp.int32, (blk, 2 * blk), 0)
    kj2 = lax.broadcasted_iota(jnp.int32, (blk, 2 * blk), 1)
    band = (kj2 >= qi2) & (kj2 <= qi2 + blk)
    bias_full = jnp.where(band, 0.0, NEG_INF)
    bias_first = jnp.where(band & (kj2 >= blk), 0.0, NEG_INF)
    qi1 = lax.broadcasted_iota(jnp.int32, (blk, blk), 0)
    kj1 = lax.broadcasted_iota(jnp.int32, (blk, blk), 1)
    bias_single = jnp.where(kj1 <= qi1, 0.0, NEG_INF)

    for g, (window, rate) in enumerate(DIL_PAIRS):
        assert window // rate == blk
        nb = t // rate // blk
        nblocks = t // blk

        def body(j, carry, g=g, nb=nb):
            base = pl.multiple_of(j * blk, blk)
            q = qb[g, pl.ds(base + blk, blk), :]
            zero = jnp.zeros_like(q)
            qs = jnp.concatenate([jnp.where(head0, q, zero), jnp.where(head0, zero, q)], axis=0)
            if nb == 1:
                keys = kb[g, pl.ds(base + blk, blk), :]
                vals = vb[g, pl.ds(base + blk, blk), :]
                bias = bias_single
            else:
                keys = kb[g, pl.ds(base, 2 * blk), :]
                vals = vb[g, pl.ds(base, 2 * blk), :]
                bias = jnp.where((j % nb) != 0, bias_full, bias_first)
            s = _dot_nt(qs, keys)
            ps, ms, ls = [], [], []
            for h in range(2):
                sh = s[h * blk:(h + 1) * blk] + bias
                m = jnp.max(sh, axis=1, keepdims=True)
                p = jnp.exp(sh - m)
                ps.append(p.astype(BF16))
                ms.append(m)
                ls.append(jnp.sum(p, axis=1, keepdims=True))
            pv = _dot(jnp.concatenate(ps, axis=0), vals)
            rows = pl.ds(base, blk)
            accb[g, rows, :] = jnp.where(head0, pv[:blk], pv[blk:])
            mb[g, rows, :] = jnp.where(head0, ms[0], ms[1])
            lb[g, rows, :] = jnp.where(head0, ls[0], ls[1])
            return carry

        lax.fori_loop(0, nblocks, body, 0)

    for g, (_, rate) in enumerate(DIL_PAIRS):
        if rate == 1:
            continue
        length = t // rate
        for res in range(rate):
            src = slice(res * length, (res + 1) * length)
            dst = pl.ds(res, length, stride=rate)
            tacc[g - 1, dst, :] = accb[g, src, :]
            tmx[g - 1, dst, :] = mb[g, src, :]
            tl[g - 1, dst, :] = lb[g, src, :]

    chunk = 256
    for c in range(t // chunk):
        sl = slice(c * chunk, (c + 1) * chunk)
        m_all = [mb[0, sl, :], tmx[0, sl, :], tmx[1, sl, :]]
        l_all = [lb[0, sl, :], tl[0, sl, :], tl[1, sl, :]]
        a_all = [accb[0, sl, :], tacc[0, sl, :], tacc[1, sl, :]]
        m_max = jnp.maximum(jnp.maximum(m_all[0], m_all[1]), m_all[2])
        ws = [jnp.exp(m - m_max) for m in m_all]
        num = ws[0] * a_all[0] + ws[1] * a_all[1] + ws[2] * a_all[2]
        den = ws[0] * l_all[0] + ws[1] * l_all[1] + ws[2] * l_all[2]
        o_ref[sl, :] = (num / den).astype(o_ref.dtype)


def _dil_attention(main3, cos_d, sin_d):
    b, t, _ = main3.shape
    assert all(t % (rate * DIL_BLOCK) == 0 for _, rate in DIL_PAIRS)
    cb = COL_DIL // LANES
    npair = DIL_HEADS // 2
    nbr = len(DIL_PAIRS)

    def col(off):
        return pl.BlockSpec((None, t, LANES), lambda bi, p: (bi, 0, cb + off + p))

    tab = pl.BlockSpec((None, t, LANES), lambda bi, p: (bi, 0, 0))
    return pl.pallas_call(
        _dil_kernel,
        grid=(b, npair),
        in_specs=[col(0), col(npair), col(2 * npair), tab, tab],
        out_specs=pl.BlockSpec((None, t, LANES), lambda bi, p: (bi, 0, p)),
        out_shape=jax.ShapeDtypeStruct((b, t, DIL_WIDTH), BF16),
        name="dilated_attention",
        scratch_shapes=[pltpu.VMEM((t, LANES), F32)] * 3
        + [pltpu.VMEM((nbr, t + DIL_BLOCK, LANES), BF16)] * 3
        + [pltpu.VMEM((nbr, t, LANES), F32)] * 3
        + [pltpu.VMEM((nbr - 1, t, LANES), F32)] * 3,
        compiler_params=_cparams(("parallel", "parallel")),
    )(main3, main3, main3, cos_d, sin_d)


def _merge_kernel(x_ref, gf_ref, gd_ref, gm_ref, of_ref, od_ref, om_ref,
                  wf_ref, wd_ref, wm_ref, wo_ref, o_ref):
    merged = (jax.nn.sigmoid(gf_ref[...].astype(F32)) * _dot(of_ref[...], wf_ref[...])
              + jax.nn.sigmoid(gd_ref[...].astype(F32)) * _dot(od_ref[...], wd_ref[...])
              + jax.nn.sigmoid(gm_ref[...].astype(F32)) * _dot(om_ref[...], wm_ref[...]))
    o_ref[...] = x_ref[...] + _dot(merged.astype(BF16), wo_ref[...])


def _merge(x2, main2, o_fox, o_dil, o_mla, wf, wd, wm, wo, tm=512):
    n = x2.shape[0]

    def tok(width, col=0):
        return pl.BlockSpec((tm, width), lambda i: (i, col))

    def full(w):
        return pl.BlockSpec(w.shape, lambda i: (0, 0))

    return pl.pallas_call(
        _merge_kernel,
        grid=(n // tm,),
        in_specs=[tok(D_MODEL), tok(D_MODEL, 0), tok(D_MODEL, 1), tok(D_MODEL, 2),
                  tok(FOX_WIDTH), tok(DIL_WIDTH), tok(MLA_WIDTH),
                  full(wf), full(wd), full(wm), full(wo)],
        out_specs=tok(D_MODEL),
        out_shape=jax.ShapeDtypeStruct((n, D_MODEL), F32),
        name="merge_outproj",
        compiler_params=_cparams(("parallel",)),
    )(x2, main2, main2, main2, o_fox, o_dil, o_mla, wf, wd, wm, wo)


def _ffn_kernel(x_ref, g_ref, wg_ref, wu_ref, wd_ref, o_ref, h_sc, acc_sc):
    c = pl.program_id(1)

    @pl.when(c == 0)
    def _():
        x = x_ref[...]
        h_sc[...] = _rms(x, g_ref[...]).astype(BF16)
        acc_sc[...] = x

    h = h_sc[...]
    gate = _dot(h, wg_ref[...])
    up = _dot(h, wu_ref[...])
    act = gate * jax.nn.sigmoid(gate) * up
    acc_sc[...] += _dot(act.astype(BF16), wd_ref[...])

    @pl.when(c == pl.num_programs(1) - 1)
    def _():
        o_ref[...] = acc_sc[...]


def _ffn(x2, gain, wg, wu, wd, tm=512):
    n = x2.shape[0]
    tok = pl.BlockSpec((tm, D_MODEL), lambda i, c: (i, 0))
    return pl.pallas_call(
        _ffn_kernel,
        grid=(n // tm, D_FF // D_FF_EXPERT),
        in_specs=[tok, pl.BlockSpec((1, D_MODEL), lambda i, c: (0, 0)),
                  pl.BlockSpec((D_MODEL, D_FF_EXPERT), lambda i, c: (0, c)),
                  pl.BlockSpec((D_MODEL, D_FF_EXPERT), lambda i, c: (0, c)),
                  pl.BlockSpec((D_FF_EXPERT, D_MODEL), lambda i, c: (c, 0))],
        out_specs=tok,
        out_shape=jax.ShapeDtypeStruct((n, D_MODEL), F32),
        name="ffn_dense",
        scratch_shapes=[pltpu.VMEM((tm, D_MODEL), BF16), pltpu.VMEM((tm, D_MODEL), F32)],
        compiler_params=_cparams(("parallel", "arbitrary")),
    )(x2, gain, wg, wu, wd)


MOE_BLOCK = 2048
MOE_TILE = 256
MOE_TILES = -(-(2 * MOE_BLOCK + N_EXPERTS * (MOE_TILE - 1)) // MOE_TILE)
MOE_META_ROWS = 32
assert MOE_TILES <= MOE_META_ROWS
SCATTER_CHUNK = 256
MOE_VMEM_LIMIT = 60 * 1024 * 1024


def _router_kernel(x_ref, g_ref, wh_ref, wm_ref, wl_ref, h_ref, rc_ref, meta_ref):
    h = _rms(x_ref[...], g_ref[...])
    h_ref[...] = h.astype(BF16)
    h_hi, h_mid, h_lo = _split3(h)
    w_hi, w_mid, w_lo = wh_ref[...], wm_ref[...], wl_ref[...]
    logits = (_dot(h_hi, w_hi) + (_dot(h_hi, w_mid) + _dot(h_mid, w_hi))
              + (_dot(h_hi, w_lo) + _dot(h_mid, w_mid) + _dot(h_lo, w_hi)))
    lane = _lane_iota()
    is_expert = lane < N_EXPERTS
    logits = jnp.where(is_expert, logits, NEG_INF)
    v1 = jnp.max(logits, axis=1, keepdims=True)
    i1 = jnp.min(jnp.where(logits == v1, lane, LANES), axis=1, keepdims=True)
    first = lane == i1
    rest = jnp.where(first, NEG_INF, logits)
    v2 = jnp.max(rest, axis=1, keepdims=True)
    i2 = jnp.min(jnp.where(rest == v2, lane, LANES), axis=1, keepdims=True)
    second = lane == i2
    e2 = jnp.exp(v2 - v1)
    w1 = 1.0 / (1.0 + e2)
    w2 = e2 / (1.0 + e2)

    sel = jnp.where(first, 1.0, jnp.where(second, 1.0, 0.0))
    row = lax.broadcasted_iota(jnp.int32, (CUMSUM_BLOCK, CUMSUM_BLOCK), 0)
    col = lax.broadcasted_iota(jnp.int32, (CUMSUM_BLOCK, CUMSUM_BLOCK), 1)
    tri = jnp.where(col < row, 1.0, 0.0).astype(BF16)
    carry = jnp.zeros((1, LANES), F32)
    ranks = []
    for blk in range(sel.shape[0] // CUMSUM_BLOCK):
        part = sel[blk * CUMSUM_BLOCK:(blk + 1) * CUMSUM_BLOCK]
        ranks.append(_dot(tri, part.astype(BF16)) + carry)
        carry = carry + jnp.sum(part, axis=0, keepdims=True)
    rank = jnp.concatenate(ranks, axis=0)
    padded = jnp.ceil(carry / MOE_TILE) * MOE_TILE
    er = lax.broadcasted_iota(jnp.int32, (LANES, LANES), 0)
    ec = lax.broadcasted_iota(jnp.int32, (LANES, LANES), 1)
    before = jnp.where(er < ec, 1.0, 0.0).astype(BF16)
    start = _dot(jnp.broadcast_to(padded, (8, LANES)).astype(BF16), before)[0:1]
    slot = start + rank
    dest1 = jnp.sum(jnp.where(first, slot, 0.0), axis=1, keepdims=True)
    dest2 = jnp.sum(jnp.where(second, slot, 0.0), axis=1, keepdims=True)
    rc_ref[...] = jnp.where(lane == 0, dest1, jnp.where(lane == 1, dest2,
                            jnp.where(lane == 2, w1, jnp.where(lane == 3, w2, 0.0))))

    end = start + padded
    tile_row = lax.broadcasted_iota(jnp.int32, (MOE_META_ROWS, LANES), 0).astype(F32) * MOE_TILE
    passed = jnp.sum(jnp.where(is_expert, jnp.where(tile_row >= end, 1.0, 0.0), 0.0),
                     axis=1, keepdims=True)
    last = jnp.max(jnp.where(is_expert, jnp.where(padded > 0, lane.astype(F32), 0.0), 0.0),
                   axis=1, keepdims=True)
    total = jnp.sum(jnp.where(is_expert, padded, 0.0), axis=1, keepdims=True)
    active = jnp.where(tile_row < total, 1.0, 0.0)
    meta = jnp.where(lane == 0, jnp.minimum(passed, last), jnp.where(lane == 1, active, 0.0))
    meta_ref[...] = meta.astype(jnp.int32)


def _router(x2, gain, w_router):
    n = x2.shape[0]
    nblk = n // MOE_BLOCK
    w_pad = jnp.zeros((D_MODEL, LANES), F32).at[:, :N_EXPERTS].set(w_router)
    w_hi = w_pad.astype(BF16)
    r1 = w_pad - w_hi.astype(F32)
    w_mid = r1.astype(BF16)
    w_lo = (r1 - w_mid.astype(F32)).astype(BF16)
    wspec = pl.BlockSpec((D_MODEL, LANES), lambda i: (0, 0))
    return pl.pallas_call(
        _router_kernel,
        grid=(nblk,),
        in_specs=[pl.BlockSpec((MOE_BLOCK, D_MODEL), lambda i: (i, 0)),
                  pl.BlockSpec((1, D_MODEL), lambda i: (0, 0)), wspec, wspec, wspec],
        out_specs=[pl.BlockSpec((MOE_BLOCK, D_MODEL), lambda i: (i, 0)),
                   pl.BlockSpec((MOE_BLOCK, LANES), lambda i: (i, 0)),
                   pl.BlockSpec((None, MOE_META_ROWS, LANES), lambda i: (i, 0, 0))],
        out_shape=[jax.ShapeDtypeStruct((n, D_MODEL), BF16),
                   jax.ShapeDtypeStruct((n, LANES), F32),
                   jax.ShapeDtypeStruct((nblk, MOE_META_ROWS, LANES), jnp.int32)],
        name="router",
        compiler_params=_cparams(("parallel",)),
    )(x2, gain, w_hi, w_mid, w_lo)


def _moe_kernel(te_ref, ta_ref, h_ref, rrow_ref, rcol_ref, wg_ref, wu_ref, wd_ref, o_ref):
    b = pl.program_id(0)
    r = pl.program_id(1)

    @pl.when(r == 0)
    def _():
        o_ref[...] = jnp.zeros(o_ref.shape, F32)

    @pl.when(ta_ref[b, r] == 1)
    def _():
        base = (r * MOE_TILE).astype(F32)
        slot_col = lax.broadcasted_iota(jnp.int32, (MOE_TILE, 1), 0).astype(F32) + base
        hit1 = rrow_ref[0:1, :] == slot_col
        hit2 = rrow_ref[1:2, :] == slot_col
        onehot = jnp.where(hit1, 1.0, jnp.where(hit2, 1.0, 0.0)).astype(BF16)
        weight = jnp.sum(jnp.where(hit1, rrow_ref[2:3, :], jnp.where(hit2, rrow_ref[3:4, :], 0.0)),
                         axis=1, keepdims=True)
        xg = _dot(onehot, h_ref[...]).astype(BF16)
        gate = _dot(xg, wg_ref[...])
        up = _dot(xg, wu_ref[...])
        act = gate * jax.nn.sigmoid(gate) * up
        y = (_dot(act.astype(BF16), wd_ref[...]) * weight).astype(BF16)
        slot_row = lax.broadcasted_iota(jnp.int32, (1, MOE_TILE), 1).astype(F32) + base
        back = jnp.where(rcol_ref[:, 0:1] == slot_row, 1.0,
                         jnp.where(rcol_ref[:, 1:2] == slot_row, 1.0, 0.0)).astype(BF16)
        for c in range(0, D_MODEL, SCATTER_CHUNK):
            o_ref[:, c:c + SCATTER_CHUNK] += _dot(back, y[:, c:c + SCATTER_CHUNK])


def _moe(h2, rrow, rcol, tile_expert, tile_active, wg, wu, wd):
    n = h2.shape[0]
    nblk = n // MOE_BLOCK

    def wspec(shape):
        return pl.BlockSpec((None,) + shape, lambda b, r, te, ta: (te[b, r], 0, 0))

    grid_spec = pltpu.PrefetchScalarGridSpec(
        num_scalar_prefetch=2,
        grid=(nblk, MOE_TILES),
        in_specs=[pl.BlockSpec((MOE_BLOCK, D_MODEL), lambda b, r, te, ta: (b, 0)),
                  pl.BlockSpec((None, 8, MOE_BLOCK), lambda b, r, te, ta: (b, 0, 0)),
                  pl.BlockSpec((MOE_BLOCK, LANES), lambda b, r, te, ta: (b, 0)),
                  wspec((D_MODEL, D_FF_EXPERT)), wspec((D_MODEL, D_FF_EXPERT)),
                  wspec((D_FF_EXPERT, D_MODEL))],
        out_specs=pl.BlockSpec((MOE_BLOCK, D_MODEL), lambda b, r, te, ta: (b, 0)),
    )
    return pl.pallas_call(
        _moe_kernel,
        grid_spec=grid_spec,
        out_shape=jax.ShapeDtypeStruct((n, D_MODEL), F32),
        name="moe_experts",
        compiler_params=pltpu.CompilerParams(dimension_semantics=("parallel", "arbitrary"),
                                             vmem_limit_bytes=MOE_VMEM_LIMIT),
    )(tile_expert, tile_active, h2, rrow, rcol, wg, wu, wd)


def _moe_layer(x2, gain, w_router, wg, wu, wd):
    n = x2.shape[0]
    nblk = n // MOE_BLOCK
    h2, rcol, meta = _router(x2, gain, w_router)
    rrow = jnp.zeros((nblk, 8, MOE_BLOCK), F32).at[:, :4, :].set(
        rcol[:, :4].reshape(nblk, MOE_BLOCK, 4).transpose(0, 2, 1))
    tile_expert = meta[:, :MOE_TILES, 0]
    tile_active = meta[:, :MOE_TILES, 1]
    return _moe(h2, rrow, rcol, tile_expert, tile_active, wg, wu, wd)


def _ple_kernel(*refs, final, has_delta):
    if has_delta:
        x_ref, d_ref, p_ref, g_ref, wg_ref, wp_ref, fg_ref, o_ref = refs
        x = x_ref[...] + d_ref[...]
    else:
        x_ref, p_ref, g_ref, wg_ref, wp_ref, fg_ref, o_ref = refs
        x = x_ref[...]
    h = _rms(x, g_ref[...]).astype(BF16)
    gate = jax.nn.sigmoid(_dot(h, wg_ref[...]))
    y = x + gate * _dot(p_ref[...].astype(BF16), wp_ref[...])
    if final:
        y = _rms(y, fg_ref[...])
    o_ref[...] = y


def _ple(x2, delta, p2, gain, wg, wp, final_gain, final, tm=512):
    n = x2.shape[0]
    vec = pl.BlockSpec((1, D_MODEL), lambda i: (0, 0))
    tok = pl.BlockSpec((tm, D_MODEL), lambda i: (i, 0))
    acts = [x2] if delta is None else [x2, delta]
    return pl.pallas_call(
        functools.partial(_ple_kernel, final=final, has_delta=delta is not None),
        grid=(n // tm,),
        in_specs=[tok] * len(acts) + [
                  pl.BlockSpec((tm, PLE_DIM), lambda i: (i, 0)), vec,
                  pl.BlockSpec(wg.shape, lambda i: (0, 0)),
                  pl.BlockSpec(wp.shape, lambda i: (0, 0)), vec],
        out_specs=pl.BlockSpec((tm, D_MODEL), lambda i: (i, 0)),
        out_shape=jax.ShapeDtypeStruct((n, D_MODEL), F32),
        name="ple_final" if final else "ple",
        compiler_params=_cparams(("parallel",)),
    )(*acts, p2, gain, wg, wp, final_gain)


def _prep_in_weights(w_in):
    sizes = (FOX_WIDTH, FOX_WIDTH, FOX_WIDTH, FOX_HEADS, DIL_WIDTH, DIL_WIDTH, DIL_WIDTH,
             MLA_Q_RANK, MLA_KV_RANK, MLA_ROPE_DIM, GATE_WIDTH)
    offs = [0]
    for s in sizes:
        offs.append(offs[-1] + s)
    fq, fk, fv, fz, dq, dk, dv, cq, ckv, kr, gates = [w_in[..., offs[i]:offs[i + 1]]
                                                      for i in range(len(sizes))]
    scale = HEAD_DIM ** -0.5
    main = jnp.concatenate([gates, fq * scale, fk, fv, dq * scale, dk, dv, cq, ckv], axis=-1)
    pad = jnp.zeros(w_in.shape[:-1] + (LANES - FOX_HEADS,), w_in.dtype)
    small = jnp.concatenate([fz, pad] + [kr] * MLA_HEADS, axis=-1)
    return main.astype(BF16), small.astype(BF16)


def _prep_mla_weights(w_uq, w_ukv):
    depth = w_uq.shape[0]
    uq = w_uq.reshape(depth, MLA_Q_RANK, MLA_HEADS, MLA_NOPE_DIM + MLA_ROPE_DIM)
    wq = jnp.concatenate([uq[..., :MLA_NOPE_DIM].reshape(depth, MLA_Q_RANK, -1),
                          uq[..., MLA_NOPE_DIM:].reshape(depth, MLA_Q_RANK, -1)], axis=-1)
    ukv = w_ukv.reshape(depth, MLA_KV_RANK, MLA_HEADS, MLA_NOPE_DIM + MLA_V_DIM)
    wkv = jnp.concatenate([ukv[..., :MLA_NOPE_DIM].reshape(depth, MLA_KV_RANK, -1),
                           ukv[..., MLA_NOPE_DIM:].reshape(depth, MLA_KV_RANK, -1)], axis=-1)
    return wq.astype(BF16), wkv.astype(BF16)


def kernel(x, p, positions, mix_norm, w_in, b_forget, mla_q_norm, mla_kv_norm, w_uq, w_ukv,
           w_br_fox, w_br_dil, w_br_mla, w_out, ffn_norm, w_ffn_gate, w_ffn_up, w_ffn_down,
           w_router, w_exp_gate, w_exp_up, w_exp_down, ple_norm, w_ple_gate, w_ple_proj,
           final_norm):
    b, t, _ = x.shape
    n = b * t
    depth = w_in.shape[0]

    w_main, w_small = _prep_in_weights(w_in)
    wq_all, wkv_all = _prep_mla_weights(w_uq, w_ukv)
    bias_all = jnp.concatenate(
        [b_forget.astype(F32), jnp.zeros((depth, LANES - FOX_HEADS), F32)], axis=-1)
    cos_d, sin_d, cos_m, sin_m = _rope_tables(positions)
    final_gain = final_norm.reshape(1, D_MODEL)

    x2 = x.reshape(n, D_MODEL)
    for i in range(depth):
        main2, small2 = _inproj(x2, mix_norm[i].reshape(1, D_MODEL), w_main[i], w_small[i])
        main3 = main2.reshape(b, t, MAIN_WIDTH)
        small3 = small2.reshape(b, t, SMALL_WIDTH)

        dcols = _decay(small3, bias_all[i].reshape(1, LANES))
        o_fox = _fox_attention(main3, dcols)

        o_dil = _dil_attention(main3, cos_d, sin_d)

        qn, qr, kn, kr4, vv = _mla_pre(main3, small3, cos_m, sin_m,
                                       mla_q_norm[i].reshape(1, -1), mla_kv_norm[i].reshape(1, -1),
                                       wq_all[i], wkv_all[i])
        o_mla = _mla_attention(qn, qr, kn, kr4, vv)

        x2 = _merge(x2, main2, o_fox.reshape(n, -1), o_dil.reshape(n, -1), o_mla.reshape(n, -1),
                    w_br_fox[i].astype(BF16), w_br_dil[i].astype(BF16), w_br_mla[i].astype(BF16),
                    w_out[i].astype(BF16))

        j = i // 2
        gain = ffn_norm[i].reshape(1, D_MODEL)
        delta = None
        if i % 2 == 0:
            x2 = _ffn(x2, gain, w_ffn_gate[j].astype(BF16), w_ffn_up[j].astype(BF16),
                      w_ffn_down[j].astype(BF16))
        else:
            delta = _moe_layer(x2, gain, w_router[j], w_exp_gate[j].astype(BF16),
                               w_exp_up[j].astype(BF16), w_exp_down[j].astype(BF16))

        x2 = _ple(x2, delta, p[i].reshape(n, PLE_DIM), ple_norm[i].reshape(1, D_MODEL),
                  w_ple_gate[i].astype(BF16), w_ple_proj[i].astype(BF16), final_gain,
                  final=(i == depth - 1))
    return x2.reshape(b, t, D_MODEL)
```

```python
import functools

import jax
import jax.numpy as jnp
from jax import lax
from jax.experimental import pallas as pl
from jax.experimental.pallas import tpu as pltpu

F32 = jnp.float32
BF16 = jnp.bfloat16

D_MODEL = 1024
HEAD_DIM = 64
ROPE_THETA = 10000.0
NORM_EPS = 1e-6
FOX_HEADS = 6
FOX_WIDTH = FOX_HEADS * HEAD_DIM
DIL_HEADS = 6
DIL_WIDTH = DIL_HEADS * HEAD_DIM
DIL_PAIRS = ((128, 1), (512, 4), (2048, 16))
DIL_BLOCK = 128
MLA_HEADS = 4
MLA_Q_RANK = 256
MLA_KV_RANK = 256
MLA_NOPE_DIM = 64
MLA_ROPE_DIM = 32
MLA_V_DIM = 64
MLA_WIDTH = MLA_HEADS * MLA_V_DIM
N_BRANCHES = 3
D_FF = 2816
N_EXPERTS = 8
D_FF_EXPERT = 1408
PLE_DIM = 256

LANES = 128
GATE_WIDTH = N_BRANCHES * D_MODEL
COL_GATES = 0
COL_FOX = GATE_WIDTH
COL_DIL = COL_FOX + 3 * FOX_WIDTH
COL_MLA = COL_DIL + 3 * DIL_WIDTH
MAIN_WIDTH = COL_MLA + MLA_Q_RANK + MLA_KV_RANK
SMALL_WIDTH = 2 * LANES
VMEM_LIMIT = 56 * 1024 * 1024

NEG_INF = float("-inf")


def _cparams(sem):
    return pltpu.CompilerParams(dimension_semantics=sem, vmem_limit_bytes=VMEM_LIMIT)


def _rms(x, gain):
    return x * lax.rsqrt(jnp.mean(x * x, axis=-1, keepdims=True) + NORM_EPS) * gain


def _dot(a, b):
    return jnp.dot(a, b, preferred_element_type=F32)


def _dot_nt(a, b):
    return lax.dot_general(a, b, (((1,), (1,)), ((), ())), preferred_element_type=F32)


def _lane_iota():
    return lax.broadcasted_iota(jnp.int32, (1, LANES), 1)


def _rope_tab_kernel(pos_ref, fd_ref, fm_ref, cd_ref, sd_ref, cm_ref, sm_ref):
    pos = pos_ref[...].astype(F32)
    lane = _lane_iota()
    ang = pos * fd_ref[...]
    cd_ref[...] = jnp.cos(ang)
    sd_ref[...] = jnp.sin(ang) * jnp.where((lane % HEAD_DIM) < HEAD_DIM // 2, -1.0, 1.0)
    ang = pos * fm_ref[...]
    cm_ref[...] = jnp.cos(ang)
    sm_ref[...] = jnp.sin(ang) * jnp.where((lane % MLA_ROPE_DIM) < MLA_ROPE_DIM // 2, -1.0, 1.0)


def _rope_tables(positions):
    b, t = positions.shape
    half_d = HEAD_DIM // 2
    half_m = MLA_ROPE_DIM // 2
    inv_d = ROPE_THETA ** (-jnp.arange(half_d, dtype=F32) / half_d)
    inv_m = ROPE_THETA ** (-jnp.arange(half_m, dtype=F32) / half_m)
    fd = jnp.tile(inv_d, LANES // half_d)[None, :]
    fm = jnp.tile(inv_m, LANES // half_m)[None, :]
    tab = jax.ShapeDtypeStruct((b, t, LANES), F32)
    tab_spec = pl.BlockSpec((None, t, LANES), lambda i: (i, 0, 0))
    vec_spec = pl.BlockSpec((1, LANES), lambda i: (0, 0))
    return pl.pallas_call(
        _rope_tab_kernel,
        grid=(b,),
        in_specs=[pl.BlockSpec((None, t, 1), lambda i: (i, 0, 0)), vec_spec, vec_spec],
        out_specs=[tab_spec] * 4,
        out_shape=[tab] * 4,
        name="rope_tables",
        compiler_params=_cparams(("parallel",)),
    )(positions.reshape(b, t, 1), fd, fm)


def _swap_halves(x, width):
    lane = _lane_iota()
    half = width // 2
    return jnp.where((lane % width) < half,
                     pltpu.roll(x, LANES - half, axis=1),
                     pltpu.roll(x, half, axis=1))


def _rope(x, cos, sin_signed, width):
    return x * cos + _swap_halves(x, width) * sin_signed


def _col_chunks(width, step=512):
    return [(c, min(step, width - c)) for c in range(0, width, step)]


def _inproj_kernel(x_ref, g_ref, w_ref, ws_ref, o_ref, os_ref):
    h = _rms(x_ref[...], g_ref[...]).astype(BF16)
    for c, n in _col_chunks(MAIN_WIDTH):
        o_ref[:, c:c + n] = _dot(h, w_ref[:, c:c + n]).astype(BF16)
    os_ref[...] = _dot(h, ws_ref[...])


def _inproj(x2, gain, w_main, w_small, tm=512):
    n = x2.shape[0]
    return pl.pallas_call(
        _inproj_kernel,
        grid=(n // tm,),
        in_specs=[
            pl.BlockSpec((tm, D_MODEL), lambda i: (i, 0)),
            pl.BlockSpec((1, D_MODEL), lambda i: (0, 0)),
            pl.BlockSpec((D_MODEL, MAIN_WIDTH), lambda i: (0, 0)),
            pl.BlockSpec((D_MODEL, SMALL_WIDTH), lambda i: (0, 0)),
        ],
        out_specs=[
            pl.BlockSpec((tm, MAIN_WIDTH), lambda i: (i, 0)),
            pl.BlockSpec((tm, SMALL_WIDTH), lambda i: (i, 0)),
        ],
        out_shape=[
            jax.ShapeDtypeStruct((n, MAIN_WIDTH), BF16),
            jax.ShapeDtypeStruct((n, SMALL_WIDTH), F32),
        ],
        name="inproj",
        compiler_params=_cparams(("parallel",)),
    )(x2, gain, w_main, w_small)


CUMSUM_BLOCK = 256
DECAY_PART_STRIDE = 8
DECAY_PARTS = 3


def _split3(a):
    hi = a.astype(BF16)
    r1 = a - hi.astype(F32)
    mid = r1.astype(BF16)
    lo = (r1 - mid.astype(F32)).astype(BF16)
    return hi, mid, lo


def _decay_kernel(z_ref, b_ref, o_ref):
    t = z_ref.shape[0]
    row = lax.broadcasted_iota(jnp.int32, (CUMSUM_BLOCK, CUMSUM_BLOCK), 0)
    col = lax.broadcasted_iota(jnp.int32, (CUMSUM_BLOCK, CUMSUM_BLOCK), 1)
    tri = jnp.where(col <= row, 1.0, 0.0).astype(BF16)
    is_head = _lane_iota() < FOX_HEADS
    carry = jnp.zeros((1, LANES), F32)
    for blk in range(t // CUMSUM_BLOCK):
        sl = slice(blk * CUMSUM_BLOCK, (blk + 1) * CUMSUM_BLOCK)
        z = z_ref[sl, :] + b_ref[...]
        logf = -(jnp.maximum(-z, 0.0) + jnp.log1p(jnp.exp(-jnp.abs(z))))
        hi, mid, lo = _split3(logf)
        cs = _dot(tri, hi) + _dot(tri, mid) + _dot(tri, lo) + carry
        carry = cs[CUMSUM_BLOCK - 1:CUMSUM_BLOCK, :]
        parts = _split3(jnp.where(is_head, cs, 0.0))
        packed = parts[0].astype(F32)
        for j in range(1, DECAY_PARTS):
            packed = packed + pltpu.roll(parts[j].astype(F32), j * DECAY_PART_STRIDE, axis=1)
        o_ref[sl, :] = packed.astype(BF16)


def _decay(small3, bias):
    b, t, _ = small3.shape
    return pl.pallas_call(
        _decay_kernel,
        grid=(b,),
        in_specs=[pl.BlockSpec((None, t, LANES), lambda i: (i, 0, 0)),
                  pl.BlockSpec((1, LANES), lambda i: (0, 0))],
        out_specs=pl.BlockSpec((None, t, LANES), lambda i: (i, 0, 0)),
        out_shape=jax.ShapeDtypeStruct((b, t, LANES), BF16),
        name="decay",
        compiler_params=_cparams(("parallel",)),
    )(small3, bias)


def _pair_softmax(s, m_prev, ok_cols):
    nc = s.shape[1] // LANES
    cols = [s[:, c * LANES:(c + 1) * LANES] for c in range(nc)]
    if ok_cols is not None:
        cols = [jnp.where(ok, col, NEG_INF) for ok, col in zip(ok_cols, cols)]
    cmax = cols[0]
    for col in cols[1:]:
        cmax = jnp.maximum(cmax, col)
    m_new = jnp.maximum(m_prev, jnp.max(cmax, axis=1, keepdims=True))
    alpha = jnp.exp(m_prev - m_new)
    p = jnp.concatenate([jnp.exp(col - m_new).astype(BF16) for col in cols], axis=1)
    return m_new, alpha, p


def _causal_pair_attention(qs, keys_at, vals_at, qi, o_ref, tq):
    lane = _lane_iota()
    head0 = lane < HEAD_DIM
    rows = 2 * tq

    def scores(j):
        return _dot_nt(qs, keys_at(j))

    def weighted_values(p, j):
        v = vals_at(j)
        one = jnp.ones_like(v)
        pv0 = _dot(p[:tq], jnp.where(head0, v, one))
        pv1 = _dot(p[tq:], jnp.where(head0, one, v))
        return jnp.concatenate([pv0, pv1], axis=0)

    s = scores(0)
    m_prev = jnp.full((rows, LANES), NEG_INF, F32)
    acc = jnp.zeros((rows, LANES), F32)
    for j in range(qi):
        s_next = scores(j + 1)
        m_prev, alpha, p = _pair_softmax(s, m_prev, None)
        acc = alpha * acc + weighted_values(p, j)
        s = s_next

    r_i = lax.broadcasted_iota(jnp.int32, (tq, LANES), 0)
    c_i = lax.broadcasted_iota(jnp.int32, (tq, LANES), 1)
    ok_cols = []
    for c in range(tq // LANES):
        ok = c_i + c * LANES <= r_i
        ok_cols.append(jnp.concatenate([ok, ok], axis=0))
    _, alpha, p = _pair_softmax(s, m_prev, ok_cols)
    acc = alpha * acc + weighted_values(p, qi)
    out = acc / pltpu.roll(acc, HEAD_DIM, axis=1)
    o_ref[...] = jnp.where(head0, out[:tq], out[tq:]).astype(o_ref.dtype)


def _block_rows(j, tq):
    return slice(j * tq, (j + 1) * tq)


def _fox_kernel(q_ref, k_ref, v_ref, d_ref, o_ref, *, tq):
    p = pl.program_id(1)
    lane = _lane_iota()

    def keys_at(j):
        rows = _block_rows(j, tq)
        return jnp.concatenate([k_ref[rows, :], d_ref[rows, :]], axis=1)

    def vals_at(j):
        return v_ref[_block_rows(j, tq), :]

    for qi in range(q_ref.shape[0] // tq):
        q = q_ref[qi * tq:(qi + 1) * tq, :]
        zero = jnp.zeros_like(q)
        halves = []
        for h in range(2):
            head = 2 * p + h
            pick = (lane % DECAY_PART_STRIDE == head) & (lane < DECAY_PARTS * DECAY_PART_STRIDE)
            neg = jnp.broadcast_to(jnp.where(pick, -1.0, 0.0).astype(BF16), q.shape)
            q_h = jnp.where((lane // HEAD_DIM) == h, q, zero)
            halves.append(jnp.concatenate([q_h, neg], axis=1))
        qs = jnp.concatenate(halves, axis=0)
        _causal_pair_attention(qs, keys_at, vals_at, qi, o_ref.at[qi * tq:(qi + 1) * tq, :], tq)


def _fox_attention(main3, dcols, tq=512):
    b, t, _ = main3.shape
    cb = COL_FOX // LANES
    npair = FOX_HEADS // 2

    def seq(col):
        return pl.BlockSpec((None, t, LANES), lambda bi, p: (bi, 0, col(p)))

    return pl.pallas_call(
        functools.partial(_fox_kernel, tq=tq),
        grid=(b, npair),
        in_specs=[
            seq(lambda p: cb + p),
            seq(lambda p: cb + npair + p),
            seq(lambda p: cb + 2 * npair + p),
            seq(lambda p: 0),
        ],
        out_specs=pl.BlockSpec((None, t, LANES), lambda bi, p: (bi, 0, p)),
        out_shape=jax.ShapeDtypeStruct((b, t, FOX_WIDTH), BF16),
        name="fox_attention",
        compiler_params=_cparams(("parallel", "parallel")),
    )(main3, main3, main3, dcols)


def _mla_pre_kernel(cq_ref, ckv_ref, kr_ref, cos_ref, sin_ref, gq_ref, gkv_ref, wq_ref, wkv_ref,
                    qn_ref, qr_ref, kn_ref, kr_out_ref, v_ref, *, scale):
    cos = cos_ref[...]
    sin = sin_ref[...]
    nope = MLA_HEADS * MLA_NOPE_DIM
    hq = _rms(cq_ref[...].astype(F32), gq_ref[...]).astype(BF16)
    q = _dot(hq, wq_ref[...])
    qn_ref[...] = (q[:, :nope] * scale).astype(BF16)
    qr_ref[...] = (_rope(q[:, nope:], cos, sin, MLA_ROPE_DIM) * scale).astype(BF16)
    hkv = _rms(ckv_ref[...].astype(F32), gkv_ref[...]).astype(BF16)
    kv = _dot(hkv, wkv_ref[...])
    kn_ref[...] = kv[:, :nope].astype(BF16)
    v_ref[...] = kv[:, nope:].astype(BF16)
    kr_out_ref[...] = _rope(kr_ref[...], cos, sin, MLA_ROPE_DIM).astype(BF16)


def _mla_pre(main3, small3, cos_m, sin_m, gq, gkv, wq, wkv, tm=512):
    b, t, _ = main3.shape
    cq_blk = COL_MLA // MLA_Q_RANK
    nope = MLA_HEADS * MLA_NOPE_DIM
    vw = MLA_HEADS * MLA_V_DIM

    def tok(width, col=0):
        return pl.BlockSpec((None, tm, width), lambda bi, i: (bi, i, col))

    def full(shape):
        return pl.BlockSpec(shape, lambda bi, i: (0,) * len(shape))

    return pl.pallas_call(
        functools.partial(_mla_pre_kernel, scale=(MLA_NOPE_DIM + MLA_ROPE_DIM) ** -0.5),
        grid=(b, t // tm),
        in_specs=[tok(MLA_Q_RANK, cq_blk), tok(MLA_KV_RANK, cq_blk + 1), tok(LANES, 1),
                  tok(LANES), tok(LANES), full((1, MLA_Q_RANK)), full((1, MLA_KV_RANK)),
                  full(wq.shape), full(wkv.shape)],
        out_specs=[tok(nope), tok(LANES), tok(nope), tok(LANES), tok(vw)],
        out_shape=[jax.ShapeDtypeStruct((b, t, nope), BF16),
                   jax.ShapeDtypeStruct((b, t, LANES), BF16),
                   jax.ShapeDtypeStruct((b, t, nope), BF16),
                   jax.ShapeDtypeStruct((b, t, LANES), BF16),
                   jax.ShapeDtypeStruct((b, t, vw), BF16)],
        name="mla_pre",
        compiler_params=_cparams(("parallel", "parallel")),
    )(main3, main3, small3, cos_m, sin_m, gq, gkv, wq, wkv)


def _mla_kernel(qn_ref, qr_ref, kn_ref, kr_ref, v_ref, o_ref, *, tq):
    p = pl.program_id(1)
    lane = _lane_iota()

    def keys_at(j):
        rows = _block_rows(j, tq)
        return jnp.concatenate([kn_ref[rows, :], kr_ref[rows, :]], axis=1)

    def vals_at(j):
        return v_ref[_block_rows(j, tq), :]

    for qi in range(qn_ref.shape[0] // tq):
        qn = qn_ref[qi * tq:(qi + 1) * tq, :]
        qr = qr_ref[qi * tq:(qi + 1) * tq, :]
        zero = jnp.zeros_like(qn)
        halves = []
        for h in range(2):
            nope_h = jnp.where((lane // MLA_NOPE_DIM) == h, qn, zero)
            rope_h = jnp.where((lane // MLA_ROPE_DIM) == 2 * p + h, qr, zero)
            halves.append(jnp.concatenate([nope_h, rope_h], axis=1))
        qs = jnp.concatenate(halves, axis=0)
        _causal_pair_attention(qs, keys_at, vals_at, qi, o_ref.at[qi * tq:(qi + 1) * tq, :], tq)


def _mla_attention(qn, qr, kn, kr4, vv, tq=512):
    b, t, _ = qn.shape
    npair = MLA_HEADS // 2
    pair = pl.BlockSpec((None, t, LANES), lambda bi, p: (bi, 0, p))
    shared = pl.BlockSpec((None, t, LANES), lambda bi, p: (bi, 0, 0))
    return pl.pallas_call(
        functools.partial(_mla_kernel, tq=tq),
        grid=(b, npair),
        in_specs=[pair, shared, pair, shared, pair],
        out_specs=pair,
        out_shape=jax.ShapeDtypeStruct((b, t, MLA_WIDTH), BF16),
        name="mla_attention",
        compiler_params=_cparams(("parallel", "parallel")),
    )(qn, qr, kn, kr4, vv)


DIL_UNROLL = 8


def _dil_kernel(q_ref, k_ref, v_ref, cos_ref, sin_ref, o_ref,
                qf, kf, vf, qb, kb, vb, accb, mb, lb):
    t = q_ref.shape[0]
    blk = DIL_BLOCK
    lane = _lane_iota()
    head0 = lane < HEAD_DIM
    cos = cos_ref[...]
    sin = sin_ref[...]
    qf[...] = _rope(q_ref[...].astype(F32), cos, sin, HEAD_DIM)
    kf[...] = _rope(k_ref[...].astype(F32), cos, sin, HEAD_DIM)
    vf[...] = v_ref[...].astype(F32)

    for g, (_, rate) in enumerate(DIL_PAIRS):
        length = t // rate
        kb[g, 0:blk, :] = jnp.zeros((blk, LANES), BF16)
        vb[g, 0:blk, :] = jnp.zeros((blk, LANES), BF16)
        for res in range(rate):
            dst = slice(blk + res * length, blk + (res + 1) * length)
            if rate == 1:
                src = slice(None)
            else:
                src = pl.ds(res, length, stride=rate)
            qb[g, dst, :] = qf[src, :].astype(BF16)
            kb[g, dst, :] = kf[src, :].astype(BF16)
            vb[g, dst, :] = vf[src, :].astype(BF16)

    qi2 = lax.broadcasted_iota(jnp.int32, (blk, 2 * blk), 0)
    kj2 = lax.broadcasted_iota(jnp.int32, (blk, 2 * blk), 1)
    band = (kj2 >= qi2) & (kj2 <= qi2 + blk)
    bias_full = jnp.where(band, 0.0, NEG_INF)
    bias_first = jnp.where(band & (kj2 >= blk), 0.0, NEG_INF)
    qi1 = lax.broadcasted_iota(jnp.int32, (blk, blk), 0)
    kj1 = lax.broadcasted_iota(jnp.int32, (blk, blk), 1)
    bias_single = jnp.where(kj1 <= qi1, 0.0, NEG_INF)

    for g, (window, rate) in enumerate(DIL_PAIRS):
        assert window // rate == blk
        nb = t // rate // blk
        nblocks = t // blk

        def body(j, carry, g=g, nb=nb):
            base = pl.multiple_of(j * blk, blk)
            q = qb[g, pl.ds(base + blk, blk), :]
            zero = jnp.zeros_like(q)
            qs = jnp.concatenate([jnp.where(head0, q, zero), jnp.where(head0, zero, q)], axis=0)
            if nb == 1:
                keys = kb[g, pl.ds(base + blk, blk), :]
                vals = vb[g, pl.ds(base + blk, blk), :]
                bias = bias_single
            else:
                keys = kb[g, pl.ds(base, 2 * blk), :]
                vals = vb[g, pl.ds(base, 2 * blk), :]
                bias = jnp.where((j % nb) != 0, bias_full, bias_first)
            s = _dot_nt(qs, keys)
            ps, ms = [], []
            for h in range(2):
                sh = s[h * blk:(h + 1) * blk] + bias
                m = jnp.max(sh, axis=1, keepdims=True)
                ps.append(jnp.exp(sh - m).astype(BF16))
                ms.append(m)
            pv = _dot(jnp.concatenate(ps, axis=0),
                      jnp.concatenate([vals, jnp.ones_like(vals)], axis=1))
            rows = pl.ds(base, blk)
            accb[g, rows, :] = jnp.where(head0, pv[:blk, :LANES], pv[blk:, :LANES])
            mb[g, rows, :] = jnp.where(head0, ms[0], ms[1])
            lb[g, rows, :] = jnp.where(head0, pv[:blk, LANES:], pv[blk:, LANES:])
            return carry

        lax.fori_loop(0, nblocks, body, 0, unroll=DIL_UNROLL)

    rate_max = max(rate for _, rate in DIL_PAIRS)
    length = t // rate_max
    for res in range(rate_max):
        m_all, l_all, a_all = [], [], []
        for g, (_, rate) in enumerate(DIL_PAIRS):
            start = (res % rate) * (t // rate) + res // rate
            step = rate_max // rate
            rows = pl.ds(start, length) if step == 1 else pl.ds(start, length, stride=step)
            m_all.append(mb[g, rows, :])
            l_all.append(lb[g, rows, :])
            a_all.append(accb[g, rows, :])
        m_max = jnp.maximum(jnp.maximum(m_all[0], m_all[1]), m_all[2])
        ws = [jnp.exp(m - m_max) for m in m_all]
        num = ws[0] * a_all[0] + ws[1] * a_all[1] + ws[2] * a_all[2]
        den = ws[0] * l_all[0] + ws[1] * l_all[1] + ws[2] * l_all[2]
        o_ref[pl.ds(res, length, stride=rate_max), :] = num / den


def _dil_attention(main3, cos_d, sin_d):
    b, t, _ = main3.shape
    assert all(t % (rate * DIL_BLOCK) == 0 for _, rate in DIL_PAIRS)
    cb = COL_DIL // LANES
    npair = DIL_HEADS // 2
    nbr = len(DIL_PAIRS)

    def col(off):
        return pl.BlockSpec((None, t, LANES), lambda bi, p: (bi, 0, cb + off + p))

    tab = pl.BlockSpec((None, t, LANES), lambda bi, p: (bi, 0, 0))
    return pl.pallas_call(
        _dil_kernel,
        grid=(b, npair),
        in_specs=[col(0), col(npair), col(2 * npair), tab, tab],
        out_specs=pl.BlockSpec((None, t, LANES), lambda bi, p: (bi, 0, p)),
        out_shape=jax.ShapeDtypeStruct((b, t, DIL_WIDTH), F32),
        name="dilated_attention",
        scratch_shapes=[pltpu.VMEM((t, LANES), F32)] * 3
        + [pltpu.VMEM((nbr, t + DIL_BLOCK, LANES), BF16)] * 3
        + [pltpu.VMEM((nbr, t, LANES), F32)] * 3,
        compiler_params=_cparams(("parallel", "parallel")),
    )(main3, main3, main3, cos_d, sin_d)


def _merge_kernel(x_ref, gf_ref, gd_ref, gm_ref, of_ref, od_ref, om_ref,
                  wf_ref, wd_ref, wm_ref, wo_ref, o_ref):
    merged = (jax.nn.sigmoid(gf_ref[...].astype(F32)) * _dot(of_ref[...], wf_ref[...])
              + jax.nn.sigmoid(gd_ref[...].astype(F32)) * _dot(od_ref[...].astype(BF16), wd_ref[...])
              + jax.nn.sigmoid(gm_ref[...].astype(F32)) * _dot(om_ref[...], wm_ref[...]))
    o_ref[...] = x_ref[...] + _dot(merged.astype(BF16), wo_ref[...])


def _merge(x2, main2, o_fox, o_dil, o_mla, wf, wd, wm, wo, tm=512):
    n = x2.shape[0]

    def tok(width, col=0):
        return pl.BlockSpec((tm, width), lambda i: (i, col))

    def full(w):
        return pl.BlockSpec(w.shape, lambda i: (0, 0))

    return pl.pallas_call(
        _merge_kernel,
        grid=(n // tm,),
        in_specs=[tok(D_MODEL), tok(D_MODEL, 0), tok(D_MODEL, 1), tok(D_MODEL, 2),
                  tok(FOX_WIDTH), tok(DIL_WIDTH), tok(MLA_WIDTH),
                  full(wf), full(wd), full(wm), full(wo)],
        out_specs=tok(D_MODEL),
        out_shape=jax.ShapeDtypeStruct((n, D_MODEL), F32),
        name="merge_outproj",
        compiler_params=_cparams(("parallel",)),
    )(x2, main2, main2, main2, o_fox, o_dil, o_mla, wf, wd, wm, wo)


def _ffn_kernel(x_ref, g_ref, wg_ref, wu_ref, wd_ref, o_ref, h_sc, acc_sc):
    c = pl.program_id(1)

    @pl.when(c == 0)
    def _():
        x = x_ref[...]
        h_sc[...] = _rms(x, g_ref[...]).astype(BF16)
        acc_sc[...] = x

    h = h_sc[...]
    gate = _dot(h, wg_ref[...])
    up = _dot(h, wu_ref[...])
    act = gate * jax.nn.sigmoid(gate) * up
    acc_sc[...] += _dot(act.astype(BF16), wd_ref[...])

    @pl.when(c == pl.num_programs(1) - 1)
    def _():
        o_ref[...] = acc_sc[...]


def _ffn(x2, gain, wg, wu, wd, tm=512):
    n = x2.shape[0]
    tok = pl.BlockSpec((tm, D_MODEL), lambda i, c: (i, 0))
    return pl.pallas_call(
        _ffn_kernel,
        grid=(n // tm, D_FF // D_FF_EXPERT),
        in_specs=[tok, pl.BlockSpec((1, D_MODEL), lambda i, c: (0, 0)),
                  pl.BlockSpec((D_MODEL, D_FF_EXPERT), lambda i, c: (0, c)),
                  pl.BlockSpec((D_MODEL, D_FF_EXPERT), lambda i, c: (0, c)),
                  pl.BlockSpec((D_FF_EXPERT, D_MODEL), lambda i, c: (c, 0))],
        out_specs=tok,
        out_shape=jax.ShapeDtypeStruct((n, D_MODEL), F32),
        name="ffn_dense",
        scratch_shapes=[pltpu.VMEM((tm, D_MODEL), BF16), pltpu.VMEM((tm, D_MODEL), F32)],
        compiler_params=_cparams(("parallel", "arbitrary")),
    )(x2, gain, wg, wu, wd)


MOE_BLOCK = 2048
MOE_TILE = 256
MOE_TILES = -(-(2 * MOE_BLOCK + N_EXPERTS * (MOE_TILE - 1)) // MOE_TILE)
MOE_META_ROWS = 32
assert MOE_TILES <= MOE_META_ROWS
SCATTER_CHUNK = 256
MOE_VMEM_LIMIT = 60 * 1024 * 1024


def _router_kernel(x_ref, g_ref, wh_ref, wm_ref, wl_ref, h_ref, rc_ref, meta_ref):
    h = _rms(x_ref[...], g_ref[...])
    h_ref[...] = h.astype(BF16)
    h_hi, h_mid, h_lo = _split3(h)
    w_hi, w_mid, w_lo = wh_ref[...], wm_ref[...], wl_ref[...]
    logits = (_dot(h_hi, w_hi) + (_dot(h_hi, w_mid) + _dot(h_mid, w_hi))
              + (_dot(h_hi, w_lo) + _dot(h_mid, w_mid) + _dot(h_lo, w_hi)))
    lane = _lane_iota()
    is_expert = lane < N_EXPERTS
    logits = jnp.where(is_expert, logits, NEG_INF)
    v1 = jnp.max(logits, axis=1, keepdims=True)
    i1 = jnp.min(jnp.where(logits == v1, lane, LANES), axis=1, keepdims=True)
    first = lane == i1
    rest = jnp.where(first, NEG_INF, logits)
    v2 = jnp.max(rest, axis=1, keepdims=True)
    i2 = jnp.min(jnp.where(rest == v2, lane, LANES), axis=1, keepdims=True)
    second = lane == i2
    e2 = jnp.exp(v2 - v1)
    w1 = 1.0 / (1.0 + e2)
    w2 = e2 / (1.0 + e2)

    sel = jnp.where(first, 1.0, jnp.where(second, 1.0, 0.0))
    row = lax.broadcasted_iota(jnp.int32, (CUMSUM_BLOCK, CUMSUM_BLOCK), 0)
    col = lax.broadcasted_iota(jnp.int32, (CUMSUM_BLOCK, CUMSUM_BLOCK), 1)
    tri = jnp.where(col < row, 1.0, 0.0).astype(BF16)
    carry = jnp.zeros((1, LANES), F32)
    ranks = []
    for blk in range(sel.shape[0] // CUMSUM_BLOCK):
        part = sel[blk * CUMSUM_BLOCK:(blk + 1) * CUMSUM_BLOCK]
        ranks.append(_dot(tri, part.astype(BF16)) + carry)
        carry = carry + jnp.sum(part, axis=0, keepdims=True)
    rank = jnp.concatenate(ranks, axis=0)
    padded = jnp.ceil(carry / MOE_TILE) * MOE_TILE
    er = lax.broadcasted_iota(jnp.int32, (LANES, LANES), 0)
    ec = lax.broadcasted_iota(jnp.int32, (LANES, LANES), 1)
    before = jnp.where(er < ec, 1.0, 0.0).astype(BF16)
    start = _dot(jnp.broadcast_to(padded, (8, LANES)).astype(BF16), before)[0:1]
    slot = start + rank
    dest1 = jnp.sum(jnp.where(first, slot, 0.0), axis=1, keepdims=True)
    dest2 = jnp.sum(jnp.where(second, slot, 0.0), axis=1, keepdims=True)
    rc_ref[...] = jnp.where(lane == 0, dest1, jnp.where(lane == 1, dest2,
                            jnp.where(lane == 2, w1, jnp.where(lane == 3, w2, 0.0))))

    end = start + padded
    tile_row = lax.broadcasted_iota(jnp.int32, (MOE_META_ROWS, LANES), 0).astype(F32) * MOE_TILE
    passed = jnp.sum(jnp.where(is_expert, jnp.where(tile_row >= end, 1.0, 0.0), 0.0),
                     axis=1, keepdims=True)
    last = jnp.max(jnp.where(is_expert, jnp.where(padded > 0, lane.astype(F32), 0.0), 0.0),
                   axis=1, keepdims=True)
    total = jnp.sum(jnp.where(is_expert, padded, 0.0), axis=1, keepdims=True)
    active = jnp.where(tile_row < total, 1.0, 0.0)
    meta = jnp.where(lane == 0, jnp.minimum(passed, last), jnp.where(lane == 1, active, 0.0))
    meta_ref[...] = meta.astype(jnp.int32)


def _router(x2, gain, w_router):
    n = x2.shape[0]
    nblk = n // MOE_BLOCK
    w_pad = jnp.zeros((D_MODEL, LANES), F32).at[:, :N_EXPERTS].set(w_router)
    w_hi = w_pad.astype(BF16)
    r1 = w_pad - w_hi.astype(F32)
    w_mid = r1.astype(BF16)
    w_lo = (r1 - w_mid.astype(F32)).astype(BF16)
    wspec = pl.BlockSpec((D_MODEL, LANES), lambda i: (0, 0))
    return pl.pallas_call(
        _router_kernel,
        grid=(nblk,),
        in_specs=[pl.BlockSpec((MOE_BLOCK, D_MODEL), lambda i: (i, 0)),
                  pl.BlockSpec((1, D_MODEL), lambda i: (0, 0)), wspec, wspec, wspec],
        out_specs=[pl.BlockSpec((MOE_BLOCK, D_MODEL), lambda i: (i, 0)),
                   pl.BlockSpec((MOE_BLOCK, LANES), lambda i: (i, 0)),
                   pl.BlockSpec((None, MOE_META_ROWS, LANES), lambda i: (i, 0, 0))],
        out_shape=[jax.ShapeDtypeStruct((n, D_MODEL), BF16),
                   jax.ShapeDtypeStruct((n, LANES), F32),
                   jax.ShapeDtypeStruct((nblk, MOE_META_ROWS, LANES), jnp.int32)],
        name="router",
        compiler_params=_cparams(("parallel",)),
    )(x2, gain, w_hi, w_mid, w_lo)


def _moe_kernel(te_ref, ta_ref, h_ref, rrow_ref, rcol_ref, wg_ref, wu_ref, wd_ref, o_ref):
    b = pl.program_id(0)
    r = pl.program_id(1)

    @pl.when(r == 0)
    def _():
        o_ref[...] = jnp.zeros(o_ref.shape, F32)

    @pl.when(ta_ref[b, r] == 1)
    def _():
        base = (r * MOE_TILE).astype(F32)
        slot_col = lax.broadcasted_iota(jnp.int32, (MOE_TILE, 1), 0).astype(F32) + base
        hit1 = rrow_ref[0:1, :] == slot_col
        hit2 = rrow_ref[1:2, :] == slot_col
        onehot = jnp.where(hit1, 1.0, jnp.where(hit2, 1.0, 0.0)).astype(BF16)
        weight = jnp.sum(jnp.where(hit1, rrow_ref[2:3, :], jnp.where(hit2, rrow_ref[3:4, :], 0.0)),
                         axis=1, keepdims=True)
        xg = _dot(onehot, h_ref[...]).astype(BF16)
        gate = _dot(xg, wg_ref[...])
        up = _dot(xg, wu_ref[...])
        act = gate * jax.nn.sigmoid(gate) * up
        y = (_dot(act.astype(BF16), wd_ref[...]) * weight).astype(BF16)
        slot_row = lax.broadcasted_iota(jnp.int32, (1, MOE_TILE), 1).astype(F32) + base
        back = jnp.where(rcol_ref[:, 0:1] == slot_row, 1.0,
                         jnp.where(rcol_ref[:, 1:2] == slot_row, 1.0, 0.0)).astype(BF16)
        for c in range(0, D_MODEL, SCATTER_CHUNK):
            o_ref[:, c:c + SCATTER_CHUNK] += _dot(back, y[:, c:c + SCATTER_CHUNK])


def _moe(h2, rrow, rcol, tile_expert, tile_active, wg, wu, wd):
    n = h2.shape[0]
    nblk = n // MOE_BLOCK

    def wspec(shape):
        return pl.BlockSpec((None,) + shape, lambda b, r, te, ta: (te[b, r], 0, 0))

    grid_spec = pltpu.PrefetchScalarGridSpec(
        num_scalar_prefetch=2,
        grid=(nblk, MOE_TILES),
        in_specs=[pl.BlockSpec((MOE_BLOCK, D_MODEL), lambda b, r, te, ta: (b, 0)),
                  pl.BlockSpec((None, 8, MOE_BLOCK), lambda b, r, te, ta: (b, 0, 0)),
                  pl.BlockSpec((MOE_BLOCK, LANES), lambda b, r, te, ta: (b, 0)),
                  wspec((D_MODEL, D_FF_EXPERT)), wspec((D_MODEL, D_FF_EXPERT)),
                  wspec((D_FF_EXPERT, D_MODEL))],
        out_specs=pl.BlockSpec((MOE_BLOCK, D_MODEL), lambda b, r, te, ta: (b, 0)),
    )
    return pl.pallas_call(
        _moe_kernel,
        grid_spec=grid_spec,
        out_shape=jax.ShapeDtypeStruct((n, D_MODEL), F32),
        name="moe_experts",
        compiler_params=pltpu.CompilerParams(dimension_semantics=("parallel", "arbitrary"),
                                             vmem_limit_bytes=MOE_VMEM_LIMIT),
    )(tile_expert, tile_active, h2, rrow, rcol, wg, wu, wd)


def _moe_layer(x2, gain, w_router, wg, wu, wd):
    n = x2.shape[0]
    nblk = n // MOE_BLOCK
    h2, rcol, meta = _router(x2, gain, w_router)
    rrow = jnp.zeros((nblk, 8, MOE_BLOCK), F32).at[:, :4, :].set(
        rcol[:, :4].reshape(nblk, MOE_BLOCK, 4).transpose(0, 2, 1))
    tile_expert = meta[:, :MOE_TILES, 0]
    tile_active = meta[:, :MOE_TILES, 1]
    return _moe(h2, rrow, rcol, tile_expert, tile_active, wg, wu, wd)


def _ple_kernel(*refs, final, has_delta):
    if has_delta:
        x_ref, d_ref, p_ref, g_ref, wg_ref, wp_ref, fg_ref, o_ref = refs
        x = x_ref[...] + d_ref[...]
    else:
        x_ref, p_ref, g_ref, wg_ref, wp_ref, fg_ref, o_ref = refs
        x = x_ref[...]
    h = _rms(x, g_ref[...]).astype(BF16)
    gate = jax.nn.sigmoid(_dot(h, wg_ref[...]))
    y = x + gate * _dot(p_ref[...].astype(BF16), wp_ref[...])
    if final:
        y = _rms(y, fg_ref[...])
    o_ref[...] = y


def _ple(x2, delta, p2, gain, wg, wp, final_gain, final, tm=512):
    n = x2.shape[0]
    vec = pl.BlockSpec((1, D_MODEL), lambda i: (0, 0))
    tok = pl.BlockSpec((tm, D_MODEL), lambda i: (i, 0))
    acts = [x2] if delta is None else [x2, delta]
    return pl.pallas_call(
        functools.partial(_ple_kernel, final=final, has_delta=delta is not None),
        grid=(n // tm,),
        in_specs=[tok] * len(acts) + [
                  pl.BlockSpec((tm, PLE_DIM), lambda i: (i, 0)), vec,
                  pl.BlockSpec(wg.shape, lambda i: (0, 0)),
                  pl.BlockSpec(wp.shape, lambda i: (0, 0)), vec],
        out_specs=pl.BlockSpec((tm, D_MODEL), lambda i: (i, 0)),
        out_shape=jax.ShapeDtypeStruct((n, D_MODEL), F32),
        name="ple_final" if final else "ple",
        compiler_params=_cparams(("parallel",)),
    )(*acts, p2, gain, wg, wp, final_gain)


def _prep_in_weights(w_in):
    sizes = (FOX_WIDTH, FOX_WIDTH, FOX_WIDTH, FOX_HEADS, DIL_WIDTH, DIL_WIDTH, DIL_WIDTH,
             MLA_Q_RANK, MLA_KV_RANK, MLA_ROPE_DIM, GATE_WIDTH)
    offs = [0]
    for s in sizes:
        offs.append(offs[-1] + s)
    fq, fk, fv, fz, dq, dk, dv, cq, ckv, kr, gates = [w_in[..., offs[i]:offs[i + 1]]
                                                      for i in range(len(sizes))]
    scale = HEAD_DIM ** -0.5
    main = jnp.concatenate([gates, fq * scale, fk, fv, dq * scale, dk, dv, cq, ckv], axis=-1)
    pad = jnp.zeros(w_in.shape[:-1] + (LANES - FOX_HEADS,), w_in.dtype)
    small = jnp.concatenate([fz, pad] + [kr] * MLA_HEADS, axis=-1)
    return main.astype(BF16), small.astype(BF16)


def _prep_mla_weights(w_uq, w_ukv):
    depth = w_uq.shape[0]
    uq = w_uq.reshape(depth, MLA_Q_RANK, MLA_HEADS, MLA_NOPE_DIM + MLA_ROPE_DIM)
    wq = jnp.concatenate([uq[..., :MLA_NOPE_DIM].reshape(depth, MLA_Q_RANK, -1),
                          uq[..., MLA_NOPE_DIM:].reshape(depth, MLA_Q_RANK, -1)], axis=-1)
    ukv = w_ukv.reshape(depth, MLA_KV_RANK, MLA_HEADS, MLA_NOPE_DIM + MLA_V_DIM)
    wkv = jnp.concatenate([ukv[..., :MLA_NOPE_DIM].reshape(depth, MLA_KV_RANK, -1),
                           ukv[..., MLA_NOPE_DIM:].reshape(depth, MLA_KV_RANK, -1)], axis=-1)
    return wq.astype(BF16), wkv.astype(BF16)


def kernel(x, p, positions, mix_norm, w_in, b_forget, mla_q_norm, mla_kv_norm, w_uq, w_ukv,
           w_br_fox, w_br_dil, w_br_mla, w_out, ffn_norm, w_ffn_gate, w_ffn_up, w_ffn_down,
           w_router, w_exp_gate, w_exp_up, w_exp_down, ple_norm, w_ple_gate, w_ple_proj,
           final_norm):
    b, t, _ = x.shape
    n = b * t
    depth = w_in.shape[0]

    w_main, w_small = _prep_in_weights(w_in)
    wq_all, wkv_all = _prep_mla_weights(w_uq, w_ukv)
    bias_all = jnp.concatenate(
        [b_forget.astype(F32), jnp.zeros((depth, LANES - FOX_HEADS), F32)], axis=-1)
    cos_d, sin_d, cos_m, sin_m = _rope_tables(positions)
    final_gain = final_norm.reshape(1, D_MODEL)

    x2 = x.reshape(n, D_MODEL)
    for i in range(depth):
        main2, small2 = _inproj(x2, mix_norm[i].reshape(1, D_MODEL), w_main[i], w_small[i])
        main3 = main2.reshape(b, t, MAIN_WIDTH)
        small3 = small2.reshape(b, t, SMALL_WIDTH)

        dcols = _decay(small3, bias_all[i].reshape(1, LANES))
        o_fox = _fox_attention(main3, dcols)

        o_dil = _dil_attention(main3, cos_d, sin_d)

        qn, qr, kn, kr4, vv = _mla_pre(main3, small3, cos_m, sin_m,
                                       mla_q_norm[i].reshape(1, -1), mla_kv_norm[i].reshape(1, -1),
                                       wq_all[i], wkv_all[i])
        o_mla = _mla_attention(qn, qr, kn, kr4, vv)

        x2 = _merge(x2, main2, o_fox.reshape(n, -1), o_dil.reshape(n, -1), o_mla.reshape(n, -1),
                    w_br_fox[i].astype(BF16), w_br_dil[i].astype(BF16), w_br_mla[i].astype(BF16),
                    w_out[i].astype(BF16))

        j = i // 2
        gain = ffn_norm[i].reshape(1, D_MODEL)
        delta = None
        if i % 2 == 0:
            x2 = _ffn(x2, gain, w_ffn_gate[j].astype(BF16), w_ffn_up[j].astype(BF16),
                      w_ffn_down[j].astype(BF16))
        else:
            delta = _moe_layer(x2, gain, w_router[j], w_exp_gate[j].astype(BF16),
                               w_exp_up[j].astype(BF16), w_exp_down[j].astype(BF16))

        x2 = _ple(x2, delta, p[i].reshape(n, PLE_DIM), ple_norm[i].reshape(1, D_MODEL),
                  w_ple_gate[i].astype(BF16), w_ple_proj[i].astype(BF16), final_gain,
                  final=(i == depth - 1))
    return x2.reshape(b, t, D_MODEL)
```

```python
import functools

import jax
import jax.numpy as jnp
from jax import lax
from jax.experimental import pallas as pl
from jax.experimental.pallas import tpu as pltpu

F32 = jnp.float32
BF16 = jnp.bfloat16

D_MODEL = 1024
HEAD_DIM = 64
ROPE_THETA = 10000.0
NORM_EPS = 1e-6
FOX_HEADS = 6
FOX_WIDTH = FOX_HEADS * HEAD_DIM
DIL_HEADS = 6
DIL_WIDTH = DIL_HEADS * HEAD_DIM
DIL_PAIRS = ((128, 1), (512, 4), (2048, 16))
DIL_BLOCK = 128
MLA_HEADS = 4
MLA_Q_RANK = 256
MLA_KV_RANK = 256
MLA_NOPE_DIM = 64
MLA_ROPE_DIM = 32
MLA_V_DIM = 64
MLA_WIDTH = MLA_HEADS * MLA_V_DIM
N_BRANCHES = 3
D_FF = 2816
N_EXPERTS = 8
D_FF_EXPERT = 1408
PLE_DIM = 256

LANES = 128
GATE_WIDTH = N_BRANCHES * D_MODEL
COL_GATES = 0
COL_DIL_QK = GATE_WIDTH
COL_FOX = COL_DIL_QK + 2 * DIL_WIDTH
COL_DIL_V = COL_FOX + 3 * FOX_WIDTH
COL_MLA = COL_DIL_V + DIL_WIDTH
MAIN_WIDTH = COL_MLA + MLA_Q_RANK + MLA_KV_RANK
MXU_TILE = 256
assert all(c % MXU_TILE == 0 for c in (COL_DIL_QK, COL_FOX, COL_MLA, MAIN_WIDTH))
SMALL_WIDTH = 2 * LANES
VMEM_LIMIT = 56 * 1024 * 1024

NEG_INF = float("-inf")


def _cparams(sem):
    return pltpu.CompilerParams(dimension_semantics=sem, vmem_limit_bytes=VMEM_LIMIT)


def _rms(x, gain):
    return x * lax.rsqrt(jnp.mean(x * x, axis=-1, keepdims=True) + NORM_EPS) * gain


def _dot(a, b):
    return jnp.dot(a, b, preferred_element_type=F32)


def _dot_nt(a, b):
    return lax.dot_general(a, b, (((1,), (1,)), ((), ())), preferred_element_type=F32)


def _lane_iota():
    return lax.broadcasted_iota(jnp.int32, (1, LANES), 1)


def _rope_tab_kernel(pos_ref, fd_ref, fm_ref, cd_ref, sd_ref, cm_ref, sm_ref):
    pos = pos_ref[...].astype(F32)
    lane = _lane_iota()
    ang = pos * fd_ref[...]
    cd_ref[...] = jnp.cos(ang)
    sd_ref[...] = jnp.sin(ang) * jnp.where((lane % HEAD_DIM) < HEAD_DIM // 2, -1.0, 1.0)
    ang = pos * fm_ref[...]
    cm_ref[...] = jnp.cos(ang)
    sm_ref[...] = jnp.sin(ang) * jnp.where((lane % MLA_ROPE_DIM) < MLA_ROPE_DIM // 2, -1.0, 1.0)


def _rope_tables(positions):
    b, t = positions.shape
    half_d = HEAD_DIM // 2
    half_m = MLA_ROPE_DIM // 2
    inv_d = ROPE_THETA ** (-jnp.arange(half_d, dtype=F32) / half_d)
    inv_m = ROPE_THETA ** (-jnp.arange(half_m, dtype=F32) / half_m)
    fd = jnp.tile(inv_d, LANES // half_d)[None, :]
    fm = jnp.tile(inv_m, LANES // half_m)[None, :]
    tab = jax.ShapeDtypeStruct((b, t, LANES), F32)
    tab_spec = pl.BlockSpec((None, t, LANES), lambda i: (i, 0, 0))
    vec_spec = pl.BlockSpec((1, LANES), lambda i: (0, 0))
    return pl.pallas_call(
        _rope_tab_kernel,
        grid=(b,),
        in_specs=[pl.BlockSpec((None, t, 1), lambda i: (i, 0, 0)), vec_spec, vec_spec],
        out_specs=[tab_spec] * 4,
        out_shape=[tab] * 4,
        name="rope_tables",
        compiler_params=_cparams(("parallel",)),
    )(positions.reshape(b, t, 1), fd, fm)


def _swap_halves(x, width):
    lane = _lane_iota()
    half = width // 2
    return jnp.where((lane % width) < half,
                     pltpu.roll(x, LANES - half, axis=1),
                     pltpu.roll(x, half, axis=1))


def _rope(x, cos, sin_signed, width):
    return x * cos + _swap_halves(x, width) * sin_signed


def _col_chunks(width, step=512):
    return [(c, min(step, width - c)) for c in range(0, width, step)]


MLA_NOPE_WIDTH = MLA_HEADS * MLA_NOPE_DIM
MLA_SCALE = (MLA_NOPE_DIM + MLA_ROPE_DIM) ** -0.5


def _inproj_kernel(x_ref, g_ref, w_ref, ws_ref, cd_ref, sd_ref, cm_ref, sm_ref,
                   gq_ref, gkv_ref, wq_ref, wkv_ref,
                   o_ref, fz_ref, qn_ref, qr_ref, kn_ref, kr_ref, v_ref):
    h = _rms(x_ref[...], g_ref[...]).astype(BF16)
    for lo, hi in ((0, COL_DIL_QK), (COL_FOX, COL_MLA)):
        for c, n in _col_chunks(hi - lo):
            o_ref[:, lo + c:lo + c + n] = _dot(h, w_ref[:, lo + c:lo + c + n]).astype(BF16)
    cos_d = cd_ref[...]
    sin_d = sd_ref[...]
    for c0 in range(COL_DIL_QK, COL_FOX, MXU_TILE):
        y = _dot(h, w_ref[:, c0:c0 + MXU_TILE])
        for c in range(0, MXU_TILE, LANES):
            o_ref[:, c0 + c:c0 + c + LANES] = _rope(y[:, c:c + LANES], cos_d, sin_d,
                                                    HEAD_DIM).astype(BF16)

    cos_m = cm_ref[...]
    sin_m = sm_ref[...]
    lat = _dot(h, w_ref[:, COL_MLA:MAIN_WIDTH])
    hq = _rms(lat[:, :MLA_Q_RANK], gq_ref[...]).astype(BF16)
    q = _dot(hq, wq_ref[...])
    qn_ref[...] = (q[:, :MLA_NOPE_WIDTH] * MLA_SCALE).astype(BF16)
    qr_ref[...] = (_rope(q[:, MLA_NOPE_WIDTH:], cos_m, sin_m, MLA_ROPE_DIM) * MLA_SCALE).astype(BF16)
    hkv = _rms(lat[:, MLA_Q_RANK:], gkv_ref[...]).astype(BF16)
    kv = _dot(hkv, wkv_ref[...])
    kn_ref[...] = kv[:, :MLA_NOPE_WIDTH].astype(BF16)
    v_ref[...] = kv[:, MLA_NOPE_WIDTH:].astype(BF16)

    small = _dot(h, ws_ref[...])
    fz_ref[...] = small[:, :LANES]
    kr_ref[...] = _rope(small[:, LANES:], cos_m, sin_m, MLA_ROPE_DIM).astype(BF16)


def _inproj(x2, gain, w_main, w_small, tabs, gq, gkv, wq, wkv, tm=512):
    n = x2.shape[0]

    def tok(width):
        return pl.BlockSpec((tm, width), lambda i: (i, 0))

    def full(a):
        return pl.BlockSpec(a.shape, lambda i: (0, 0))

    widths = (COL_MLA, LANES, MLA_NOPE_WIDTH, LANES, MLA_NOPE_WIDTH, LANES, MLA_WIDTH)
    dtypes = (BF16, F32, BF16, BF16, BF16, BF16, BF16)
    return pl.pallas_call(
        _inproj_kernel,
        grid=(n // tm,),
        in_specs=[tok(D_MODEL), full(gain), full(w_main), full(w_small)]
        + [tok(LANES)] * 4 + [full(gq), full(gkv), full(wq), full(wkv)],
        out_specs=[tok(w) for w in widths],
        out_shape=[jax.ShapeDtypeStruct((n, w), d) for w, d in zip(widths, dtypes)],
        name="inproj",
        compiler_params=_cparams(("parallel",)),
    )(x2, gain, w_main, w_small, *tabs, gq, gkv, wq, wkv)


CUMSUM_BLOCK = 256
DECAY_PART_STRIDE = 8
DECAY_PARTS = 3


def _split3(a):
    hi = a.astype(BF16)
    r1 = a - hi.astype(F32)
    mid = r1.astype(BF16)
    lo = (r1 - mid.astype(F32)).astype(BF16)
    return hi, mid, lo


def _decay_kernel(z_ref, b_ref, o_ref):
    t = z_ref.shape[0]
    row = lax.broadcasted_iota(jnp.int32, (CUMSUM_BLOCK, CUMSUM_BLOCK), 0)
    col = lax.broadcasted_iota(jnp.int32, (CUMSUM_BLOCK, CUMSUM_BLOCK), 1)
    tri = jnp.where(col <= row, 1.0, 0.0).astype(BF16)
    is_head = _lane_iota() < FOX_HEADS
    carry = jnp.zeros((1, LANES), F32)
    for blk in range(t // CUMSUM_BLOCK):
        sl = slice(blk * CUMSUM_BLOCK, (blk + 1) * CUMSUM_BLOCK)
        z = z_ref[sl, :] + b_ref[...]
        logf = -(jnp.maximum(-z, 0.0) + jnp.log1p(jnp.exp(-jnp.abs(z))))
        hi, mid, lo = _split3(logf)
        cs = _dot(tri, hi) + _dot(tri, mid) + _dot(tri, lo) + carry
        carry = cs[CUMSUM_BLOCK - 1:CUMSUM_BLOCK, :]
        parts = _split3(jnp.where(is_head, cs, 0.0))
        packed = parts[0].astype(F32)
        for j in range(1, DECAY_PARTS):
            packed = packed + pltpu.roll(parts[j].astype(F32), j * DECAY_PART_STRIDE, axis=1)
        o_ref[sl, :] = packed.astype(BF16)


def _decay(small3, bias):
    b, t, _ = small3.shape
    return pl.pallas_call(
        _decay_kernel,
        grid=(b,),
        in_specs=[pl.BlockSpec((None, t, LANES), lambda i: (i, 0, 0)),
                  pl.BlockSpec((1, LANES), lambda i: (0, 0))],
        out_specs=pl.BlockSpec((None, t, LANES), lambda i: (i, 0, 0)),
        out_shape=jax.ShapeDtypeStruct((b, t, LANES), BF16),
        name="decay",
        compiler_params=_cparams(("parallel",)),
    )(small3, bias)


def _pair_softmax(s, m_prev, ok_cols):
    nc = s.shape[1] // LANES
    cols = [s[:, c * LANES:(c + 1) * LANES] for c in range(nc)]
    if ok_cols is not None:
        cols = [jnp.where(ok, col, NEG_INF) for ok, col in zip(ok_cols, cols)]
    cmax = cols[0]
    for col in cols[1:]:
        cmax = jnp.maximum(cmax, col)
    m_new = jnp.maximum(m_prev, jnp.max(cmax, axis=1, keepdims=True))
    alpha = jnp.exp(m_prev - m_new)
    p = jnp.concatenate([jnp.exp(col - m_new).astype(BF16) for col in cols], axis=1)
    return m_new, alpha, p


def _causal_pair_attention(qs, keys_at, vals_at, qi, o_ref, tq):
    lane = _lane_iota()
    head0 = lane < HEAD_DIM
    rows = 2 * tq

    def scores(j):
        return _dot_nt(qs, keys_at(j))

    def weighted_values(p, j):
        v = vals_at(j)
        one = jnp.ones_like(v)
        pv0 = _dot(p[:tq], jnp.where(head0, v, one))
        pv1 = _dot(p[tq:], jnp.where(head0, one, v))
        return jnp.concatenate([pv0, pv1], axis=0)

    s = scores(0)
    m_prev = jnp.full((rows, LANES), NEG_INF, F32)
    acc = jnp.zeros((rows, LANES), F32)
    for j in range(qi):
        s_next = scores(j + 1)
        m_prev, alpha, p = _pair_softmax(s, m_prev, None)
        acc = alpha * acc + weighted_values(p, j)
        s = s_next

    r_i = lax.broadcasted_iota(jnp.int32, (tq, LANES), 0)
    c_i = lax.broadcasted_iota(jnp.int32, (tq, LANES), 1)
    ok_cols = []
    for c in range(tq // LANES):
        ok = c_i + c * LANES <= r_i
        ok_cols.append(jnp.concatenate([ok, ok], axis=0))
    _, alpha, p = _pair_softmax(s, m_prev, ok_cols)
    acc = alpha * acc + weighted_values(p, qi)
    out = acc / pltpu.roll(acc, HEAD_DIM, axis=1)
    o_ref[...] = jnp.where(head0, out[:tq], out[tq:]).astype(o_ref.dtype)


def _block_rows(j, tq):
    return slice(j * tq, (j + 1) * tq)


def _fox_kernel(q_ref, k_ref, v_ref, d_ref, o_ref, *, tq):
    p = pl.program_id(1)
    lane = _lane_iota()

    def keys_at(j):
        rows = _block_rows(j, tq)
        return jnp.concatenate([k_ref[rows, :], d_ref[rows, :]], axis=1)

    def vals_at(j):
        return v_ref[_block_rows(j, tq), :]

    for qi in range(q_ref.shape[0] // tq):
        q = q_ref[qi * tq:(qi + 1) * tq, :]
        zero = jnp.zeros_like(q)
        halves = []
        for h in range(2):
            head = 2 * p + h
            pick = (lane % DECAY_PART_STRIDE == head) & (lane < DECAY_PARTS * DECAY_PART_STRIDE)
            neg = jnp.broadcast_to(jnp.where(pick, -1.0, 0.0).astype(BF16), q.shape)
            q_h = jnp.where((lane // HEAD_DIM) == h, q, zero)
            halves.append(jnp.concatenate([q_h, neg], axis=1))
        qs = jnp.concatenate(halves, axis=0)
        _causal_pair_attention(qs, keys_at, vals_at, qi, o_ref.at[qi * tq:(qi + 1) * tq, :], tq)


def _fox_attention(main3, dcols, tq=512):
    b, t, _ = main3.shape
    cb = COL_FOX // LANES
    npair = FOX_HEADS // 2

    def seq(col):
        return pl.BlockSpec((None, t, LANES), lambda bi, p: (bi, 0, col(p)))

    return pl.pallas_call(
        functools.partial(_fox_kernel, tq=tq),
        grid=(b, npair),
        in_specs=[
            seq(lambda p: cb + p),
            seq(lambda p: cb + npair + p),
            seq(lambda p: cb + 2 * npair + p),
            seq(lambda p: 0),
        ],
        out_specs=pl.BlockSpec((None, t, LANES), lambda bi, p: (bi, 0, p)),
        out_shape=jax.ShapeDtypeStruct((b, t, FOX_WIDTH), BF16),
        name="fox_attention",
        compiler_params=_cparams(("parallel", "parallel")),
    )(main3, main3, main3, dcols)


def _mla_kernel(qn_ref, qr_ref, kn_ref, kr_ref, v_ref, o_ref, *, tq):
    p = pl.program_id(1)
    lane = _lane_iota()

    def keys_at(j):
        rows = _block_rows(j, tq)
        return jnp.concatenate([kn_ref[rows, :], kr_ref[rows, :]], axis=1)

    def vals_at(j):
        return v_ref[_block_rows(j, tq), :]

    for qi in range(qn_ref.shape[0] // tq):
        qn = qn_ref[qi * tq:(qi + 1) * tq, :]
        qr = qr_ref[qi * tq:(qi + 1) * tq, :]
        zero = jnp.zeros_like(qn)
        halves = []
        for h in range(2):
            nope_h = jnp.where((lane // MLA_NOPE_DIM) == h, qn, zero)
            rope_h = jnp.where((lane // MLA_ROPE_DIM) == 2 * p + h, qr, zero)
            halves.append(jnp.concatenate([nope_h, rope_h], axis=1))
        qs = jnp.concatenate(halves, axis=0)
        _causal_pair_attention(qs, keys_at, vals_at, qi, o_ref.at[qi * tq:(qi + 1) * tq, :], tq)


def _mla_attention(qn, qr, kn, kr4, vv, tq=512):
    b, t, _ = qn.shape
    npair = MLA_HEADS // 2
    pair = pl.BlockSpec((None, t, LANES), lambda bi, p: (bi, 0, p))
    shared = pl.BlockSpec((None, t, LANES), lambda bi, p: (bi, 0, 0))
    return pl.pallas_call(
        functools.partial(_mla_kernel, tq=tq),
        grid=(b, npair),
        in_specs=[pair, shared, pair, shared, pair],
        out_specs=pair,
        out_shape=jax.ShapeDtypeStruct((b, t, MLA_WIDTH), BF16),
        name="mla_attention",
        compiler_params=_cparams(("parallel", "parallel")),
    )(qn, qr, kn, kr4, vv)


DIL_UNROLL = 8


def _dil_kernel(q_ref, k_ref, v_ref, o_ref, qf, kf, vf, qb, kb, vb, accb, mb, lb):
    t = q_ref.shape[0]
    blk = DIL_BLOCK
    lane = _lane_iota()
    head0 = lane < HEAD_DIM
    qf[...] = q_ref[...].astype(F32)
    kf[...] = k_ref[...].astype(F32)
    vf[...] = v_ref[...].astype(F32)

    for g, (_, rate) in enumerate(DIL_PAIRS):
        length = t // rate
        kb[g, 0:blk, :] = jnp.zeros((blk, LANES), BF16)
        vb[g, 0:blk, :] = jnp.zeros((blk, LANES), BF16)
        if rate == 1:
            qb[g, blk:, :] = q_ref[...]
            kb[g, blk:, :] = k_ref[...]
            vb[g, blk:, :] = v_ref[...]
            continue
        for res in range(rate):
            dst = slice(blk + res * length, blk + (res + 1) * length)
            src = pl.ds(res, length, stride=rate)
            qb[g, dst, :] = qf[src, :].astype(BF16)
            kb[g, dst, :] = kf[src, :].astype(BF16)
            vb[g, dst, :] = vf[src, :].astype(BF16)

    qi2 = lax.broadcasted_iota(jnp.int32, (blk, 2 * blk), 0)
    kj2 = lax.broadcasted_iota(jnp.int32, (blk, 2 * blk), 1)
    band = (kj2 >= qi2) & (kj2 <= qi2 + blk)
    bias_full = jnp.where(band, 0.0, NEG_INF)
    bias_first = jnp.where(band & (kj2 >= blk), 0.0, NEG_INF)
    qi1 = lax.broadcasted_iota(jnp.int32, (blk, blk), 0)
    kj1 = lax.broadcasted_iota(jnp.int32, (blk, blk), 1)
    bias_single = jnp.where(kj1 <= qi1, 0.0, NEG_INF)

    for g, (window, rate) in enumerate(DIL_PAIRS):
        assert window // rate == blk
        nb = t // rate // blk
        nblocks = t // blk

        def body(j, carry, g=g, nb=nb):
            base = pl.multiple_of(j * blk, blk)
            q = qb[g, pl.ds(base + blk, blk), :]
            zero = jnp.zeros_like(q)
            qs = jnp.concatenate([jnp.where(head0, q, zero), jnp.where(head0, zero, q)], axis=0)
            if nb == 1:
                keys = kb[g, pl.ds(base + blk, blk), :]
                vals = vb[g, pl.ds(base + blk, blk), :]
                bias = bias_single
            else:
                keys = kb[g, pl.ds(base, 2 * blk), :]
                vals = vb[g, pl.ds(base, 2 * blk), :]
                bias = jnp.where((j % nb) != 0, bias_full, bias_first)
            s = _dot_nt(qs, keys)
            ps, ms = [], []
            for h in range(2):
                sh = s[h * blk:(h + 1) * blk] + bias
                m = jnp.max(sh, axis=1, keepdims=True)
                ps.append(jnp.exp(sh - m).astype(BF16))
                ms.append(m)
            pv = _dot(jnp.concatenate(ps, axis=0),
                      jnp.concatenate([vals, jnp.ones_like(vals)], axis=1))
            rows = pl.ds(base, blk)
            accb[g, rows, :] = jnp.where(head0, pv[:blk, :LANES], pv[blk:, :LANES])
            mb[g, rows, :] = jnp.where(head0, ms[0], ms[1])
            lb[g, rows, :] = jnp.where(head0, pv[:blk, LANES:], pv[blk:, LANES:])
            return carry

        lax.fori_loop(0, nblocks, body, 0, unroll=DIL_UNROLL)

    rate_max = max(rate for _, rate in DIL_PAIRS)
    length = t // rate_max
    for res in range(rate_max):
        m_all, l_all, a_all = [], [], []
        for g, (_, rate) in enumerate(DIL_PAIRS):
            start = (res % rate) * (t // rate) + res // rate
            step = rate_max // rate
            rows = pl.ds(start, length) if step == 1 else pl.ds(start, length, stride=step)
            m_all.append(mb[g, rows, :])
            l_all.append(lb[g, rows, :])
            a_all.append(accb[g, rows, :])
        m_max = jnp.maximum(jnp.maximum(m_all[0], m_all[1]), m_all[2])
        ws = [jnp.exp(m - m_max) for m in m_all]
        num = ws[0] * a_all[0] + ws[1] * a_all[1] + ws[2] * a_all[2]
        den = ws[0] * l_all[0] + ws[1] * l_all[1] + ws[2] * l_all[2]
        o_ref[pl.ds(res, length, stride=rate_max), :] = num / den


def _dil_attention(main3):
    b, t, _ = main3.shape
    assert all(t % (rate * DIL_BLOCK) == 0 for _, rate in DIL_PAIRS)
    npair = DIL_HEADS // 2
    nbr = len(DIL_PAIRS)

    def col(start):
        return pl.BlockSpec((None, t, LANES), lambda bi, p: (bi, 0, start // LANES + p))

    return pl.pallas_call(
        _dil_kernel,
        grid=(b, npair),
        in_specs=[col(COL_DIL_QK), col(COL_DIL_QK + DIL_WIDTH), col(COL_DIL_V)],
        out_specs=pl.BlockSpec((None, t, LANES), lambda bi, p: (bi, 0, p)),
        out_shape=jax.ShapeDtypeStruct((b, t, DIL_WIDTH), F32),
        name="dilated_attention",
        scratch_shapes=[pltpu.VMEM((t, LANES), F32)] * 3
        + [pltpu.VMEM((nbr, t + DIL_BLOCK, LANES), BF16)] * 3
        + [pltpu.VMEM((nbr, t, LANES), F32)] * 3,
        compiler_params=_cparams(("parallel", "parallel")),
    )(main3, main3, main3)


def _merge_kernel(x_ref, gf_ref, gd_ref, gm_ref, of_ref, od_ref, om_ref,
                  wf_ref, wd_ref, wm_ref, wo_ref, o_ref):
    merged = (jax.nn.sigmoid(gf_ref[...].astype(F32)) * _dot(of_ref[...], wf_ref[...])
              + jax.nn.sigmoid(gd_ref[...].astype(F32)) * _dot(od_ref[...].astype(BF16), wd_ref[...])
              + jax.nn.sigmoid(gm_ref[...].astype(F32)) * _dot(om_ref[...], wm_ref[...]))
    o_ref[...] = x_ref[...] + _dot(merged.astype(BF16), wo_ref[...])


def _merge(x2, main2, o_fox, o_dil, o_mla, wf, wd, wm, wo, tm=512):
    n = x2.shape[0]

    def tok(width, col=0):
        return pl.BlockSpec((tm, width), lambda i: (i, col))

    def full(w):
        return pl.BlockSpec(w.shape, lambda i: (0, 0))

    return pl.pallas_call(
        _merge_kernel,
        grid=(n // tm,),
        in_specs=[tok(D_MODEL), tok(D_MODEL, 0), tok(D_MODEL, 1), tok(D_MODEL, 2),
                  tok(FOX_WIDTH), tok(DIL_WIDTH), tok(MLA_WIDTH),
                  full(wf), full(wd), full(wm), full(wo)],
        out_specs=tok(D_MODEL),
        out_shape=jax.ShapeDtypeStruct((n, D_MODEL), F32),
        name="merge_outproj",
        compiler_params=_cparams(("parallel",)),
    )(x2, main2, main2, main2, o_fox, o_dil, o_mla, wf, wd, wm, wo)


def _ffn_kernel(x_ref, g_ref, wg_ref, wu_ref, wd_ref, p_ref, pg_ref, wpg_ref, wpp_ref,
                o_ref, h_sc, acc_sc):
    c = pl.program_id(1)

    @pl.when(c == 0)
    def _():
        x = x_ref[...]
        h_sc[...] = _rms(x, g_ref[...]).astype(BF16)
        acc_sc[...] = x

    h = h_sc[...]
    gate = _dot(h, wg_ref[...])
    up = _dot(h, wu_ref[...])
    act = gate * jax.nn.sigmoid(gate) * up
    acc_sc[...] += _dot(act.astype(BF16), wd_ref[...])

    @pl.when(c == pl.num_programs(1) - 1)
    def _():
        x = acc_sc[...]
        hp = _rms(x, pg_ref[...]).astype(BF16)
        gate_p = jax.nn.sigmoid(_dot(hp, wpg_ref[...]))
        o_ref[...] = x + gate_p * _dot(p_ref[...].astype(BF16), wpp_ref[...])


def _ffn_ple(x2, gain, wg, wu, wd, p2, ple_gain, wpg, wpp, tm=512):
    n = x2.shape[0]
    tok = pl.BlockSpec((tm, D_MODEL), lambda i, c: (i, 0))
    vec = pl.BlockSpec((1, D_MODEL), lambda i, c: (0, 0))
    return pl.pallas_call(
        _ffn_kernel,
        grid=(n // tm, D_FF // D_FF_EXPERT),
        in_specs=[tok, vec,
                  pl.BlockSpec((D_MODEL, D_FF_EXPERT), lambda i, c: (0, c)),
                  pl.BlockSpec((D_MODEL, D_FF_EXPERT), lambda i, c: (0, c)),
                  pl.BlockSpec((D_FF_EXPERT, D_MODEL), lambda i, c: (c, 0)),
                  pl.BlockSpec((tm, PLE_DIM), lambda i, c: (i, 0)), vec,
                  pl.BlockSpec(wpg.shape, lambda i, c: (0, 0)),
                  pl.BlockSpec(wpp.shape, lambda i, c: (0, 0))],
        out_specs=tok,
        out_shape=jax.ShapeDtypeStruct((n, D_MODEL), F32),
        name="ffn_dense_ple",
        scratch_shapes=[pltpu.VMEM((tm, D_MODEL), BF16), pltpu.VMEM((tm, D_MODEL), F32)],
        compiler_params=_cparams(("parallel", "arbitrary")),
    )(x2, gain, wg, wu, wd, p2, ple_gain, wpg, wpp)


MOE_BLOCK = 2048
MOE_TILE = 256
MOE_TILES = -(-(2 * MOE_BLOCK + N_EXPERTS * (MOE_TILE - 1)) // MOE_TILE)
MOE_META_ROWS = 32
assert MOE_TILES <= MOE_META_ROWS
MOE_CHUNK = 1024
MOE_VMEM_LIMIT = 60 * 1024 * 1024


def _router_kernel(x_ref, g_ref, wh_ref, wm_ref, wl_ref, h_ref, rc_ref, rr_ref, meta_ref):
    h = _rms(x_ref[...], g_ref[...])
    h_ref[...] = h.astype(BF16)
    h_hi, h_mid, h_lo = _split3(h)
    w_hi, w_mid, w_lo = wh_ref[...], wm_ref[...], wl_ref[...]
    logits = (_dot(h_hi, w_hi) + (_dot(h_hi, w_mid) + _dot(h_mid, w_hi))
              + (_dot(h_hi, w_lo) + _dot(h_mid, w_mid) + _dot(h_lo, w_hi)))
    lane = _lane_iota()
    is_expert = lane < N_EXPERTS
    logits = jnp.where(is_expert, logits, NEG_INF)
    v1 = jnp.max(logits, axis=1, keepdims=True)
    i1 = jnp.min(jnp.where(logits == v1, lane, LANES), axis=1, keepdims=True)
    first = lane == i1
    rest = jnp.where(first, NEG_INF, logits)
    v2 = jnp.max(rest, axis=1, keepdims=True)
    i2 = jnp.min(jnp.where(rest == v2, lane, LANES), axis=1, keepdims=True)
    second = lane == i2
    e2 = jnp.exp(v2 - v1)
    w1 = 1.0 / (1.0 + e2)
    w2 = e2 / (1.0 + e2)

    sel = jnp.where(first, 1.0, jnp.where(second, 1.0, 0.0))
    row = lax.broadcasted_iota(jnp.int32, (CUMSUM_BLOCK, CUMSUM_BLOCK), 0)
    col = lax.broadcasted_iota(jnp.int32, (CUMSUM_BLOCK, CUMSUM_BLOCK), 1)
    tri = jnp.where(col < row, 1.0, 0.0).astype(BF16)
    carry = jnp.zeros((1, LANES), F32)
    ranks = []
    for blk in range(sel.shape[0] // CUMSUM_BLOCK):
        part = sel[blk * CUMSUM_BLOCK:(blk + 1) * CUMSUM_BLOCK]
        ranks.append(_dot(tri, part.astype(BF16)) + carry)
        carry = carry + jnp.sum(part, axis=0, keepdims=True)
    rank = jnp.concatenate(ranks, axis=0)
    padded = jnp.ceil(carry / MOE_TILE) * MOE_TILE
    er = lax.broadcasted_iota(jnp.int32, (LANES, LANES), 0)
    ec = lax.broadcasted_iota(jnp.int32, (LANES, LANES), 1)
    before = jnp.where(er < ec, 1.0, 0.0).astype(BF16)
    start = _dot(jnp.broadcast_to(padded, (8, LANES)).astype(BF16), before)[0:1]
    slot = start + rank
    dest1 = jnp.sum(jnp.where(first, slot, 0.0), axis=1, keepdims=True)
    dest2 = jnp.sum(jnp.where(second, slot, 0.0), axis=1, keepdims=True)
    routing = jnp.where(lane == 0, dest1, jnp.where(lane == 1, dest2,
                        jnp.where(lane == 2, w1, jnp.where(lane == 3, w2, 0.0))))
    rc_ref[...] = routing
    rrow = routing.T[0:8]
    rr_ref[...] = rrow

    end = start + padded
    tile_row = lax.broadcasted_iota(jnp.int32, (MOE_META_ROWS, LANES), 0).astype(F32) * MOE_TILE
    passed = jnp.sum(jnp.where(is_expert, jnp.where(tile_row >= end, 1.0, 0.0), 0.0),
                     axis=1, keepdims=True)
    last = jnp.max(jnp.where(is_expert, jnp.where(padded > 0, lane.astype(F32), 0.0), 0.0),
                   axis=1, keepdims=True)
    total = jnp.sum(jnp.where(is_expert, padded, 0.0), axis=1, keepdims=True)
    active = jnp.where(tile_row < total, 1.0, 0.0)
    nblock = rrow.shape[1]
    tok = lax.broadcasted_iota(jnp.int32, (MOE_META_ROWS, nblock), 1).astype(F32)
    tile_id = lax.broadcasted_iota(jnp.int32, (MOE_META_ROWS, nblock), 0).astype(F32)
    in_tile = jnp.where(jnp.floor(rrow[0:1] / MOE_TILE) == tile_id, 1.0,
                        jnp.where(jnp.floor(rrow[1:2] / MOE_TILE) == tile_id, 1.0, 0.0))
    first_tok = jnp.min(jnp.where(in_tile > 0, tok, float(nblock)), axis=1, keepdims=True)
    last_tok = jnp.max(jnp.where(in_tile > 0, tok, -1.0), axis=1, keepdims=True)
    chunk_lo = jnp.floor(first_tok / MOE_CHUNK)
    chunk_hi = jnp.floor(last_tok / MOE_CHUNK)
    meta = jnp.where(lane == 0, jnp.minimum(passed, last),
                     jnp.where(lane == 1, active,
                               jnp.where(lane == 2, chunk_lo, jnp.where(lane == 3, chunk_hi, 0.0))))
    meta_ref[...] = meta.astype(jnp.int32)


def _router(x2, gain, w_router):
    n = x2.shape[0]
    nblk = n // MOE_BLOCK
    w_pad = jnp.zeros((D_MODEL, LANES), F32).at[:, :N_EXPERTS].set(w_router)
    w_hi = w_pad.astype(BF16)
    r1 = w_pad - w_hi.astype(F32)
    w_mid = r1.astype(BF16)
    w_lo = (r1 - w_mid.astype(F32)).astype(BF16)
    wspec = pl.BlockSpec((D_MODEL, LANES), lambda i: (0, 0))
    return pl.pallas_call(
        _router_kernel,
        grid=(nblk,),
        in_specs=[pl.BlockSpec((MOE_BLOCK, D_MODEL), lambda i: (i, 0)),
                  pl.BlockSpec((1, D_MODEL), lambda i: (0, 0)), wspec, wspec, wspec],
        out_specs=[pl.BlockSpec((MOE_BLOCK, D_MODEL), lambda i: (i, 0)),
                   pl.BlockSpec((MOE_BLOCK, LANES), lambda i: (i, 0)),
                   pl.BlockSpec((None, 8, MOE_BLOCK), lambda i: (i, 0, 0)),
                   pl.BlockSpec((None, MOE_META_ROWS, LANES), lambda i: (i, 0, 0))],
        out_shape=[jax.ShapeDtypeStruct((n, D_MODEL), BF16),
                   jax.ShapeDtypeStruct((n, LANES), F32),
                   jax.ShapeDtypeStruct((nblk, 8, MOE_BLOCK), F32),
                   jax.ShapeDtypeStruct((nblk, MOE_META_ROWS, LANES), jnp.int32)],
        name="router",
        compiler_params=_cparams(("parallel",)),
    )(x2, gain, w_hi, w_mid, w_lo)


def _moe_kernel(te_ref, ta_ref, lo_ref, hi_ref, h_ref, rrow_ref, rcol_ref, wg_ref, wu_ref, wd_ref,
                o_ref, xg_sc, w_sc):
    b = pl.program_id(0)
    r = pl.program_id(1)

    @pl.when(r == 0)
    def _():
        o_ref[...] = jnp.zeros(o_ref.shape, F32)

    @pl.when(ta_ref[b, r] == 1)
    def _():
        base = (r * MOE_TILE).astype(F32)
        chunk_lo = lo_ref[b, r]
        chunk_hi = hi_ref[b, r]
        slot_col = lax.broadcasted_iota(jnp.int32, (MOE_TILE, 1), 0).astype(F32) + base
        slot_row = lax.broadcasted_iota(jnp.int32, (1, MOE_TILE), 1).astype(F32) + base
        xg_sc[...] = jnp.zeros(xg_sc.shape, F32)
        w_sc[...] = jnp.zeros(w_sc.shape, F32)

        for c in range(MOE_BLOCK // MOE_CHUNK):
            toks = slice(c * MOE_CHUNK, (c + 1) * MOE_CHUNK)

            @pl.when((chunk_lo <= c) & (c <= chunk_hi))
            def _(toks=toks):
                hit1 = rrow_ref[0:1, toks] == slot_col
                hit2 = rrow_ref[1:2, toks] == slot_col
                onehot = jnp.where(hit1, 1.0, jnp.where(hit2, 1.0, 0.0)).astype(BF16)
                xg_sc[...] += _dot(onehot, h_ref[toks, :])
                w_sc[...] += jnp.sum(
                    jnp.where(hit1, rrow_ref[2:3, toks], jnp.where(hit2, rrow_ref[3:4, toks], 0.0)),
                    axis=1, keepdims=True)

        xg = xg_sc[...].astype(BF16)
        gate = _dot(xg, wg_ref[...])
        up = _dot(xg, wu_ref[...])
        act = gate * jax.nn.sigmoid(gate) * up
        y = (_dot(act.astype(BF16), wd_ref[...]) * w_sc[...]).astype(BF16)

        for c in range(MOE_BLOCK // MOE_CHUNK):
            toks = slice(c * MOE_CHUNK, (c + 1) * MOE_CHUNK)

            @pl.when((chunk_lo <= c) & (c <= chunk_hi))
            def _(toks=toks):
                back = jnp.where(rcol_ref[toks, 0:1] == slot_row, 1.0,
                                 jnp.where(rcol_ref[toks, 1:2] == slot_row, 1.0, 0.0)).astype(BF16)
                o_ref[toks, :] += _dot(back, y)


def _moe(h2, rrow, rcol, meta, wg, wu, wd):
    n = h2.shape[0]
    nblk = n // MOE_BLOCK
    tile_tables = [meta[:, :MOE_TILES, k] for k in range(4)]

    def blocked(shape, index):
        return pl.BlockSpec(shape, lambda b, r, te, ta, lo, hi: index(b))

    def wspec(shape):
        return pl.BlockSpec((None,) + shape, lambda b, r, te, ta, lo, hi: (te[b, r], 0, 0))

    grid_spec = pltpu.PrefetchScalarGridSpec(
        num_scalar_prefetch=len(tile_tables),
        grid=(nblk, MOE_TILES),
        in_specs=[blocked((MOE_BLOCK, D_MODEL), lambda b: (b, 0)),
                  blocked((None, 8, MOE_BLOCK), lambda b: (b, 0, 0)),
                  blocked((MOE_BLOCK, LANES), lambda b: (b, 0)),
                  wspec((D_MODEL, D_FF_EXPERT)), wspec((D_MODEL, D_FF_EXPERT)),
                  wspec((D_FF_EXPERT, D_MODEL))],
        out_specs=blocked((MOE_BLOCK, D_MODEL), lambda b: (b, 0)),
        scratch_shapes=[pltpu.VMEM((MOE_TILE, D_MODEL), F32), pltpu.VMEM((MOE_TILE, 1), F32)],
    )
    return pl.pallas_call(
        _moe_kernel,
        grid_spec=grid_spec,
        out_shape=jax.ShapeDtypeStruct((n, D_MODEL), F32),
        name="moe_experts",
        compiler_params=pltpu.CompilerParams(dimension_semantics=("parallel", "arbitrary"),
                                             vmem_limit_bytes=MOE_VMEM_LIMIT),
    )(*tile_tables, h2, rrow, rcol, wg, wu, wd)


def _moe_layer(x2, gain, w_router, wg, wu, wd):
    h2, rcol, rrow, meta = _router(x2, gain, w_router)
    return _moe(h2, rrow, rcol, meta, wg, wu, wd)


def _ple_kernel(x_ref, d_ref, p_ref, g_ref, wg_ref, wp_ref, fg_ref, o_ref, *, final):
    x = x_ref[...] + d_ref[...]
    h = _rms(x, g_ref[...]).astype(BF16)
    gate = jax.nn.sigmoid(_dot(h, wg_ref[...]))
    y = x + gate * _dot(p_ref[...].astype(BF16), wp_ref[...])
    if final:
        y = _rms(y, fg_ref[...])
    o_ref[...] = y


def _ple(x2, delta, p2, gain, wg, wp, final_gain, final, tm=512):
    n = x2.shape[0]
    vec = pl.BlockSpec((1, D_MODEL), lambda i: (0, 0))
    tok = pl.BlockSpec((tm, D_MODEL), lambda i: (i, 0))
    return pl.pallas_call(
        functools.partial(_ple_kernel, final=final),
        grid=(n // tm,),
        in_specs=[tok, tok,
                  pl.BlockSpec((tm, PLE_DIM), lambda i: (i, 0)), vec,
                  pl.BlockSpec(wg.shape, lambda i: (0, 0)),
                  pl.BlockSpec(wp.shape, lambda i: (0, 0)), vec],
        out_specs=pl.BlockSpec((tm, D_MODEL), lambda i: (i, 0)),
        out_shape=jax.ShapeDtypeStruct((n, D_MODEL), F32),
        name="ple_final" if final else "ple",
        compiler_params=_cparams(("parallel",)),
    )(x2, delta, p2, gain, wg, wp, final_gain)


def _prep_in_weights(w_in):
    sizes = (FOX_WIDTH, FOX_WIDTH, FOX_WIDTH, FOX_HEADS, DIL_WIDTH, DIL_WIDTH, DIL_WIDTH,
             MLA_Q_RANK, MLA_KV_RANK, MLA_ROPE_DIM, GATE_WIDTH)
    offs = [0]
    for s in sizes:
        offs.append(offs[-1] + s)
    w_bf = w_in.astype(BF16)
    fq, fk, fv, fz, dq, dk, dv, cq, ckv, kr, gates = [w_bf[..., offs[i]:offs[i + 1]]
                                                      for i in range(len(sizes))]
    scale = HEAD_DIM ** -0.5
    main = jnp.concatenate([gates, dq * scale, dk, fq * scale, fk, fv, dv, cq, ckv], axis=-1)
    pad = jnp.zeros(w_in.shape[:-1] + (LANES - FOX_HEADS,), BF16)
    small = jnp.concatenate([fz, pad] + [kr] * MLA_HEADS, axis=-1)
    return main, small


def _prep_mla_weights(w_uq, w_ukv):
    depth = w_uq.shape[0]
    uq = w_uq.reshape(depth, MLA_Q_RANK, MLA_HEADS, MLA_NOPE_DIM + MLA_ROPE_DIM)
    wq = jnp.concatenate([uq[..., :MLA_NOPE_DIM].reshape(depth, MLA_Q_RANK, -1),
                          uq[..., MLA_NOPE_DIM:].reshape(depth, MLA_Q_RANK, -1)], axis=-1)
    ukv = w_ukv.reshape(depth, MLA_KV_RANK, MLA_HEADS, MLA_NOPE_DIM + MLA_V_DIM)
    wkv = jnp.concatenate([ukv[..., :MLA_NOPE_DIM].reshape(depth, MLA_KV_RANK, -1),
                           ukv[..., MLA_NOPE_DIM:].reshape(depth, MLA_KV_RANK, -1)], axis=-1)
    return wq.astype(BF16), wkv.astype(BF16)


def kernel(x, p, positions, mix_norm, w_in, b_forget, mla_q_norm, mla_kv_norm, w_uq, w_ukv,
           w_br_fox, w_br_dil, w_br_mla, w_out, ffn_norm, w_ffn_gate, w_ffn_up, w_ffn_down,
           w_router, w_exp_gate, w_exp_up, w_exp_down, ple_norm, w_ple_gate, w_ple_proj,
           final_norm):
    b, t, _ = x.shape
    n = b * t
    depth = w_in.shape[0]
    assert depth % 2 == 0

    w_main, w_small = _prep_in_weights(w_in)
    wq_all, wkv_all = _prep_mla_weights(w_uq, w_ukv)
    bias_all = jnp.concatenate(
        [b_forget.astype(F32), jnp.zeros((depth, LANES - FOX_HEADS), F32)], axis=-1)
    tabs = [tab.reshape(n, LANES) for tab in _rope_tables(positions)]
    final_gain = final_norm.reshape(1, D_MODEL)

    x2 = x.reshape(n, D_MODEL)
    for i in range(depth):
        main2, fz, qn, qr, kn, kr4, vv = _inproj(
            x2, mix_norm[i].reshape(1, D_MODEL), w_main[i], w_small[i], tabs,
            mla_q_norm[i].reshape(1, -1), mla_kv_norm[i].reshape(1, -1), wq_all[i], wkv_all[i])
        main3 = main2.reshape(b, t, COL_MLA)

        dcols = _decay(fz.reshape(b, t, LANES), bias_all[i].reshape(1, LANES))
        o_fox = _fox_attention(main3, dcols)
        o_dil = _dil_attention(main3)
        o_mla = _mla_attention(*[a.reshape(b, t, -1) for a in (qn, qr, kn, kr4, vv)])

        x2 = _merge(x2, main2, o_fox.reshape(n, -1), o_dil.reshape(n, -1), o_mla.reshape(n, -1),
                    w_br_fox[i].astype(BF16), w_br_dil[i].astype(BF16), w_br_mla[i].astype(BF16),
                    w_out[i].astype(BF16))

        j = i // 2
        gain = ffn_norm[i].reshape(1, D_MODEL)
        ple_args = (p[i].reshape(n, PLE_DIM), ple_norm[i].reshape(1, D_MODEL),
                    w_ple_gate[i].astype(BF16), w_ple_proj[i].astype(BF16))
        if i % 2 == 0:
            x2 = _ffn_ple(x2, gain, w_ffn_gate[j].astype(BF16), w_ffn_up[j].astype(BF16),
                          w_ffn_down[j].astype(BF16), *ple_args)
        else:
            delta = _moe_layer(x2, gain, w_router[j], w_exp_gate[j].astype(BF16),
                               w_exp_up[j].astype(BF16), w_exp_down[j].astype(BF16))
            x2 = _ple(x2, delta, *ple_args, final_gain, final=(i == depth - 1))
    return x2.reshape(b, t, D_MODEL)
```

```python
import functools

import jax
import jax.numpy as jnp
from jax import lax
from jax.experimental import pallas as pl
from jax.experimental.pallas import tpu as pltpu

F32 = jnp.float32
BF16 = jnp.bfloat16

D_MODEL = 1024
HEAD_DIM = 64
ROPE_THETA = 10000.0
NORM_EPS = 1e-6
FOX_HEADS = 6
FOX_WIDTH = FOX_HEADS * HEAD_DIM
DIL_HEADS = 6
DIL_WIDTH = DIL_HEADS * HEAD_DIM
DIL_PAIRS = ((128, 1), (512, 4), (2048, 16))
DIL_BLOCK = 128
MLA_HEADS = 4
MLA_Q_RANK = 256
MLA_KV_RANK = 256
MLA_NOPE_DIM = 64
MLA_ROPE_DIM = 32
MLA_V_DIM = 64
MLA_WIDTH = MLA_HEADS * MLA_V_DIM
N_BRANCHES = 3
D_FF = 2816
N_EXPERTS = 8
D_FF_EXPERT = 1408
PLE_DIM = 256

LANES = 128
GATE_WIDTH = N_BRANCHES * D_MODEL
COL_GATES = 0
COL_DIL_QK = GATE_WIDTH
COL_FOX = COL_DIL_QK + 2 * DIL_WIDTH
COL_DIL_V = COL_FOX + 3 * FOX_WIDTH
COL_MLA = COL_DIL_V + DIL_WIDTH
MAIN_WIDTH = COL_MLA + MLA_Q_RANK + MLA_KV_RANK
MXU_TILE = 256
assert all(c % MXU_TILE == 0 for c in (COL_DIL_QK, COL_FOX, COL_MLA, MAIN_WIDTH))
SMALL_WIDTH = 2 * LANES
VMEM_LIMIT = 56 * 1024 * 1024

NEG_INF = float("-inf")


def _cparams(sem):
    return pltpu.CompilerParams(dimension_semantics=sem, vmem_limit_bytes=VMEM_LIMIT)


def _rms(x, gain):
    return x * lax.rsqrt(jnp.mean(x * x, axis=-1, keepdims=True) + NORM_EPS) * gain


def _dot(a, b):
    return jnp.dot(a, b, preferred_element_type=F32)


def _dot_nt(a, b):
    return lax.dot_general(a, b, (((1,), (1,)), ((), ())), preferred_element_type=F32)


def _lane_iota():
    return lax.broadcasted_iota(jnp.int32, (1, LANES), 1)


def _rope_tab_kernel(pos_ref, fd_ref, fm_ref, cd_ref, sd_ref, cm_ref, sm_ref):
    pos = pos_ref[...].astype(F32)
    lane = _lane_iota()
    ang = pos * fd_ref[...]
    cd_ref[...] = jnp.cos(ang)
    sd_ref[...] = jnp.sin(ang) * jnp.where((lane % HEAD_DIM) < HEAD_DIM // 2, -1.0, 1.0)
    ang = pos * fm_ref[...]
    cm_ref[...] = jnp.cos(ang)
    sm_ref[...] = jnp.sin(ang) * jnp.where((lane % MLA_ROPE_DIM) < MLA_ROPE_DIM // 2, -1.0, 1.0)


def _rope_tables(positions):
    b, t = positions.shape
    half_d = HEAD_DIM // 2
    half_m = MLA_ROPE_DIM // 2
    inv_d = ROPE_THETA ** (-jnp.arange(half_d, dtype=F32) / half_d)
    inv_m = ROPE_THETA ** (-jnp.arange(half_m, dtype=F32) / half_m)
    fd = jnp.tile(inv_d, LANES // half_d)[None, :]
    fm = jnp.tile(inv_m, LANES // half_m)[None, :]
    tab = jax.ShapeDtypeStruct((b, t, LANES), F32)
    tab_spec = pl.BlockSpec((None, t, LANES), lambda i: (i, 0, 0))
    vec_spec = pl.BlockSpec((1, LANES), lambda i: (0, 0))
    return pl.pallas_call(
        _rope_tab_kernel,
        grid=(b,),
        in_specs=[pl.BlockSpec((None, t, 1), lambda i: (i, 0, 0)), vec_spec, vec_spec],
        out_specs=[tab_spec] * 4,
        out_shape=[tab] * 4,
        name="rope_tables",
        compiler_params=_cparams(("parallel",)),
    )(positions.reshape(b, t, 1), fd, fm)


def _swap_halves(x, width):
    lane = _lane_iota()
    half = width // 2
    return jnp.where((lane % width) < half,
                     pltpu.roll(x, LANES - half, axis=1),
                     pltpu.roll(x, half, axis=1))


def _rope(x, cos, sin_signed, width):
    return x * cos + _swap_halves(x, width) * sin_signed


def _col_chunks(width, step=512):
    return [(c, min(step, width - c)) for c in range(0, width, step)]


MLA_NOPE_WIDTH = MLA_HEADS * MLA_NOPE_DIM
MLA_SCALE = (MLA_NOPE_DIM + MLA_ROPE_DIM) ** -0.5


def _inproj_kernel(x_ref, g_ref, w_ref, ws_ref, cd_ref, sd_ref, cm_ref, sm_ref,
                   gq_ref, gkv_ref, wq_ref, wkv_ref,
                   o_ref, fz_ref, qn_ref, qr_ref, kn_ref, kr_ref, v_ref):
    h = _rms(x_ref[...], g_ref[...]).astype(BF16)
    for lo, hi in ((0, COL_DIL_QK), (COL_FOX, COL_MLA)):
        for c, n in _col_chunks(hi - lo):
            o_ref[:, lo + c:lo + c + n] = _dot(h, w_ref[:, lo + c:lo + c + n]).astype(BF16)
    cos_d = cd_ref[...]
    sin_d = sd_ref[...]
    for c0 in range(COL_DIL_QK, COL_FOX, MXU_TILE):
        y = _dot(h, w_ref[:, c0:c0 + MXU_TILE])
        for c in range(0, MXU_TILE, LANES):
            o_ref[:, c0 + c:c0 + c + LANES] = _rope(y[:, c:c + LANES], cos_d, sin_d,
                                                    HEAD_DIM).astype(BF16)

    cos_m = cm_ref[...]
    sin_m = sm_ref[...]
    lat = _dot(h, w_ref[:, COL_MLA:MAIN_WIDTH])
    hq = _rms(lat[:, :MLA_Q_RANK], gq_ref[...]).astype(BF16)
    q = _dot(hq, wq_ref[...])
    qn_ref[...] = (q[:, :MLA_NOPE_WIDTH] * MLA_SCALE).astype(BF16)
    qr_ref[...] = (_rope(q[:, MLA_NOPE_WIDTH:], cos_m, sin_m, MLA_ROPE_DIM) * MLA_SCALE).astype(BF16)
    hkv = _rms(lat[:, MLA_Q_RANK:], gkv_ref[...]).astype(BF16)
    kv = _dot(hkv, wkv_ref[...])
    kn_ref[...] = kv[:, :MLA_NOPE_WIDTH].astype(BF16)
    v_ref[...] = kv[:, MLA_NOPE_WIDTH:].astype(BF16)

    small = _dot(h, ws_ref[...])
    fz_ref[...] = small[:, :LANES]
    kr_ref[...] = _rope(small[:, LANES:], cos_m, sin_m, MLA_ROPE_DIM).astype(BF16)


def _inproj(x2, gain, w_main, w_small, tabs, gq, gkv, wq, wkv, tm=512):
    n = x2.shape[0]

    def tok(width):
        return pl.BlockSpec((tm, width), lambda i: (i, 0))

    def full(a):
        return pl.BlockSpec(a.shape, lambda i: (0, 0))

    widths = (COL_MLA, LANES, MLA_NOPE_WIDTH, LANES, MLA_NOPE_WIDTH, LANES, MLA_WIDTH)
    dtypes = (BF16, F32, BF16, BF16, BF16, BF16, BF16)
    return pl.pallas_call(
        _inproj_kernel,
        grid=(n // tm,),
        in_specs=[tok(D_MODEL), full(gain), full(w_main), full(w_small)]
        + [tok(LANES)] * 4 + [full(gq), full(gkv), full(wq), full(wkv)],
        out_specs=[tok(w) for w in widths],
        out_shape=[jax.ShapeDtypeStruct((n, w), d) for w, d in zip(widths, dtypes)],
        name="inproj",
        compiler_params=_cparams(("parallel",)),
    )(x2, gain, w_main, w_small, *tabs, gq, gkv, wq, wkv)


CUMSUM_BLOCK = 256
DECAY_PART_STRIDE = 8
DECAY_PARTS = 3


def _split3(a):
    hi = a.astype(BF16)
    r1 = a - hi.astype(F32)
    mid = r1.astype(BF16)
    lo = (r1 - mid.astype(F32)).astype(BF16)
    return hi, mid, lo


def _decay_kernel(z_ref, b_ref, o_ref):
    t = z_ref.shape[0]
    row = lax.broadcasted_iota(jnp.int32, (CUMSUM_BLOCK, CUMSUM_BLOCK), 0)
    col = lax.broadcasted_iota(jnp.int32, (CUMSUM_BLOCK, CUMSUM_BLOCK), 1)
    tri = jnp.where(col <= row, 1.0, 0.0).astype(BF16)
    is_head = _lane_iota() < FOX_HEADS
    carry = jnp.zeros((1, LANES), F32)
    for blk in range(t // CUMSUM_BLOCK):
        sl = slice(blk * CUMSUM_BLOCK, (blk + 1) * CUMSUM_BLOCK)
        z = z_ref[sl, :] + b_ref[...]
        logf = -(jnp.maximum(-z, 0.0) + jnp.log1p(jnp.exp(-jnp.abs(z))))
        hi, mid, lo = _split3(logf)
        cs = _dot(tri, hi) + _dot(tri, mid) + _dot(tri, lo) + carry
        carry = cs[CUMSUM_BLOCK - 1:CUMSUM_BLOCK, :]
        parts = _split3(jnp.where(is_head, cs, 0.0))
        packed = parts[0].astype(F32)
        for j in range(1, DECAY_PARTS):
            packed = packed + pltpu.roll(parts[j].astype(F32), j * DECAY_PART_STRIDE, axis=1)
        o_ref[sl, :] = packed.astype(BF16)


def _decay(small3, bias):
    b, t, _ = small3.shape
    return pl.pallas_call(
        _decay_kernel,
        grid=(b,),
        in_specs=[pl.BlockSpec((None, t, LANES), lambda i: (i, 0, 0)),
                  pl.BlockSpec((1, LANES), lambda i: (0, 0))],
        out_specs=pl.BlockSpec((None, t, LANES), lambda i: (i, 0, 0)),
        out_shape=jax.ShapeDtypeStruct((b, t, LANES), BF16),
        name="decay",
        compiler_params=_cparams(("parallel",)),
    )(small3, bias)


def _pair_softmax(s, m_prev, ok_cols):
    nc = s.shape[1] // LANES
    cols = [s[:, c * LANES:(c + 1) * LANES] for c in range(nc)]
    if ok_cols is not None:
        cols = [jnp.where(ok, col, NEG_INF) for ok, col in zip(ok_cols, cols)]
    cmax = cols[0]
    for col in cols[1:]:
        cmax = jnp.maximum(cmax, col)
    m_new = jnp.maximum(m_prev, jnp.max(cmax, axis=1, keepdims=True))
    alpha = jnp.exp(m_prev - m_new)
    p = jnp.concatenate([jnp.exp(col - m_new).astype(BF16) for col in cols], axis=1)
    return m_new, alpha, p


def _causal_pair_attention(qs, keys_at, vals_at, qi, o_ref, tq):
    lane = _lane_iota()
    head0 = lane < HEAD_DIM
    rows = 2 * tq

    def scores(j):
        return _dot_nt(qs, keys_at(j))

    def weighted_values(p, j):
        v = vals_at(j)
        one = jnp.ones_like(v)
        pv0 = _dot(p[:tq], jnp.where(head0, v, one))
        pv1 = _dot(p[tq:], jnp.where(head0, one, v))
        return jnp.concatenate([pv0, pv1], axis=0)

    s = scores(0)
    m_prev = jnp.full((rows, LANES), NEG_INF, F32)
    acc = jnp.zeros((rows, LANES), F32)
    for j in range(qi):
        s_next = scores(j + 1)
        m_prev, alpha, p = _pair_softmax(s, m_prev, None)
        acc = alpha * acc + weighted_values(p, j)
        s = s_next

    r_i = lax.broadcasted_iota(jnp.int32, (tq, LANES), 0)
    c_i = lax.broadcasted_iota(jnp.int32, (tq, LANES), 1)
    ok_cols = []
    for c in range(tq // LANES):
        ok = c_i + c * LANES <= r_i
        ok_cols.append(jnp.concatenate([ok, ok], axis=0))
    _, alpha, p = _pair_softmax(s, m_prev, ok_cols)
    acc = alpha * acc + weighted_values(p, qi)
    out = acc / pltpu.roll(acc, HEAD_DIM, axis=1)
    o_ref[...] = jnp.where(head0, out[:tq], out[tq:]).astype(o_ref.dtype)


def _block_rows(j, tq):
    return slice(j * tq, (j + 1) * tq)


def _fox_kernel(q_ref, k_ref, v_ref, d_ref, o_ref, *, tq):
    p = pl.program_id(1)
    lane = _lane_iota()

    def keys_at(j):
        rows = _block_rows(j, tq)
        return jnp.concatenate([k_ref[rows, :], d_ref[rows, :]], axis=1)

    def vals_at(j):
        return v_ref[_block_rows(j, tq), :]

    for qi in range(q_ref.shape[0] // tq):
        q = q_ref[qi * tq:(qi + 1) * tq, :]
        zero = jnp.zeros_like(q)
        halves = []
        for h in range(2):
            head = 2 * p + h
            pick = (lane % DECAY_PART_STRIDE == head) & (lane < DECAY_PARTS * DECAY_PART_STRIDE)
            neg = jnp.broadcast_to(jnp.where(pick, -1.0, 0.0).astype(BF16), q.shape)
            q_h = jnp.where((lane // HEAD_DIM) == h, q, zero)
            halves.append(jnp.concatenate([q_h, neg], axis=1))
        qs = jnp.concatenate(halves, axis=0)
        _causal_pair_attention(qs, keys_at, vals_at, qi, o_ref.at[qi * tq:(qi + 1) * tq, :], tq)


def _fox_attention(main3, dcols, tq=512):
    b, t, _ = main3.shape
    cb = COL_FOX // LANES
    npair = FOX_HEADS // 2

    def seq(col):
        return pl.BlockSpec((None, t, LANES), lambda bi, p: (bi, 0, col(p)))

    return pl.pallas_call(
        functools.partial(_fox_kernel, tq=tq),
        grid=(b, npair),
        in_specs=[
            seq(lambda p: cb + p),
            seq(lambda p: cb + npair + p),
            seq(lambda p: cb + 2 * npair + p),
            seq(lambda p: 0),
        ],
        out_specs=pl.BlockSpec((None, t, LANES), lambda bi, p: (bi, 0, p)),
        out_shape=jax.ShapeDtypeStruct((b, t, FOX_WIDTH), BF16),
        name="fox_attention",
        compiler_params=_cparams(("parallel", "parallel")),
    )(main3, main3, main3, dcols)


def _mla_kernel(qn_ref, qr_ref, kn_ref, kr_ref, v_ref, o_ref, *, tq):
    p = pl.program_id(1)
    lane = _lane_iota()

    def keys_at(j):
        rows = _block_rows(j, tq)
        return jnp.concatenate([kn_ref[rows, :], kr_ref[rows, :]], axis=1)

    def vals_at(j):
        return v_ref[_block_rows(j, tq), :]

    for qi in range(qn_ref.shape[0] // tq):
        qn = qn_ref[qi * tq:(qi + 1) * tq, :]
        qr = qr_ref[qi * tq:(qi + 1) * tq, :]
        zero = jnp.zeros_like(qn)
        halves = []
        for h in range(2):
            nope_h = jnp.where((lane // MLA_NOPE_DIM) == h, qn, zero)
            rope_h = jnp.where((lane // MLA_ROPE_DIM) == 2 * p + h, qr, zero)
            halves.append(jnp.concatenate([nope_h, rope_h], axis=1))
        qs = jnp.concatenate(halves, axis=0)
        _causal_pair_attention(qs, keys_at, vals_at, qi, o_ref.at[qi * tq:(qi + 1) * tq, :], tq)


def _mla_attention(qn, qr, kn, kr4, vv, tq=512):
    b, t, _ = qn.shape
    npair = MLA_HEADS // 2
    pair = pl.BlockSpec((None, t, LANES), lambda bi, p: (bi, 0, p))
    shared = pl.BlockSpec((None, t, LANES), lambda bi, p: (bi, 0, 0))
    return pl.pallas_call(
        functools.partial(_mla_kernel, tq=tq),
        grid=(b, npair),
        in_specs=[pair, shared, pair, shared, pair],
        out_specs=pair,
        out_shape=jax.ShapeDtypeStruct((b, t, MLA_WIDTH), BF16),
        name="mla_attention",
        compiler_params=_cparams(("parallel", "parallel")),
    )(qn, qr, kn, kr4, vv)


DIL_UNROLL = 8


def _dil_kernel(q_ref, k_ref, v_ref, o_ref, qf, kf, vf, qb, kb, vb, accb, mb, lb):
    t = q_ref.shape[0]
    blk = DIL_BLOCK
    lane = _lane_iota()
    head0 = lane < HEAD_DIM
    qf[...] = q_ref[...].astype(F32)
    kf[...] = k_ref[...].astype(F32)
    vf[...] = v_ref[...].astype(F32)

    for g, (_, rate) in enumerate(DIL_PAIRS):
        length = t // rate
        kb[g, 0:blk, :] = jnp.zeros((blk, LANES), BF16)
        vb[g, 0:blk, :] = jnp.zeros((blk, LANES), BF16)
        if rate == 1:
            qb[g, blk:, :] = q_ref[...]
            kb[g, blk:, :] = k_ref[...]
            vb[g, blk:, :] = v_ref[...]
            continue
        for res in range(rate):
            dst = slice(blk + res * length, blk + (res + 1) * length)
            src = pl.ds(res, length, stride=rate)
            qb[g, dst, :] = qf[src, :].astype(BF16)
            kb[g, dst, :] = kf[src, :].astype(BF16)
            vb[g, dst, :] = vf[src, :].astype(BF16)

    qi2 = lax.broadcasted_iota(jnp.int32, (blk, 2 * blk), 0)
    kj2 = lax.broadcasted_iota(jnp.int32, (blk, 2 * blk), 1)
    band = (kj2 >= qi2) & (kj2 <= qi2 + blk)
    bias_full = jnp.where(band, 0.0, NEG_INF)
    bias_first = jnp.where(band & (kj2 >= blk), 0.0, NEG_INF)
    qi1 = lax.broadcasted_iota(jnp.int32, (blk, blk), 0)
    kj1 = lax.broadcasted_iota(jnp.int32, (blk, blk), 1)
    bias_single = jnp.where(kj1 <= qi1, 0.0, NEG_INF)

    for g, (window, rate) in enumerate(DIL_PAIRS):
        assert window // rate == blk
        nb = t // rate // blk
        nblocks = t // blk

        def body(j, carry, g=g, nb=nb):
            base = pl.multiple_of(j * blk, blk)
            q = qb[g, pl.ds(base + blk, blk), :]
            zero = jnp.zeros_like(q)
            qs = jnp.concatenate([jnp.where(head0, q, zero), jnp.where(head0, zero, q)], axis=0)
            if nb == 1:
                keys = kb[g, pl.ds(base + blk, blk), :]
                vals = vb[g, pl.ds(base + blk, blk), :]
                bias = bias_single
            else:
                keys = kb[g, pl.ds(base, 2 * blk), :]
                vals = vb[g, pl.ds(base, 2 * blk), :]
                bias = jnp.where((j % nb) != 0, bias_full, bias_first)
            s = _dot_nt(qs, keys)
            ps, ms = [], []
            for h in range(2):
                sh = s[h * blk:(h + 1) * blk] + bias
                m = jnp.max(sh, axis=1, keepdims=True)
                ps.append(jnp.exp(sh - m).astype(BF16))
                ms.append(m)
            pv = _dot(jnp.concatenate(ps, axis=0),
                      jnp.concatenate([vals, jnp.ones_like(vals)], axis=1))
            rows = pl.ds(base, blk)
            accb[g, rows, :] = jnp.where(head0, pv[:blk, :LANES], pv[blk:, :LANES])
            mb[g, rows, :] = jnp.where(head0, ms[0], ms[1])
            lb[g, rows, :] = jnp.where(head0, pv[:blk, LANES:], pv[blk:, LANES:])
            return carry

        lax.fori_loop(0, nblocks, body, 0, unroll=DIL_UNROLL)

    rate_max = max(rate for _, rate in DIL_PAIRS)
    length = t // rate_max
    for res in range(rate_max):
        m_all, l_all, a_all = [], [], []
        for g, (_, rate) in enumerate(DIL_PAIRS):
            start = (res % rate) * (t // rate) + res // rate
            step = rate_max // rate
            rows = pl.ds(start, length) if step == 1 else pl.ds(start, length, stride=step)
            m_all.append(mb[g, rows, :])
            l_all.append(lb[g, rows, :])
            a_all.append(accb[g, rows, :])
        m_max = jnp.maximum(jnp.maximum(m_all[0], m_all[1]), m_all[2])
        ws = [jnp.exp(m - m_max) for m in m_all]
        num = ws[0] * a_all[0] + ws[1] * a_all[1] + ws[2] * a_all[2]
        den = ws[0] * l_all[0] + ws[1] * l_all[1] + ws[2] * l_all[2]
        o_ref[pl.ds(res, length, stride=rate_max), :] = num / den


def _dil_attention(main3):
    b, t, _ = main3.shape
    assert all(t % (rate * DIL_BLOCK) == 0 for _, rate in DIL_PAIRS)
    npair = DIL_HEADS // 2
    nbr = len(DIL_PAIRS)

    def col(start):
        return pl.BlockSpec((None, t, LANES), lambda bi, p: (bi, 0, start // LANES + p))

    return pl.pallas_call(
        _dil_kernel,
        grid=(b, npair),
        in_specs=[col(COL_DIL_QK), col(COL_DIL_QK + DIL_WIDTH), col(COL_DIL_V)],
        out_specs=pl.BlockSpec((None, t, LANES), lambda bi, p: (bi, 0, p)),
        out_shape=jax.ShapeDtypeStruct((b, t, DIL_WIDTH), F32),
        name="dilated_attention",
        scratch_shapes=[pltpu.VMEM((t, LANES), F32)] * 3
        + [pltpu.VMEM((nbr, t + DIL_BLOCK, LANES), BF16)] * 3
        + [pltpu.VMEM((nbr, t, LANES), F32)] * 3,
        compiler_params=_cparams(("parallel", "parallel")),
    )(main3, main3, main3)


def _merge_kernel(x_ref, gf_ref, gd_ref, gm_ref, of_ref, od_ref, om_ref,
                  wf_ref, wd_ref, wm_ref, wo_ref, o_ref):
    merged = (jax.nn.sigmoid(gf_ref[...].astype(F32)) * _dot(of_ref[...], wf_ref[...])
              + jax.nn.sigmoid(gd_ref[...].astype(F32)) * _dot(od_ref[...].astype(BF16), wd_ref[...])
              + jax.nn.sigmoid(gm_ref[...].astype(F32)) * _dot(om_ref[...], wm_ref[...]))
    o_ref[...] = x_ref[...] + _dot(merged.astype(BF16), wo_ref[...])


def _merge(x2, main2, o_fox, o_dil, o_mla, wf, wd, wm, wo, tm=512):
    n = x2.shape[0]

    def tok(width, col=0):
        return pl.BlockSpec((tm, width), lambda i: (i, col))

    def full(w):
        return pl.BlockSpec(w.shape, lambda i: (0, 0))

    return pl.pallas_call(
        _merge_kernel,
        grid=(n // tm,),
        in_specs=[tok(D_MODEL), tok(D_MODEL, 0), tok(D_MODEL, 1), tok(D_MODEL, 2),
                  tok(FOX_WIDTH), tok(DIL_WIDTH), tok(MLA_WIDTH),
                  full(wf), full(wd), full(wm), full(wo)],
        out_specs=tok(D_MODEL),
        out_shape=jax.ShapeDtypeStruct((n, D_MODEL), F32),
        name="merge_outproj",
        compiler_params=_cparams(("parallel",)),
    )(x2, main2, main2, main2, o_fox, o_dil, o_mla, wf, wd, wm, wo)


def _ffn_kernel(x_ref, g_ref, wg_ref, wu_ref, wd_ref, p_ref, pg_ref, wpg_ref, wpp_ref,
                o_ref, h_sc, acc_sc):
    c = pl.program_id(1)

    @pl.when(c == 0)
    def _():
        x = x_ref[...]
        h_sc[...] = _rms(x, g_ref[...]).astype(BF16)
        acc_sc[...] = x

    h = h_sc[...]
    gate = _dot(h, wg_ref[...])
    up = _dot(h, wu_ref[...])
    act = gate * jax.nn.sigmoid(gate) * up
    acc_sc[...] += _dot(act.astype(BF16), wd_ref[...])

    @pl.when(c == pl.num_programs(1) - 1)
    def _():
        x = acc_sc[...]
        hp = _rms(x, pg_ref[...]).astype(BF16)
        gate_p = jax.nn.sigmoid(_dot(hp, wpg_ref[...]))
        o_ref[...] = x + gate_p * _dot(p_ref[...].astype(BF16), wpp_ref[...])


def _ffn_ple(x2, gain, wg, wu, wd, p2, ple_gain, wpg, wpp, tm=512):
    n = x2.shape[0]
    tok = pl.BlockSpec((tm, D_MODEL), lambda i, c: (i, 0))
    vec = pl.BlockSpec((1, D_MODEL), lambda i, c: (0, 0))
    return pl.pallas_call(
        _ffn_kernel,
        grid=(n // tm, D_FF // D_FF_EXPERT),
        in_specs=[tok, vec,
                  pl.BlockSpec((D_MODEL, D_FF_EXPERT), lambda i, c: (0, c)),
                  pl.BlockSpec((D_MODEL, D_FF_EXPERT), lambda i, c: (0, c)),
                  pl.BlockSpec((D_FF_EXPERT, D_MODEL), lambda i, c: (c, 0)),
                  pl.BlockSpec((tm, PLE_DIM), lambda i, c: (i, 0)), vec,
                  pl.BlockSpec(wpg.shape, lambda i, c: (0, 0)),
                  pl.BlockSpec(wpp.shape, lambda i, c: (0, 0))],
        out_specs=tok,
        out_shape=jax.ShapeDtypeStruct((n, D_MODEL), F32),
        name="ffn_dense_ple",
        scratch_shapes=[pltpu.VMEM((tm, D_MODEL), BF16), pltpu.VMEM((tm, D_MODEL), F32)],
        compiler_params=_cparams(("parallel", "arbitrary")),
    )(x2, gain, wg, wu, wd, p2, ple_gain, wpg, wpp)


MOE_BLOCK = 2048
MOE_TILE = 256
MOE_TILES = -(-(2 * MOE_BLOCK + N_EXPERTS * (MOE_TILE - 1)) // MOE_TILE)
MOE_META_ROWS = 32
assert MOE_TILES <= MOE_META_ROWS
MOE_CHUNK = 1024
MOE_VMEM_LIMIT = 60 * 1024 * 1024


def _router_kernel(x_ref, g_ref, wh_ref, wm_ref, wl_ref, h_ref, rc_ref, rr_ref, meta_ref):
    h = _rms(x_ref[...], g_ref[...])
    h_ref[...] = h.astype(BF16)
    h_hi, h_mid, h_lo = _split3(h)
    w_hi, w_mid, w_lo = wh_ref[...], wm_ref[...], wl_ref[...]
    logits = (_dot(h_hi, w_hi) + (_dot(h_hi, w_mid) + _dot(h_mid, w_hi))
              + (_dot(h_hi, w_lo) + _dot(h_mid, w_mid) + _dot(h_lo, w_hi)))
    lane = _lane_iota()
    is_expert = lane < N_EXPERTS
    logits = jnp.where(is_expert, logits, NEG_INF)
    v1 = jnp.max(logits, axis=1, keepdims=True)
    i1 = jnp.min(jnp.where(logits == v1, lane, LANES), axis=1, keepdims=True)
    first = lane == i1
    rest = jnp.where(first, NEG_INF, logits)
    v2 = jnp.max(rest, axis=1, keepdims=True)
    i2 = jnp.min(jnp.where(rest == v2, lane, LANES), axis=1, keepdims=True)
    second = lane == i2
    e2 = jnp.exp(v2 - v1)
    w1 = 1.0 / (1.0 + e2)
    w2 = e2 / (1.0 + e2)

    sel = jnp.where(first, 1.0, jnp.where(second, 1.0, 0.0))
    row = lax.broadcasted_iota(jnp.int32, (CUMSUM_BLOCK, CUMSUM_BLOCK), 0)
    col = lax.broadcasted_iota(jnp.int32, (CUMSUM_BLOCK, CUMSUM_BLOCK), 1)
    tri = jnp.where(col < row, 1.0, 0.0).astype(BF16)
    carry = jnp.zeros((1, LANES), F32)
    ranks = []
    for blk in range(sel.shape[0] // CUMSUM_BLOCK):
        part = sel[blk * CUMSUM_BLOCK:(blk + 1) * CUMSUM_BLOCK]
        ranks.append(_dot(tri, part.astype(BF16)) + carry)
        carry = carry + jnp.sum(part, axis=0, keepdims=True)
    rank = jnp.concatenate(ranks, axis=0)
    padded = jnp.ceil(carry / MOE_TILE) * MOE_TILE
    er = lax.broadcasted_iota(jnp.int32, (LANES, LANES), 0)
    ec = lax.broadcasted_iota(jnp.int32, (LANES, LANES), 1)
    before = jnp.where(er < ec, 1.0, 0.0).astype(BF16)
    start = _dot(jnp.broadcast_to(padded, (8, LANES)).astype(BF16), before)[0:1]
    slot = start + rank
    dest1 = jnp.sum(jnp.where(first, slot, 0.0), axis=1, keepdims=True)
    dest2 = jnp.sum(jnp.where(second, slot, 0.0), axis=1, keepdims=True)
    routing = jnp.where(lane == 0, dest1, jnp.where(lane == 1, dest2,
                        jnp.where(lane == 2, w1, jnp.where(lane == 3, w2, 0.0))))
    rc_ref[...] = routing
    rrow = routing.T[0:8]
    rr_ref[...] = rrow

    end = start + padded
    tile_row = lax.broadcasted_iota(jnp.int32, (MOE_META_ROWS, LANES), 0).astype(F32) * MOE_TILE
    passed = jnp.sum(jnp.where(is_expert, jnp.where(tile_row >= end, 1.0, 0.0), 0.0),
                     axis=1, keepdims=True)
    last = jnp.max(jnp.where(is_expert, jnp.where(padded > 0, lane.astype(F32), 0.0), 0.0),
                   axis=1, keepdims=True)
    total = jnp.sum(jnp.where(is_expert, padded, 0.0), axis=1, keepdims=True)
    active = jnp.where(tile_row < total, 1.0, 0.0)
    nblock = rrow.shape[1]
    tok = lax.broadcasted_iota(jnp.int32, (MOE_META_ROWS, nblock), 1).astype(F32)
    tile_id = lax.broadcasted_iota(jnp.int32, (MOE_META_ROWS, nblock), 0).astype(F32)
    in_tile = jnp.where(jnp.floor(rrow[0:1] / MOE_TILE) == tile_id, 1.0,
                        jnp.where(jnp.floor(rrow[1:2] / MOE_TILE) == tile_id, 1.0, 0.0))
    first_tok = jnp.min(jnp.where(in_tile > 0, tok, float(nblock)), axis=1, keepdims=True)
    last_tok = jnp.max(jnp.where(in_tile > 0, tok, -1.0), axis=1, keepdims=True)
    chunk_lo = jnp.floor(first_tok / MOE_CHUNK)
    chunk_hi = jnp.floor(last_tok / MOE_CHUNK)
    meta = jnp.where(lane == 0, jnp.minimum(passed, last),
                     jnp.where(lane == 1, active,
                               jnp.where(lane == 2, chunk_lo, jnp.where(lane == 3, chunk_hi, 0.0))))
    meta_ref[...] = meta.astype(jnp.int32)


def _router(x2, gain, w_router):
    n = x2.shape[0]
    nblk = n // MOE_BLOCK
    w_pad = jnp.zeros((D_MODEL, LANES), F32).at[:, :N_EXPERTS].set(w_router)
    w_hi = w_pad.astype(BF16)
    r1 = w_pad - w_hi.astype(F32)
    w_mid = r1.astype(BF16)
    w_lo = (r1 - w_mid.astype(F32)).astype(BF16)
    wspec = pl.BlockSpec((D_MODEL, LANES), lambda i: (0, 0))
    return pl.pallas_call(
        _router_kernel,
        grid=(nblk,),
        in_specs=[pl.BlockSpec((MOE_BLOCK, D_MODEL), lambda i: (i, 0)),
                  pl.BlockSpec((1, D_MODEL), lambda i: (0, 0)), wspec, wspec, wspec],
        out_specs=[pl.BlockSpec((MOE_BLOCK, D_MODEL), lambda i: (i, 0)),
                   pl.BlockSpec((MOE_BLOCK, LANES), lambda i: (i, 0)),
                   pl.BlockSpec((None, 8, MOE_BLOCK), lambda i: (i, 0, 0)),
                   pl.BlockSpec((None, MOE_META_ROWS, LANES), lambda i: (i, 0, 0))],
        out_shape=[jax.ShapeDtypeStruct((n, D_MODEL), BF16),
                   jax.ShapeDtypeStruct((n, LANES), F32),
                   jax.ShapeDtypeStruct((nblk, 8, MOE_BLOCK), F32),
                   jax.ShapeDtypeStruct((nblk, MOE_META_ROWS, LANES), jnp.int32)],
        name="router",
        compiler_params=_cparams(("parallel",)),
    )(x2, gain, w_hi, w_mid, w_lo)


def _moe_kernel(te_ref, ta_ref, lo_ref, hi_ref, h_ref, rrow_ref, rcol_ref, wg_ref, wu_ref, wd_ref,
                o_ref, xg_sc, w_sc):
    b = pl.program_id(0)
    r = pl.program_id(1)

    @pl.when(r == 0)
    def _():
        o_ref[...] = jnp.zeros(o_ref.shape, F32)

    @pl.when(ta_ref[b, r] == 1)
    def _():
        base = (r * MOE_TILE).astype(F32)
        chunk_lo = lo_ref[b, r]
        chunk_hi = hi_ref[b, r]
        slot_col = lax.broadcasted_iota(jnp.int32, (MOE_TILE, 1), 0).astype(F32) + base
        slot_row = lax.broadcasted_iota(jnp.int32, (1, MOE_TILE), 1).astype(F32) + base
        xg_sc[...] = jnp.zeros(xg_sc.shape, F32)
        w_sc[...] = jnp.zeros(w_sc.shape, F32)

        for c in range(MOE_BLOCK // MOE_CHUNK):
            toks = slice(c * MOE_CHUNK, (c + 1) * MOE_CHUNK)

            @pl.when((chunk_lo <= c) & (c <= chunk_hi))
            def _(toks=toks):
                hit1 = rrow_ref[0:1, toks] == slot_col
                hit2 = rrow_ref[1:2, toks] == slot_col
                onehot = jnp.where(hit1, 1.0, jnp.where(hit2, 1.0, 0.0)).astype(BF16)
                xg_sc[...] += _dot(onehot, h_ref[toks, :])
                w_sc[...] += jnp.sum(
                    jnp.where(hit1, rrow_ref[2:3, toks], jnp.where(hit2, rrow_ref[3:4, toks], 0.0)),
                    axis=1, keepdims=True)

        xg = xg_sc[...].astype(BF16)
        gate = _dot(xg, wg_ref[...])
        up = _dot(xg, wu_ref[...])
        act = gate * jax.nn.sigmoid(gate) * up
        y = (_dot(act.astype(BF16), wd_ref[...]) * w_sc[...]).astype(BF16)

        for c in range(MOE_BLOCK // MOE_CHUNK):
            toks = slice(c * MOE_CHUNK, (c + 1) * MOE_CHUNK)

            @pl.when((chunk_lo <= c) & (c <= chunk_hi))
            def _(toks=toks):
                back = jnp.where(rcol_ref[toks, 0:1] == slot_row, 1.0,
                                 jnp.where(rcol_ref[toks, 1:2] == slot_row, 1.0, 0.0)).astype(BF16)
                o_ref[toks, :] += _dot(back, y)


def _moe(h2, rrow, rcol, meta, wg, wu, wd):
    n = h2.shape[0]
    nblk = n // MOE_BLOCK
    tile_tables = [meta[:, :MOE_TILES, k] for k in range(4)]

    def blocked(shape, index):
        return pl.BlockSpec(shape, lambda b, r, te, ta, lo, hi: index(b))

    def wspec(shape):
        return pl.BlockSpec((None,) + shape, lambda b, r, te, ta, lo, hi: (te[b, r], 0, 0))

    grid_spec = pltpu.PrefetchScalarGridSpec(
        num_scalar_prefetch=len(tile_tables),
        grid=(nblk, MOE_TILES),
        in_specs=[blocked((MOE_BLOCK, D_MODEL), lambda b: (b, 0)),
                  blocked((None, 8, MOE_BLOCK), lambda b: (b, 0, 0)),
                  blocked((MOE_BLOCK, LANES), lambda b: (b, 0)),
                  wspec((D_MODEL, D_FF_EXPERT)), wspec((D_MODEL, D_FF_EXPERT)),
                  wspec((D_FF_EXPERT, D_MODEL))],
        out_specs=blocked((MOE_BLOCK, D_MODEL), lambda b: (b, 0)),
        scratch_shapes=[pltpu.VMEM((MOE_TILE, D_MODEL), F32), pltpu.VMEM((MOE_TILE, 1), F32)],
    )
    return pl.pallas_call(
        _moe_kernel,
        grid_spec=grid_spec,
        out_shape=jax.ShapeDtypeStruct((n, D_MODEL), F32),
        name="moe_experts",
        compiler_params=pltpu.CompilerParams(dimension_semantics=("parallel", "arbitrary"),
                                             vmem_limit_bytes=MOE_VMEM_LIMIT),
    )(*tile_tables, h2, rrow, rcol, wg, wu, wd)


def _moe_layer(x2, gain, w_router, wg, wu, wd):
    h2, rcol, rrow, meta = _router(x2, gain, w_router)
    return _moe(h2, rrow, rcol, meta, wg, wu, wd)


def _ple_kernel(x_ref, d_ref, p_ref, g_ref, wg_ref, wp_ref, fg_ref, o_ref, *, final):
    x = x_ref[...] + d_ref[...]
    h = _rms(x, g_ref[...]).astype(BF16)
    gate = jax.nn.sigmoid(_dot(h, wg_ref[...]))
    y = x + gate * _dot(p_ref[...].astype(BF16), wp_ref[...])
    if final:
        y = _rms(y, fg_ref[...])
    o_ref[...] = y


def _ple(x2, delta, p2, gain, wg, wp, final_gain, final, tm=512):
    n = x2.shape[0]
    vec = pl.BlockSpec((1, D_MODEL), lambda i: (0, 0))
    tok = pl.BlockSpec((tm, D_MODEL), lambda i: (i, 0))
    return pl.pallas_call(
        functools.partial(_ple_kernel, final=final),
        grid=(n // tm,),
        in_specs=[tok, tok,
                  pl.BlockSpec((tm, PLE_DIM), lambda i: (i, 0)), vec,
                  pl.BlockSpec(wg.shape, lambda i: (0, 0)),
                  pl.BlockSpec(wp.shape, lambda i: (0, 0)), vec],
        out_specs=pl.BlockSpec((tm, D_MODEL), lambda i: (i, 0)),
        out_shape=jax.ShapeDtypeStruct((n, D_MODEL), F32),
        name="ple_final" if final else "ple",
        compiler_params=_cparams(("parallel",)),
    )(x2, delta, p2, gain, wg, wp, final_gain)


def _cast_kernel(x_ref, o_ref):
    o_ref[...] = x_ref[...].astype(BF16)


def _to_bf16(a):
    r, c = a.shape[-2:]
    a3 = a.reshape((-1, r, c))
    spec = pl.BlockSpec((None, r, c), lambda i: (i, 0, 0))
    out = pl.pallas_call(
        _cast_kernel,
        grid=(a3.shape[0],),
        in_specs=[spec],
        out_specs=spec,
        out_shape=jax.ShapeDtypeStruct(a3.shape, BF16),
        name="cast_bf16",
        compiler_params=_cparams(("parallel",)),
    )(a3)
    return out.reshape(a.shape)


IN_SIZES = (FOX_WIDTH, FOX_WIDTH, FOX_WIDTH, FOX_HEADS, DIL_WIDTH, DIL_WIDTH, DIL_WIDTH,
            MLA_Q_RANK, MLA_KV_RANK, MLA_ROPE_DIM, GATE_WIDTH)
IN_WIDTH = sum(IN_SIZES)
IN_PREP_ROWS = 256


def _inproj_weight_kernel(w_ref, main_ref, small_ref):
    w = w_ref[...]
    offs = [0]
    for s in IN_SIZES:
        offs.append(offs[-1] + s)
    fq, fk, fv, fz, dq, dk, dv, cq, ckv, kr, gates = [w[:, offs[i]:offs[i + 1]]
                                                      for i in range(len(IN_SIZES))]
    scale = HEAD_DIM ** -0.5
    main = jnp.concatenate([gates, dq * scale, dk, fq * scale, fk, fv, dv, cq, ckv], axis=1)
    main_ref[...] = main.astype(BF16)
    pad = jnp.zeros((w.shape[0], LANES - FOX_HEADS), F32)
    small_ref[...] = jnp.concatenate([fz, pad] + [kr] * MLA_HEADS, axis=1).astype(BF16)


def _prep_in_weights(w_in):
    depth = w_in.shape[0]
    rb = IN_PREP_ROWS

    def blk(width):
        return pl.BlockSpec((None, rb, width), lambda l, i: (l, i, 0))

    return pl.pallas_call(
        _inproj_weight_kernel,
        grid=(depth, D_MODEL // rb),
        in_specs=[blk(IN_WIDTH)],
        out_specs=[blk(MAIN_WIDTH), blk(SMALL_WIDTH)],
        out_shape=[jax.ShapeDtypeStruct((depth, D_MODEL, MAIN_WIDTH), BF16),
                   jax.ShapeDtypeStruct((depth, D_MODEL, SMALL_WIDTH), BF16)],
        name="inproj_weights",
        compiler_params=_cparams(("parallel", "parallel")),
    )(w_in)


def _prep_mla_weights(w_uq, w_ukv):
    depth = w_uq.shape[0]
    uq = w_uq.reshape(depth, MLA_Q_RANK, MLA_HEADS, MLA_NOPE_DIM + MLA_ROPE_DIM)
    wq = jnp.concatenate([uq[..., :MLA_NOPE_DIM].reshape(depth, MLA_Q_RANK, -1),
                          uq[..., MLA_NOPE_DIM:].reshape(depth, MLA_Q_RANK, -1)], axis=-1)
    ukv = w_ukv.reshape(depth, MLA_KV_RANK, MLA_HEADS, MLA_NOPE_DIM + MLA_V_DIM)
    wkv = jnp.concatenate([ukv[..., :MLA_NOPE_DIM].reshape(depth, MLA_KV_RANK, -1),
                           ukv[..., MLA_NOPE_DIM:].reshape(depth, MLA_KV_RANK, -1)], axis=-1)
    return wq.astype(BF16), wkv.astype(BF16)


def kernel(x, p, positions, mix_norm, w_in, b_forget, mla_q_norm, mla_kv_norm, w_uq, w_ukv,
           w_br_fox, w_br_dil, w_br_mla, w_out, ffn_norm, w_ffn_gate, w_ffn_up, w_ffn_down,
           w_router, w_exp_gate, w_exp_up, w_exp_down, ple_norm, w_ple_gate, w_ple_proj,
           final_norm):
    b, t, _ = x.shape
    n = b * t
    depth = w_in.shape[0]
    assert depth % 2 == 0

    w_main, w_small = _prep_in_weights(w_in)
    wq_all, wkv_all = _prep_mla_weights(w_uq, w_ukv)
    (w_br_fox, w_br_dil, w_br_mla, w_out, w_ffn_gate, w_ffn_up, w_ffn_down, w_exp_gate, w_exp_up,
     w_exp_down, w_ple_gate, w_ple_proj) = [
        _to_bf16(w) for w in (w_br_fox, w_br_dil, w_br_mla, w_out, w_ffn_gate, w_ffn_up, w_ffn_down,
                              w_exp_gate, w_exp_up, w_exp_down, w_ple_gate, w_ple_proj)]
    bias_all = jnp.concatenate(
        [b_forget.astype(F32), jnp.zeros((depth, LANES - FOX_HEADS), F32)], axis=-1)
    tabs = [tab.reshape(n, LANES) for tab in _rope_tables(positions)]
    final_gain = final_norm.reshape(1, D_MODEL)

    x2 = x.reshape(n, D_MODEL)
    for i in range(depth):
        main2, fz, qn, qr, kn, kr4, vv = _inproj(
            x2, mix_norm[i].reshape(1, D_MODEL), w_main[i], w_small[i], tabs,
            mla_q_norm[i].reshape(1, -1), mla_kv_norm[i].reshape(1, -1), wq_all[i], wkv_all[i])
        main3 = main2.reshape(b, t, COL_MLA)

        dcols = _decay(fz.reshape(b, t, LANES), bias_all[i].reshape(1, LANES))
        o_fox = _fox_attention(main3, dcols)
        o_dil = _dil_attention(main3)
        o_mla = _mla_attention(*[a.reshape(b, t, -1) for a in (qn, qr, kn, kr4, vv)])

        x2 = _merge(x2, main2, o_fox.reshape(n, -1), o_dil.reshape(n, -1), o_mla.reshape(n, -1),
                    w_br_fox[i], w_br_dil[i], w_br_mla[i], w_out[i])

        j = i // 2
        gain = ffn_norm[i].reshape(1, D_MODEL)
        ple_args = (p[i].reshape(n, PLE_DIM), ple_norm[i].reshape(1, D_MODEL),
                    w_ple_gate[i], w_ple_proj[i])
        if i % 2 == 0:
            x2 = _ffn_ple(x2, gain, w_ffn_gate[j], w_ffn_up[j], w_ffn_down[j], *ple_args)
        else:
            delta = _moe_layer(x2, gain, w_router[j], w_exp_gate[j], w_exp_up[j], w_exp_down[j])
            x2 = _ple(x2, delta, *ple_args, final_gain, final=(i == depth - 1))
    return x2.reshape(b, t, D_MODEL)
```

```python
import functools

import jax
import jax.numpy as jnp
from jax import lax
from jax.experimental import pallas as pl
from jax.experimental.pallas import tpu as pltpu

F32 = jnp.float32
BF16 = jnp.bfloat16

D_MODEL = 1024
HEAD_DIM = 64
ROPE_THETA = 10000.0
NORM_EPS = 1e-6
FOX_HEADS = 6
FOX_WIDTH = FOX_HEADS * HEAD_DIM
DIL_HEADS = 6
DIL_WIDTH = DIL_HEADS * HEAD_DIM
DIL_PAIRS = ((128, 1), (512, 4), (2048, 16))
DIL_BLOCK = 128
MLA_HEADS = 4
MLA_Q_RANK = 256
MLA_KV_RANK = 256
MLA_NOPE_DIM = 64
MLA_ROPE_DIM = 32
MLA_V_DIM = 64
MLA_WIDTH = MLA_HEADS * MLA_V_DIM
N_BRANCHES = 3
D_FF = 2816
N_EXPERTS = 8
D_FF_EXPERT = 1408
PLE_DIM = 256

LANES = 128
GATE_WIDTH = N_BRANCHES * D_MODEL
COL_GATES = 0
COL_DIL_QK = GATE_WIDTH
COL_FOX = COL_DIL_QK + 2 * DIL_WIDTH
COL_DIL_V = COL_FOX + 3 * FOX_WIDTH
COL_MLA = COL_DIL_V + DIL_WIDTH
MAIN_WIDTH = COL_MLA + MLA_Q_RANK + MLA_KV_RANK
MXU_TILE = 256
assert all(c % MXU_TILE == 0 for c in (COL_DIL_QK, COL_FOX, COL_MLA, MAIN_WIDTH))
SMALL_WIDTH = 2 * LANES
VMEM_LIMIT = 56 * 1024 * 1024

NEG_INF = float("-inf")


def _cparams(sem):
    return pltpu.CompilerParams(dimension_semantics=sem, vmem_limit_bytes=VMEM_LIMIT)


def _rms(x, gain):
    return x * lax.rsqrt(jnp.mean(x * x, axis=-1, keepdims=True) + NORM_EPS) * gain


def _dot(a, b):
    return jnp.dot(a, b, preferred_element_type=F32)


def _dot_nt(a, b):
    return lax.dot_general(a, b, (((1,), (1,)), ((), ())), preferred_element_type=F32)


def _lane_iota():
    return lax.broadcasted_iota(jnp.int32, (1, LANES), 1)


def _rope_tab_kernel(pos_ref, fd_ref, fm_ref, cd_ref, sd_ref, cm_ref, sm_ref):
    pos = pos_ref[...].astype(F32)
    lane = _lane_iota()
    ang = pos * fd_ref[...]
    cd_ref[...] = jnp.cos(ang)
    sd_ref[...] = jnp.sin(ang) * jnp.where((lane % HEAD_DIM) < HEAD_DIM // 2, -1.0, 1.0)
    ang = pos * fm_ref[...]
    cm_ref[...] = jnp.cos(ang)
    sm_ref[...] = jnp.sin(ang) * jnp.where((lane % MLA_ROPE_DIM) < MLA_ROPE_DIM // 2, -1.0, 1.0)


def _rope_tables(positions):
    b, t = positions.shape
    half_d = HEAD_DIM // 2
    half_m = MLA_ROPE_DIM // 2
    inv_d = ROPE_THETA ** (-jnp.arange(half_d, dtype=F32) / half_d)
    inv_m = ROPE_THETA ** (-jnp.arange(half_m, dtype=F32) / half_m)
    fd = jnp.tile(inv_d, LANES // half_d)[None, :]
    fm = jnp.tile(inv_m, LANES // half_m)[None, :]
    tab = jax.ShapeDtypeStruct((b, t, LANES), F32)
    tab_spec = pl.BlockSpec((None, t, LANES), lambda i: (i, 0, 0))
    vec_spec = pl.BlockSpec((1, LANES), lambda i: (0, 0))
    return pl.pallas_call(
        _rope_tab_kernel,
        grid=(b,),
        in_specs=[pl.BlockSpec((None, t, 1), lambda i: (i, 0, 0)), vec_spec, vec_spec],
        out_specs=[tab_spec] * 4,
        out_shape=[tab] * 4,
        name="rope_tables",
        compiler_params=_cparams(("parallel",)),
    )(positions.reshape(b, t, 1), fd, fm)


def _swap_halves(x, width):
    lane = _lane_iota()
    half = width // 2
    return jnp.where((lane % width) < half,
                     pltpu.roll(x, LANES - half, axis=1),
                     pltpu.roll(x, half, axis=1))


def _rope(x, cos, sin_signed, width):
    return x * cos + _swap_halves(x, width) * sin_signed


def _col_chunks(width, step=512):
    return [(c, min(step, width - c)) for c in range(0, width, step)]


MLA_NOPE_WIDTH = MLA_HEADS * MLA_NOPE_DIM
MLA_SCALE = (MLA_NOPE_DIM + MLA_ROPE_DIM) ** -0.5


def _inproj_kernel(x_ref, g_ref, w_ref, ws_ref, cd_ref, sd_ref, cm_ref, sm_ref,
                   gq_ref, gkv_ref, wq_ref, wkv_ref,
                   o_ref, fz_ref, qn_ref, qr_ref, kn_ref, kr_ref, v_ref):
    h = _rms(x_ref[...], g_ref[...]).astype(BF16)
    for lo, hi in ((0, COL_DIL_QK), (COL_FOX, COL_MLA)):
        for c, n in _col_chunks(hi - lo):
            o_ref[:, lo + c:lo + c + n] = _dot_nt(h, w_ref[lo + c:lo + c + n, :]).astype(BF16)
    cos_d = cd_ref[...]
    sin_d = sd_ref[...]
    for c0 in range(COL_DIL_QK, COL_FOX, MXU_TILE):
        y = _dot_nt(h, w_ref[c0:c0 + MXU_TILE, :])
        for c in range(0, MXU_TILE, LANES):
            o_ref[:, c0 + c:c0 + c + LANES] = _rope(y[:, c:c + LANES], cos_d, sin_d,
                                                    HEAD_DIM).astype(BF16)

    cos_m = cm_ref[...]
    sin_m = sm_ref[...]
    lat = _dot_nt(h, w_ref[COL_MLA:MAIN_WIDTH, :])
    hq = _rms(lat[:, :MLA_Q_RANK], gq_ref[...]).astype(BF16)
    q = _dot(hq, wq_ref[...])
    qn_ref[...] = (q[:, :MLA_NOPE_WIDTH] * MLA_SCALE).astype(BF16)
    qr_ref[...] = (_rope(q[:, MLA_NOPE_WIDTH:], cos_m, sin_m, MLA_ROPE_DIM) * MLA_SCALE).astype(BF16)
    hkv = _rms(lat[:, MLA_Q_RANK:], gkv_ref[...]).astype(BF16)
    kv = _dot(hkv, wkv_ref[...])
    kn_ref[...] = kv[:, :MLA_NOPE_WIDTH].astype(BF16)
    v_ref[...] = kv[:, MLA_NOPE_WIDTH:].astype(BF16)

    small = _dot_nt(h, ws_ref[...])
    fz_ref[...] = small[:, :LANES]
    kr_ref[...] = _rope(small[:, LANES:], cos_m, sin_m, MLA_ROPE_DIM).astype(BF16)


def _inproj(x2, gain, layer, w_main_t, w_small_t, tabs, gq, gkv, wq, wkv, tm=512):
    n = x2.shape[0]

    def tok(width):
        return pl.BlockSpec((tm, width), lambda i: (i, 0))

    def full(a):
        return pl.BlockSpec(a.shape, lambda i: (0, 0))

    def of_layer(a):
        return pl.BlockSpec((None,) + a.shape[1:], lambda i: (layer, 0, 0))

    widths = (COL_MLA, LANES, MLA_NOPE_WIDTH, LANES, MLA_NOPE_WIDTH, LANES, MLA_WIDTH)
    dtypes = (BF16, F32, BF16, BF16, BF16, BF16, BF16)
    return pl.pallas_call(
        _inproj_kernel,
        grid=(n // tm,),
        in_specs=[tok(D_MODEL), full(gain), of_layer(w_main_t), of_layer(w_small_t)]
        + [tok(LANES)] * 4 + [full(gq), full(gkv), full(wq), full(wkv)],
        out_specs=[tok(w) for w in widths],
        out_shape=[jax.ShapeDtypeStruct((n, w), d) for w, d in zip(widths, dtypes)],
        name="inproj",
        compiler_params=_cparams(("parallel",)),
    )(x2, gain, w_main_t, w_small_t, *tabs, gq, gkv, wq, wkv)


CUMSUM_BLOCK = 256
DECAY_PART_STRIDE = 8
DECAY_PARTS = 3


def _split3(a):
    hi = a.astype(BF16)
    r1 = a - hi.astype(F32)
    mid = r1.astype(BF16)
    lo = (r1 - mid.astype(F32)).astype(BF16)
    return hi, mid, lo


def _decay_kernel(z_ref, b_ref, o_ref):
    t = z_ref.shape[0]
    row = lax.broadcasted_iota(jnp.int32, (CUMSUM_BLOCK, CUMSUM_BLOCK), 0)
    col = lax.broadcasted_iota(jnp.int32, (CUMSUM_BLOCK, CUMSUM_BLOCK), 1)
    tri = jnp.where(col <= row, 1.0, 0.0).astype(BF16)
    is_head = _lane_iota() < FOX_HEADS
    carry = jnp.zeros((1, LANES), F32)
    for blk in range(t // CUMSUM_BLOCK):
        sl = slice(blk * CUMSUM_BLOCK, (blk + 1) * CUMSUM_BLOCK)
        z = z_ref[sl, :] + b_ref[...]
        logf = -(jnp.maximum(-z, 0.0) + jnp.log1p(jnp.exp(-jnp.abs(z))))
        hi, mid, lo = _split3(logf)
        cs = _dot(tri, hi) + _dot(tri, mid) + _dot(tri, lo) + carry
        carry = cs[CUMSUM_BLOCK - 1:CUMSUM_BLOCK, :]
        parts = _split3(jnp.where(is_head, cs, 0.0))
        packed = parts[0].astype(F32)
        for j in range(1, DECAY_PARTS):
            packed = packed + pltpu.roll(parts[j].astype(F32), j * DECAY_PART_STRIDE, axis=1)
        o_ref[sl, :] = packed.astype(BF16)


def _decay(small3, bias):
    b, t, _ = small3.shape
    return pl.pallas_call(
        _decay_kernel,
        grid=(b,),
        in_specs=[pl.BlockSpec((None, t, LANES), lambda i: (i, 0, 0)),
                  pl.BlockSpec((1, LANES), lambda i: (0, 0))],
        out_specs=pl.BlockSpec((None, t, LANES), lambda i: (i, 0, 0)),
        out_shape=jax.ShapeDtypeStruct((b, t, LANES), BF16),
        name="decay",
        compiler_params=_cparams(("parallel",)),
    )(small3, bias)


def _pair_softmax(s, m_prev, ok_cols):
    nc = s.shape[1] // LANES
    cols = [s[:, c * LANES:(c + 1) * LANES] for c in range(nc)]
    if ok_cols is not None:
        cols = [jnp.where(ok, col, NEG_INF) for ok, col in zip(ok_cols, cols)]
    cmax = cols[0]
    for col in cols[1:]:
        cmax = jnp.maximum(cmax, col)
    m_new = jnp.maximum(m_prev, jnp.max(cmax, axis=1, keepdims=True))
    alpha = jnp.exp(m_prev - m_new)
    p = jnp.concatenate([jnp.exp(col - m_new).astype(BF16) for col in cols], axis=1)
    return m_new, alpha, p


def _causal_pair_attention(qs, keys_at, vals_at, qi, o_ref, tq):
    lane = _lane_iota()
    head0 = lane < HEAD_DIM
    rows = 2 * tq

    def scores(j):
        return _dot_nt(qs, keys_at(j))

    def weighted_values(p, j):
        v = vals_at(j)
        one = jnp.ones_like(v)
        pv0 = _dot(p[:tq], jnp.where(head0, v, one))
        pv1 = _dot(p[tq:], jnp.where(head0, one, v))
        return jnp.concatenate([pv0, pv1], axis=0)

    s = scores(0)
    m_prev = jnp.full((rows, LANES), NEG_INF, F32)
    acc = jnp.zeros((rows, LANES), F32)
    for j in range(qi):
        s_next = scores(j + 1)
        m_prev, alpha, p = _pair_softmax(s, m_prev, None)
        acc = alpha * acc + weighted_values(p, j)
        s = s_next

    r_i = lax.broadcasted_iota(jnp.int32, (tq, LANES), 0)
    c_i = lax.broadcasted_iota(jnp.int32, (tq, LANES), 1)
    ok_cols = []
    for c in range(tq // LANES):
        ok = c_i + c * LANES <= r_i
        ok_cols.append(jnp.concatenate([ok, ok], axis=0))
    _, alpha, p = _pair_softmax(s, m_prev, ok_cols)
    acc = alpha * acc + weighted_values(p, qi)
    out = acc / pltpu.roll(acc, HEAD_DIM, axis=1)
    o_ref[...] = jnp.where(head0, out[:tq], out[tq:]).astype(o_ref.dtype)


def _block_rows(j, tq):
    return slice(j * tq, (j + 1) * tq)


def _fox_kernel(q_ref, k_ref, v_ref, d_ref, o_ref, *, tq):
    p = pl.program_id(1)
    lane = _lane_iota()

    def keys_at(j):
        rows = _block_rows(j, tq)
        return jnp.concatenate([k_ref[rows, :], d_ref[rows, :]], axis=1)

    def vals_at(j):
        return v_ref[_block_rows(j, tq), :]

    for qi in range(q_ref.shape[0] // tq):
        q = q_ref[qi * tq:(qi + 1) * tq, :]
        zero = jnp.zeros_like(q)
        halves = []
        for h in range(2):
            head = 2 * p + h
            pick = (lane % DECAY_PART_STRIDE == head) & (lane < DECAY_PARTS * DECAY_PART_STRIDE)
            neg = jnp.broadcast_to(jnp.where(pick, -1.0, 0.0).astype(BF16), q.shape)
            q_h = jnp.where((lane // HEAD_DIM) == h, q, zero)
            halves.append(jnp.concatenate([q_h, neg], axis=1))
        qs = jnp.concatenate(halves, axis=0)
        _causal_pair_attention(qs, keys_at, vals_at, qi, o_ref.at[qi * tq:(qi + 1) * tq, :], tq)


def _fox_attention(main3, dcols, tq=512):
    b, t, _ = main3.shape
    cb = COL_FOX // LANES
    npair = FOX_HEADS // 2

    def seq(col):
        return pl.BlockSpec((None, t, LANES), lambda bi, p: (bi, 0, col(p)))

    return pl.pallas_call(
        functools.partial(_fox_kernel, tq=tq),
        grid=(b, npair),
        in_specs=[
            seq(lambda p: cb + p),
            seq(lambda p: cb + npair + p),
            seq(lambda p: cb + 2 * npair + p),
            seq(lambda p: 0),
        ],
        out_specs=pl.BlockSpec((None, t, LANES), lambda bi, p: (bi, 0, p)),
        out_shape=jax.ShapeDtypeStruct((b, t, FOX_WIDTH), BF16),
        name="fox_attention",
        compiler_params=_cparams(("parallel", "parallel")),
    )(main3, main3, main3, dcols)


def _mla_kernel(qn_ref, qr_ref, kn_ref, kr_ref, v_ref, o_ref, *, tq):
    p = pl.program_id(1)
    lane = _lane_iota()

    def keys_at(j):
        rows = _block_rows(j, tq)
        return jnp.concatenate([kn_ref[rows, :], kr_ref[rows, :]], axis=1)

    def vals_at(j):
        return v_ref[_block_rows(j, tq), :]

    for qi in range(qn_ref.shape[0] // tq):
        qn = qn_ref[qi * tq:(qi + 1) * tq, :]
        qr = qr_ref[qi * tq:(qi + 1) * tq, :]
        zero = jnp.zeros_like(qn)
        halves = []
        for h in range(2):
            nope_h = jnp.where((lane // MLA_NOPE_DIM) == h, qn, zero)
            rope_h = jnp.where((lane // MLA_ROPE_DIM) == 2 * p + h, qr, zero)
            halves.append(jnp.concatenate([nope_h, rope_h], axis=1))
        qs = jnp.concatenate(halves, axis=0)
        _causal_pair_attention(qs, keys_at, vals_at, qi, o_ref.at[qi * tq:(qi + 1) * tq, :], tq)


def _mla_attention(qn, qr, kn, kr4, vv, tq=512):
    b, t, _ = qn.shape
    npair = MLA_HEADS // 2
    pair = pl.BlockSpec((None, t, LANES), lambda bi, p: (bi, 0, p))
    shared = pl.BlockSpec((None, t, LANES), lambda bi, p: (bi, 0, 0))
    return pl.pallas_call(
        functools.partial(_mla_kernel, tq=tq),
        grid=(b, npair),
        in_specs=[pair, shared, pair, shared, pair],
        out_specs=pair,
        out_shape=jax.ShapeDtypeStruct((b, t, MLA_WIDTH), BF16),
        name="mla_attention",
        compiler_params=_cparams(("parallel", "parallel")),
    )(qn, qr, kn, kr4, vv)


DIL_UNROLL = 8


def _dil_kernel(q_ref, k_ref, v_ref, o_ref, qf, kf, vf, qb, kb, vb, accb, mb, lb):
    t = q_ref.shape[0]
    blk = DIL_BLOCK
    lane = _lane_iota()
    head0 = lane < HEAD_DIM
    qf[...] = q_ref[...].astype(F32)
    kf[...] = k_ref[...].astype(F32)
    vf[...] = v_ref[...].astype(F32)

    for g, (_, rate) in enumerate(DIL_PAIRS):
        length = t // rate
        kb[g, 0:blk, :] = jnp.zeros((blk, LANES), BF16)
        vb[g, 0:blk, :] = jnp.zeros((blk, LANES), BF16)
        if rate == 1:
            qb[g, blk:, :] = q_ref[...]
            kb[g, blk:, :] = k_ref[...]
            vb[g, blk:, :] = v_ref[...]
            continue
        for res in range(rate):
            dst = slice(blk + res * length, blk + (res + 1) * length)
            src = pl.ds(res, length, stride=rate)
            qb[g, dst, :] = qf[src, :].astype(BF16)
            kb[g, dst, :] = kf[src, :].astype(BF16)
            vb[g, dst, :] = vf[src, :].astype(BF16)

    qi2 = lax.broadcasted_iota(jnp.int32, (blk, 2 * blk), 0)
    kj2 = lax.broadcasted_iota(jnp.int32, (blk, 2 * blk), 1)
    band = (kj2 >= qi2) & (kj2 <= qi2 + blk)
    bias_full = jnp.where(band, 0.0, NEG_INF)
    bias_first = jnp.where(band & (kj2 >= blk), 0.0, NEG_INF)
    qi1 = lax.broadcasted_iota(jnp.int32, (blk, blk), 0)
    kj1 = lax.broadcasted_iota(jnp.int32, (blk, blk), 1)
    bias_single = jnp.where(kj1 <= qi1, 0.0, NEG_INF)

    for g, (window, rate) in enumerate(DIL_PAIRS):
        assert window // rate == blk
        nb = t // rate // blk
        nblocks = t // blk

        def body(j, carry, g=g, nb=nb):
            base = pl.multiple_of(j * blk, blk)
            q = qb[g, pl.ds(base + blk, blk), :]
            zero = jnp.zeros_like(q)
            qs = jnp.concatenate([jnp.where(head0, q, zero), jnp.where(head0, zero, q)], axis=0)
            if nb == 1:
                keys = kb[g, pl.ds(base + blk, blk), :]
                vals = vb[g, pl.ds(base + blk, blk), :]
                bias = bias_single
            else:
                keys = kb[g, pl.ds(base, 2 * blk), :]
                vals = vb[g, pl.ds(base, 2 * blk), :]
                bias = jnp.where((j % nb) != 0, bias_full, bias_first)
            s = _dot_nt(qs, keys)
            ps, ms = [], []
            for h in range(2):
                sh = s[h * blk:(h + 1) * blk] + bias
                m = jnp.max(sh, axis=1, keepdims=True)
                ps.append(jnp.exp(sh - m).astype(BF16))
                ms.append(m)
            pv = _dot(jnp.concatenate(ps, axis=0),
                      jnp.concatenate([vals, jnp.ones_like(vals)], axis=1))
            rows = pl.ds(base, blk)
            accb[g, rows, :] = jnp.where(head0, pv[:blk, :LANES], pv[blk:, :LANES])
            mb[g, rows, :] = jnp.where(head0, ms[0], ms[1])
            lb[g, rows, :] = jnp.where(head0, pv[:blk, LANES:], pv[blk:, LANES:])
            return carry

        lax.fori_loop(0, nblocks, body, 0, unroll=DIL_UNROLL)

    rate_max = max(rate for _, rate in DIL_PAIRS)
    length = t // rate_max
    for res in range(rate_max):
        m_all, l_all, a_all = [], [], []
        for g, (_, rate) in enumerate(DIL_PAIRS):
            start = (res % rate) * (t // rate) + res // rate
            step = rate_max // rate
            rows = pl.ds(start, length) if step == 1 else pl.ds(start, length, stride=step)
            m_all.append(mb[g, rows, :])
            l_all.append(lb[g, rows, :])
            a_all.append(accb[g, rows, :])
        m_max = jnp.maximum(jnp.maximum(m_all[0], m_all[1]), m_all[2])
        ws = [jnp.exp(m - m_max) for m in m_all]
        num = ws[0] * a_all[0] + ws[1] * a_all[1] + ws[2] * a_all[2]
        den = ws[0] * l_all[0] + ws[1] * l_all[1] + ws[2] * l_all[2]
        o_ref[pl.ds(res, length, stride=rate_max), :] = num / den


def _dil_attention(main3):
    b, t, _ = main3.shape
    assert all(t % (rate * DIL_BLOCK) == 0 for _, rate in DIL_PAIRS)
    npair = DIL_HEADS // 2
    nbr = len(DIL_PAIRS)

    def col(start):
        return pl.BlockSpec((None, t, LANES), lambda bi, p: (bi, 0, start // LANES + p))

    return pl.pallas_call(
        _dil_kernel,
        grid=(b, npair),
        in_specs=[col(COL_DIL_QK), col(COL_DIL_QK + DIL_WIDTH), col(COL_DIL_V)],
        out_specs=pl.BlockSpec((None, t, LANES), lambda bi, p: (bi, 0, p)),
        out_shape=jax.ShapeDtypeStruct((b, t, DIL_WIDTH), F32),
        name="dilated_attention",
        scratch_shapes=[pltpu.VMEM((t, LANES), F32)] * 3
        + [pltpu.VMEM((nbr, t + DIL_BLOCK, LANES), BF16)] * 3
        + [pltpu.VMEM((nbr, t, LANES), F32)] * 3,
        compiler_params=_cparams(("parallel", "parallel")),
    )(main3, main3, main3)


def _merge_kernel(x_ref, gf_ref, gd_ref, gm_ref, of_ref, od_ref, om_ref,
                  wf_ref, wd_ref, wm_ref, wo_ref, o_ref):
    merged = (jax.nn.sigmoid(gf_ref[...].astype(F32)) * _dot(of_ref[...], wf_ref[...])
              + jax.nn.sigmoid(gd_ref[...].astype(F32)) * _dot(od_ref[...].astype(BF16), wd_ref[...])
              + jax.nn.sigmoid(gm_ref[...].astype(F32)) * _dot(om_ref[...], wm_ref[...]))
    o_ref[...] = x_ref[...] + _dot(merged.astype(BF16), wo_ref[...])


def _merge(x2, main2, o_fox, o_dil, o_mla, layer, wf, wd, wm, wo, tm=512):
    n = x2.shape[0]

    def tok(width, col=0):
        return pl.BlockSpec((tm, width), lambda i: (i, col))

    def full(w):
        return pl.BlockSpec((None,) + w.shape[1:], lambda i: (layer, 0, 0))

    return pl.pallas_call(
        _merge_kernel,
        grid=(n // tm,),
        in_specs=[tok(D_MODEL), tok(D_MODEL, 0), tok(D_MODEL, 1), tok(D_MODEL, 2),
                  tok(FOX_WIDTH), tok(DIL_WIDTH), tok(MLA_WIDTH),
                  full(wf), full(wd), full(wm), full(wo)],
        out_specs=tok(D_MODEL),
        out_shape=jax.ShapeDtypeStruct((n, D_MODEL), F32),
        name="merge_outproj",
        compiler_params=_cparams(("parallel",)),
    )(x2, main2, main2, main2, o_fox, o_dil, o_mla, wf, wd, wm, wo)


def _ffn_kernel(x_ref, g_ref, wg_ref, wu_ref, wd_ref, p_ref, pg_ref, wpg_ref, wpp_ref,
                o_ref, h_sc, acc_sc):
    c = pl.program_id(1)

    @pl.when(c == 0)
    def _():
        x = x_ref[...]
        h_sc[...] = _rms(x, g_ref[...]).astype(BF16)
        acc_sc[...] = x

    h = h_sc[...]
    gate = _dot(h, wg_ref[...])
    up = _dot(h, wu_ref[...])
    act = gate * jax.nn.sigmoid(gate) * up
    acc_sc[...] += _dot(act.astype(BF16), wd_ref[...])

    @pl.when(c == pl.num_programs(1) - 1)
    def _():
        x = acc_sc[...]
        hp = _rms(x, pg_ref[...]).astype(BF16)
        gate_p = jax.nn.sigmoid(_dot(hp, wpg_ref[...]))
        o_ref[...] = x + gate_p * _dot(p_ref[...].astype(BF16), wpp_ref[...])


def _ffn_ple(x2, gain, j, wg, wu, wd, layer, p3, ple_gain, wpg, wpp, tm=512):
    n = x2.shape[0]
    tok = pl.BlockSpec((tm, D_MODEL), lambda i, c: (i, 0))
    vec = pl.BlockSpec((1, D_MODEL), lambda i, c: (0, 0))
    return pl.pallas_call(
        _ffn_kernel,
        grid=(n // tm, D_FF // D_FF_EXPERT),
        in_specs=[tok, vec,
                  pl.BlockSpec((None, D_MODEL, D_FF_EXPERT), lambda i, c: (j, 0, c)),
                  pl.BlockSpec((None, D_MODEL, D_FF_EXPERT), lambda i, c: (j, 0, c)),
                  pl.BlockSpec((None, D_FF_EXPERT, D_MODEL), lambda i, c: (j, c, 0)),
                  pl.BlockSpec((None, tm, PLE_DIM), lambda i, c: (layer, i, 0)), vec,
                  pl.BlockSpec((None,) + wpg.shape[1:], lambda i, c: (layer, 0, 0)),
                  pl.BlockSpec((None,) + wpp.shape[1:], lambda i, c: (layer, 0, 0))],
        out_specs=tok,
        out_shape=jax.ShapeDtypeStruct((n, D_MODEL), F32),
        name="ffn_dense_ple",
        scratch_shapes=[pltpu.VMEM((tm, D_MODEL), BF16), pltpu.VMEM((tm, D_MODEL), F32)],
        compiler_params=_cparams(("parallel", "arbitrary")),
    )(x2, gain, wg, wu, wd, p3, ple_gain, wpg, wpp)


MOE_BLOCK = 2048
MOE_TILE = 256
MOE_TILES = -(-(2 * MOE_BLOCK + N_EXPERTS * (MOE_TILE - 1)) // MOE_TILE)
MOE_META_ROWS = 32
assert MOE_TILES <= MOE_META_ROWS
MOE_CHUNK = 1024
MOE_VMEM_LIMIT = 60 * 1024 * 1024


def _router_kernel(x_ref, g_ref, wh_ref, wm_ref, wl_ref, h_ref, rc_ref, rr_ref, meta_ref):
    h = _rms(x_ref[...], g_ref[...])
    h_ref[...] = h.astype(BF16)
    h_hi, h_mid, h_lo = _split3(h)
    w_hi, w_mid, w_lo = wh_ref[...], wm_ref[...], wl_ref[...]
    logits = (_dot(h_hi, w_hi) + (_dot(h_hi, w_mid) + _dot(h_mid, w_hi))
              + (_dot(h_hi, w_lo) + _dot(h_mid, w_mid) + _dot(h_lo, w_hi)))
    lane = _lane_iota()
    is_expert = lane < N_EXPERTS
    logits = jnp.where(is_expert, logits, NEG_INF)
    v1 = jnp.max(logits, axis=1, keepdims=True)
    i1 = jnp.min(jnp.where(logits == v1, lane, LANES), axis=1, keepdims=True)
    first = lane == i1
    rest = jnp.where(first, NEG_INF, logits)
    v2 = jnp.max(rest, axis=1, keepdims=True)
    i2 = jnp.min(jnp.where(rest == v2, lane, LANES), axis=1, keepdims=True)
    second = lane == i2
    e2 = jnp.exp(v2 - v1)
    w1 = 1.0 / (1.0 + e2)
    w2 = e2 / (1.0 + e2)

    sel = jnp.where(first, 1.0, jnp.where(second, 1.0, 0.0))
    row = lax.broadcasted_iota(jnp.int32, (CUMSUM_BLOCK, CUMSUM_BLOCK), 0)
    col = lax.broadcasted_iota(jnp.int32, (CUMSUM_BLOCK, CUMSUM_BLOCK), 1)
    tri = jnp.where(col < row, 1.0, 0.0).astype(BF16)
    carry = jnp.zeros((1, LANES), F32)
    ranks = []
    for blk in range(sel.shape[0] // CUMSUM_BLOCK):
        part = sel[blk * CUMSUM_BLOCK:(blk + 1) * CUMSUM_BLOCK]
        ranks.append(_dot(tri, part.astype(BF16)) + carry)
        carry = carry + jnp.sum(part, axis=0, keepdims=True)
    rank = jnp.concatenate(ranks, axis=0)
    padded = jnp.ceil(carry / MOE_TILE) * MOE_TILE
    er = lax.broadcasted_iota(jnp.int32, (LANES, LANES), 0)
    ec = lax.broadcasted_iota(jnp.int32, (LANES, LANES), 1)
    before = jnp.where(er < ec, 1.0, 0.0).astype(BF16)
    start = _dot(jnp.broadcast_to(padded, (8, LANES)).astype(BF16), before)[0:1]
    slot = start + rank
    dest1 = jnp.sum(jnp.where(first, slot, 0.0), axis=1, keepdims=True)
    dest2 = jnp.sum(jnp.where(second, slot, 0.0), axis=1, keepdims=True)
    routing = jnp.where(lane == 0, dest1, jnp.where(lane == 1, dest2,
                        jnp.where(lane == 2, w1, jnp.where(lane == 3, w2, 0.0))))
    rc_ref[...] = routing
    rrow = routing.T[0:8]
    rr_ref[...] = rrow

    end = start + padded
    tile_row = lax.broadcasted_iota(jnp.int32, (MOE_META_ROWS, LANES), 0).astype(F32) * MOE_TILE
    passed = jnp.sum(jnp.where(is_expert, jnp.where(tile_row >= end, 1.0, 0.0), 0.0),
                     axis=1, keepdims=True)
    last = jnp.max(jnp.where(is_expert, jnp.where(padded > 0, lane.astype(F32), 0.0), 0.0),
                   axis=1, keepdims=True)
    total = jnp.sum(jnp.where(is_expert, padded, 0.0), axis=1, keepdims=True)
    active = jnp.where(tile_row < total, 1.0, 0.0)
    nblock = rrow.shape[1]
    tok = lax.broadcasted_iota(jnp.int32, (MOE_META_ROWS, nblock), 1).astype(F32)
    tile_id = lax.broadcasted_iota(jnp.int32, (MOE_META_ROWS, nblock), 0).astype(F32)
    in_tile = jnp.where(jnp.floor(rrow[0:1] / MOE_TILE) == tile_id, 1.0,
                        jnp.where(jnp.floor(rrow[1:2] / MOE_TILE) == tile_id, 1.0, 0.0))
    first_tok = jnp.min(jnp.where(in_tile > 0, tok, float(nblock)), axis=1, keepdims=True)
    last_tok = jnp.max(jnp.where(in_tile > 0, tok, -1.0), axis=1, keepdims=True)
    chunk_lo = jnp.floor(first_tok / MOE_CHUNK)
    chunk_hi = jnp.floor(last_tok / MOE_CHUNK)
    meta = jnp.where(lane == 0, jnp.minimum(passed, last),
                     jnp.where(lane == 1, active,
                               jnp.where(lane == 2, chunk_lo, jnp.where(lane == 3, chunk_hi, 0.0))))
    meta_ref[...] = meta.astype(jnp.int32)


def _router(x2, gain, w_router):
    n = x2.shape[0]
    nblk = n // MOE_BLOCK
    w_pad = jnp.zeros((D_MODEL, LANES), F32).at[:, :N_EXPERTS].set(w_router)
    w_hi = w_pad.astype(BF16)
    r1 = w_pad - w_hi.astype(F32)
    w_mid = r1.astype(BF16)
    w_lo = (r1 - w_mid.astype(F32)).astype(BF16)
    wspec = pl.BlockSpec((D_MODEL, LANES), lambda i: (0, 0))
    return pl.pallas_call(
        _router_kernel,
        grid=(nblk,),
        in_specs=[pl.BlockSpec((MOE_BLOCK, D_MODEL), lambda i: (i, 0)),
                  pl.BlockSpec((1, D_MODEL), lambda i: (0, 0)), wspec, wspec, wspec],
        out_specs=[pl.BlockSpec((MOE_BLOCK, D_MODEL), lambda i: (i, 0)),
                   pl.BlockSpec((MOE_BLOCK, LANES), lambda i: (i, 0)),
                   pl.BlockSpec((None, 8, MOE_BLOCK), lambda i: (i, 0, 0)),
                   pl.BlockSpec((None, MOE_META_ROWS, LANES), lambda i: (i, 0, 0))],
        out_shape=[jax.ShapeDtypeStruct((n, D_MODEL), BF16),
                   jax.ShapeDtypeStruct((n, LANES), F32),
                   jax.ShapeDtypeStruct((nblk, 8, MOE_BLOCK), F32),
                   jax.ShapeDtypeStruct((nblk, MOE_META_ROWS, LANES), jnp.int32)],
        name="router",
        compiler_params=_cparams(("parallel",)),
    )(x2, gain, w_hi, w_mid, w_lo)


def _moe_kernel(te_ref, ta_ref, lo_ref, hi_ref, h_ref, rrow_ref, rcol_ref, wg_ref, wu_ref, wd_ref,
                o_ref, xg_sc, w_sc):
    b = pl.program_id(0)
    r = pl.program_id(1)

    @pl.when(r == 0)
    def _():
        o_ref[...] = jnp.zeros(o_ref.shape, F32)

    @pl.when(ta_ref[b, r] == 1)
    def _():
        base = (r * MOE_TILE).astype(F32)
        chunk_lo = lo_ref[b, r]
        chunk_hi = hi_ref[b, r]
        slot_col = lax.broadcasted_iota(jnp.int32, (MOE_TILE, 1), 0).astype(F32) + base
        slot_row = lax.broadcasted_iota(jnp.int32, (1, MOE_TILE), 1).astype(F32) + base
        xg_sc[...] = jnp.zeros(xg_sc.shape, F32)
        w_sc[...] = jnp.zeros(w_sc.shape, F32)

        for c in range(MOE_BLOCK // MOE_CHUNK):
            toks = slice(c * MOE_CHUNK, (c + 1) * MOE_CHUNK)

            @pl.when((chunk_lo <= c) & (c <= chunk_hi))
            def _(toks=toks):
                hit1 = rrow_ref[0:1, toks] == slot_col
                hit2 = rrow_ref[1:2, toks] == slot_col
                onehot = jnp.where(hit1, 1.0, jnp.where(hit2, 1.0, 0.0)).astype(BF16)
                xg_sc[...] += _dot(onehot, h_ref[toks, :])
                w_sc[...] += jnp.sum(
                    jnp.where(hit1, rrow_ref[2:3, toks], jnp.where(hit2, rrow_ref[3:4, toks], 0.0)),
                    axis=1, keepdims=True)

        xg = xg_sc[...].astype(BF16)
        gate = _dot(xg, wg_ref[...])
        up = _dot(xg, wu_ref[...])
        act = gate * jax.nn.sigmoid(gate) * up
        y = (_dot(act.astype(BF16), wd_ref[...]) * w_sc[...]).astype(BF16)

        for c in range(MOE_BLOCK // MOE_CHUNK):
            toks = slice(c * MOE_CHUNK, (c + 1) * MOE_CHUNK)

            @pl.when((chunk_lo <= c) & (c <= chunk_hi))
            def _(toks=toks):
                back = jnp.where(rcol_ref[toks, 0:1] == slot_row, 1.0,
                                 jnp.where(rcol_ref[toks, 1:2] == slot_row, 1.0, 0.0)).astype(BF16)
                o_ref[toks, :] += _dot(back, y)


def _moe(h2, rrow, rcol, meta, j, wg, wu, wd):
    n = h2.shape[0]
    nblk = n // MOE_BLOCK
    tile_tables = [meta[:, :MOE_TILES, k] for k in range(4)]

    def blocked(shape, index):
        return pl.BlockSpec(shape, lambda b, r, te, ta, lo, hi: index(b))

    def wspec(shape):
        return pl.BlockSpec((None,) + shape,
                            lambda b, r, te, ta, lo, hi: (j * N_EXPERTS + te[b, r], 0, 0))

    grid_spec = pltpu.PrefetchScalarGridSpec(
        num_scalar_prefetch=len(tile_tables),
        grid=(nblk, MOE_TILES),
        in_specs=[blocked((MOE_BLOCK, D_MODEL), lambda b: (b, 0)),
                  blocked((None, 8, MOE_BLOCK), lambda b: (b, 0, 0)),
                  blocked((MOE_BLOCK, LANES), lambda b: (b, 0)),
                  wspec((D_MODEL, D_FF_EXPERT)), wspec((D_MODEL, D_FF_EXPERT)),
                  wspec((D_FF_EXPERT, D_MODEL))],
        out_specs=blocked((MOE_BLOCK, D_MODEL), lambda b: (b, 0)),
        scratch_shapes=[pltpu.VMEM((MOE_TILE, D_MODEL), F32), pltpu.VMEM((MOE_TILE, 1), F32)],
    )
    return pl.pallas_call(
        _moe_kernel,
        grid_spec=grid_spec,
        out_shape=jax.ShapeDtypeStruct((n, D_MODEL), F32),
        name="moe_experts",
        compiler_params=pltpu.CompilerParams(dimension_semantics=("parallel", "arbitrary"),
                                             vmem_limit_bytes=MOE_VMEM_LIMIT),
    )(*tile_tables, h2, rrow, rcol, wg, wu, wd)


def _moe_layer(x2, gain, w_router, j, wg, wu, wd):
    h2, rcol, rrow, meta = _router(x2, gain, w_router)
    return _moe(h2, rrow, rcol, meta, j, wg, wu, wd)


def _ple_kernel(x_ref, d_ref, p_ref, g_ref, wg_ref, wp_ref, fg_ref, o_ref, *, final):
    x = x_ref[...] + d_ref[...]
    h = _rms(x, g_ref[...]).astype(BF16)
    gate = jax.nn.sigmoid(_dot(h, wg_ref[...]))
    y = x + gate * _dot(p_ref[...].astype(BF16), wp_ref[...])
    if final:
        y = _rms(y, fg_ref[...])
    o_ref[...] = y


def _ple(x2, delta, layer, p3, gain, wg, wp, final_gain, final, tm=512):
    n = x2.shape[0]
    vec = pl.BlockSpec((1, D_MODEL), lambda i: (0, 0))
    tok = pl.BlockSpec((tm, D_MODEL), lambda i: (i, 0))
    return pl.pallas_call(
        functools.partial(_ple_kernel, final=final),
        grid=(n // tm,),
        in_specs=[tok, tok,
                  pl.BlockSpec((None, tm, PLE_DIM), lambda i: (layer, i, 0)), vec,
                  pl.BlockSpec((None,) + wg.shape[1:], lambda i: (layer, 0, 0)),
                  pl.BlockSpec((None,) + wp.shape[1:], lambda i: (layer, 0, 0)), vec],
        out_specs=pl.BlockSpec((tm, D_MODEL), lambda i: (i, 0)),
        out_shape=jax.ShapeDtypeStruct((n, D_MODEL), F32),
        name="ple_final" if final else "ple",
        compiler_params=_cparams(("parallel",)),
    )(x2, delta, p3, gain, wg, wp, final_gain)


def _cast_kernel(x_ref, o_ref):
    o_ref[...] = x_ref[...].astype(BF16)


def _to_bf16(a):
    r, c = a.shape[-2:]
    a3 = a.reshape((-1, r, c))
    spec = pl.BlockSpec((None, r, c), lambda i: (i, 0, 0))
    out = pl.pallas_call(
        _cast_kernel,
        grid=(a3.shape[0],),
        in_specs=[spec],
        out_specs=spec,
        out_shape=jax.ShapeDtypeStruct(a3.shape, BF16),
        name="cast_bf16",
        compiler_params=_cparams(("parallel",)),
    )(a3)
    return out.reshape(a.shape)


IN_SIZES = (FOX_WIDTH, FOX_WIDTH, FOX_WIDTH, FOX_HEADS, DIL_WIDTH, DIL_WIDTH, DIL_WIDTH,
            MLA_Q_RANK, MLA_KV_RANK, MLA_ROPE_DIM, GATE_WIDTH)
IN_WIDTH = sum(IN_SIZES)
IN_PREP_COLS = 256


def _inproj_weight_kernel(w_ref, main_ref, small_ref):
    offs = [0]
    for s in IN_SIZES:
        offs.append(offs[-1] + s)
    fq, fk, fv, fz, dq, dk, dv, cq, ckv, kr, gates = [(offs[i], offs[i + 1])
                                                      for i in range(len(IN_SIZES))]
    scale = HEAD_DIM ** -0.5
    row = 0
    for (lo, hi), mult in ((gates, 1.0), (dq, scale), (dk, 1.0), (fq, scale), (fk, 1.0), (fv, 1.0),
                           (dv, 1.0), (cq, 1.0), (ckv, 1.0)):
        main_ref[row:row + hi - lo, :] = (w_ref[lo:hi, :] * mult).astype(BF16)
        row += hi - lo
    small_ref[...] = jnp.zeros(small_ref.shape, BF16)
    small_ref[0:FOX_HEADS, :] = w_ref[fz[0]:fz[1], :].astype(BF16)
    for h in range(MLA_HEADS):
        r0 = LANES + h * MLA_ROPE_DIM
        small_ref[r0:r0 + MLA_ROPE_DIM, :] = w_ref[kr[0]:kr[1], :].astype(BF16)


def _prep_in_weights(w_in):
    depth = w_in.shape[0]
    cb = IN_PREP_COLS
    w_t = jnp.swapaxes(w_in, 1, 2)

    def blk(rows):
        return pl.BlockSpec((None, rows, cb), lambda l, i: (l, 0, i))

    return pl.pallas_call(
        _inproj_weight_kernel,
        grid=(depth, D_MODEL // cb),
        in_specs=[blk(IN_WIDTH)],
        out_specs=[blk(MAIN_WIDTH), blk(SMALL_WIDTH)],
        out_shape=[jax.ShapeDtypeStruct((depth, MAIN_WIDTH, D_MODEL), BF16),
                   jax.ShapeDtypeStruct((depth, SMALL_WIDTH, D_MODEL), BF16)],
        name="inproj_weights",
        compiler_params=_cparams(("parallel", "parallel")),
    )(w_t)


def _prep_mla_weights(w_uq, w_ukv):
    depth = w_uq.shape[0]
    uq = w_uq.reshape(depth, MLA_Q_RANK, MLA_HEADS, MLA_NOPE_DIM + MLA_ROPE_DIM)
    wq = jnp.concatenate([uq[..., :MLA_NOPE_DIM].reshape(depth, MLA_Q_RANK, -1),
                          uq[..., MLA_NOPE_DIM:].reshape(depth, MLA_Q_RANK, -1)], axis=-1)
    ukv = w_ukv.reshape(depth, MLA_KV_RANK, MLA_HEADS, MLA_NOPE_DIM + MLA_V_DIM)
    wkv = jnp.concatenate([ukv[..., :MLA_NOPE_DIM].reshape(depth, MLA_KV_RANK, -1),
                           ukv[..., MLA_NOPE_DIM:].reshape(depth, MLA_KV_RANK, -1)], axis=-1)
    return wq.astype(BF16), wkv.astype(BF16)


def kernel(x, p, positions, mix_norm, w_in, b_forget, mla_q_norm, mla_kv_norm, w_uq, w_ukv,
           w_br_fox, w_br_dil, w_br_mla, w_out, ffn_norm, w_ffn_gate, w_ffn_up, w_ffn_down,
           w_router, w_exp_gate, w_exp_up, w_exp_down, ple_norm, w_ple_gate, w_ple_proj,
           final_norm):
    b, t, _ = x.shape
    n = b * t
    depth = w_in.shape[0]
    assert depth % 2 == 0

    w_main, w_small = _prep_in_weights(w_in)
    wq_all, wkv_all = _prep_mla_weights(w_uq, w_ukv)
    (w_br_fox, w_br_dil, w_br_mla, w_out, w_ffn_gate, w_ffn_up, w_ffn_down, w_exp_gate, w_exp_up,
     w_exp_down, w_ple_gate, w_ple_proj) = [
        _to_bf16(w) for w in (w_br_fox, w_br_dil, w_br_mla, w_out, w_ffn_gate, w_ffn_up, w_ffn_down,
                              w_exp_gate, w_exp_up, w_exp_down, w_ple_gate, w_ple_proj)]
    w_exp_gate, w_exp_up, w_exp_down = [w.reshape((-1,) + w.shape[2:])
                                        for w in (w_exp_gate, w_exp_up, w_exp_down)]
    p3 = p.reshape(depth, n, PLE_DIM)
    bias_all = jnp.concatenate(
        [b_forget.astype(F32), jnp.zeros((depth, LANES - FOX_HEADS), F32)], axis=-1)
    tabs = [tab.reshape(n, LANES) for tab in _rope_tables(positions)]
    final_gain = final_norm.reshape(1, D_MODEL)

    x2 = x.reshape(n, D_MODEL)
    for i in range(depth):
        main2, fz, qn, qr, kn, kr4, vv = _inproj(
            x2, mix_norm[i].reshape(1, D_MODEL), i, w_main, w_small, tabs,
            mla_q_norm[i].reshape(1, -1), mla_kv_norm[i].reshape(1, -1), wq_all[i], wkv_all[i])
        main3 = main2.reshape(b, t, COL_MLA)

        dcols = _decay(fz.reshape(b, t, LANES), bias_all[i].reshape(1, LANES))
        o_fox = _fox_attention(main3, dcols)
        o_dil = _dil_attention(main3)
        o_mla = _mla_attention(*[a.reshape(b, t, -1) for a in (qn, qr, kn, kr4, vv)])

        x2 = _merge(x2, main2, o_fox.reshape(n, -1), o_dil.reshape(n, -1), o_mla.reshape(n, -1),
                    i, w_br_fox, w_br_dil, w_br_mla, w_out)

        j = i // 2
        gain = ffn_norm[i].reshape(1, D_MODEL)
        ple_args = (i, p3, ple_norm[i].reshape(1, D_MODEL), w_ple_gate, w_ple_proj)
        if i % 2 == 0:
            x2 = _ffn_ple(x2, gain, j, w_ffn_gate, w_ffn_up, w_ffn_down, *ple_args)
        else:
            delta = _moe_layer(x2, gain, w_router[j], j, w_exp_gate, w_exp_up, w_exp_down)
            x2 = _ple(x2, delta, *ple_args, final_gain, final=(i == depth - 1))
    return x2.reshape(b, t, D_MODEL)
```

```python
import functools

import jax
import jax.numpy as jnp
from jax import lax
from jax.experimental import pallas as pl
from jax.experimental.pallas import tpu as pltpu

F32 = jnp.float32
BF16 = jnp.bfloat16

D_MODEL = 1024
HEAD_DIM = 64
ROPE_THETA = 10000.0
NORM_EPS = 1e-6
FOX_HEADS = 6
FOX_WIDTH = FOX_HEADS * HEAD_DIM
DIL_HEADS = 6
DIL_WIDTH = DIL_HEADS * HEAD_DIM
DIL_PAIRS = ((128, 1), (512, 4), (2048, 16))
DIL_BLOCK = 128
MLA_HEADS = 4
MLA_Q_RANK = 256
MLA_KV_RANK = 256
MLA_NOPE_DIM = 64
MLA_ROPE_DIM = 32
MLA_V_DIM = 64
MLA_WIDTH = MLA_HEADS * MLA_V_DIM
N_BRANCHES = 3
D_FF = 2816
N_EXPERTS = 8
D_FF_EXPERT = 1408
PLE_DIM = 256

LANES = 128
GATE_WIDTH = N_BRANCHES * D_MODEL
COL_GATES = 0
COL_DIL_QK = GATE_WIDTH
COL_FOX = COL_DIL_QK + 2 * DIL_WIDTH
COL_DIL_V = COL_FOX + 3 * FOX_WIDTH
COL_MLA = COL_DIL_V + DIL_WIDTH
MAIN_WIDTH = COL_MLA + MLA_Q_RANK + MLA_KV_RANK
MXU_TILE = 256
assert all(c % MXU_TILE == 0 for c in (COL_DIL_QK, COL_FOX, COL_MLA, MAIN_WIDTH))
SMALL_WIDTH = 2 * LANES
VMEM_LIMIT = 56 * 1024 * 1024

NEG_INF = float("-inf")


def _cparams(sem):
    return pltpu.CompilerParams(dimension_semantics=sem, vmem_limit_bytes=VMEM_LIMIT)


def _rms(x, gain):
    return x * lax.rsqrt(jnp.mean(x * x, axis=-1, keepdims=True) + NORM_EPS) * gain


def _dot(a, b):
    return jnp.dot(a, b, preferred_element_type=F32)


def _dot_nt(a, b):
    return lax.dot_general(a, b, (((1,), (1,)), ((), ())), preferred_element_type=F32)


def _lane_iota():
    return lax.broadcasted_iota(jnp.int32, (1, LANES), 1)


def _rope_tab_kernel(pos_ref, fd_ref, fm_ref, cd_ref, sd_ref, cm_ref, sm_ref):
    pos = pos_ref[...].astype(F32)
    lane = _lane_iota()
    ang = pos * fd_ref[...]
    cd_ref[...] = jnp.cos(ang)
    sd_ref[...] = jnp.sin(ang) * jnp.where((lane % HEAD_DIM) < HEAD_DIM // 2, -1.0, 1.0)
    ang = pos * fm_ref[...]
    cm_ref[...] = jnp.cos(ang)
    sm_ref[...] = jnp.sin(ang) * jnp.where((lane % MLA_ROPE_DIM) < MLA_ROPE_DIM // 2, -1.0, 1.0)


def _rope_tables(positions):
    b, t = positions.shape
    half_d = HEAD_DIM // 2
    half_m = MLA_ROPE_DIM // 2
    inv_d = ROPE_THETA ** (-jnp.arange(half_d, dtype=F32) / half_d)
    inv_m = ROPE_THETA ** (-jnp.arange(half_m, dtype=F32) / half_m)
    fd = jnp.tile(inv_d, LANES // half_d)[None, :]
    fm = jnp.tile(inv_m, LANES // half_m)[None, :]
    tab = jax.ShapeDtypeStruct((b, t, LANES), F32)
    tab_spec = pl.BlockSpec((None, t, LANES), lambda i: (i, 0, 0))
    vec_spec = pl.BlockSpec((1, LANES), lambda i: (0, 0))
    return pl.pallas_call(
        _rope_tab_kernel,
        grid=(b,),
        in_specs=[pl.BlockSpec((None, t, 1), lambda i: (i, 0, 0)), vec_spec, vec_spec],
        out_specs=[tab_spec] * 4,
        out_shape=[tab] * 4,
        name="rope_tables",
        compiler_params=_cparams(("parallel",)),
    )(positions.reshape(b, t, 1), fd, fm)


def _swap_halves(x, width):
    lane = _lane_iota()
    half = width // 2
    return jnp.where((lane % width) < half,
                     pltpu.roll(x, LANES - half, axis=1),
                     pltpu.roll(x, half, axis=1))


def _rope(x, cos, sin_signed, width):
    return x * cos + _swap_halves(x, width) * sin_signed


def _col_chunks(width, step=512):
    return [(c, min(step, width - c)) for c in range(0, width, step)]


MLA_NOPE_WIDTH = MLA_HEADS * MLA_NOPE_DIM
MLA_SCALE = (MLA_NOPE_DIM + MLA_ROPE_DIM) ** -0.5


def _inproj_kernel(x_ref, g_ref, w_ref, ws_ref, cd_ref, sd_ref, cm_ref, sm_ref,
                   gq_ref, gkv_ref, wq_ref, wkv_ref,
                   o_ref, fz_ref, qn_ref, qr_ref, kn_ref, kr_ref, v_ref):
    h = _rms(x_ref[...], g_ref[...]).astype(BF16)
    for lo, hi in ((0, COL_DIL_QK), (COL_FOX, COL_MLA)):
        for c, n in _col_chunks(hi - lo):
            o_ref[:, lo + c:lo + c + n] = _dot_nt(h, w_ref[lo + c:lo + c + n, :]).astype(BF16)
    cos_d = cd_ref[...]
    sin_d = sd_ref[...]
    for c0 in range(COL_DIL_QK, COL_FOX, MXU_TILE):
        y = _dot_nt(h, w_ref[c0:c0 + MXU_TILE, :])
        for c in range(0, MXU_TILE, LANES):
            o_ref[:, c0 + c:c0 + c + LANES] = _rope(y[:, c:c + LANES], cos_d, sin_d,
                                                    HEAD_DIM).astype(BF16)

    cos_m = cm_ref[...]
    sin_m = sm_ref[...]
    lat = _dot_nt(h, w_ref[COL_MLA:MAIN_WIDTH, :])
    hq = _rms(lat[:, :MLA_Q_RANK], gq_ref[...]).astype(BF16)
    q = _dot(hq, wq_ref[...])
    qn_ref[...] = (q[:, :MLA_NOPE_WIDTH] * MLA_SCALE).astype(BF16)
    qr_ref[...] = (_rope(q[:, MLA_NOPE_WIDTH:], cos_m, sin_m, MLA_ROPE_DIM) * MLA_SCALE).astype(BF16)
    hkv = _rms(lat[:, MLA_Q_RANK:], gkv_ref[...]).astype(BF16)
    kv = _dot(hkv, wkv_ref[...])
    kn_ref[...] = kv[:, :MLA_NOPE_WIDTH].astype(BF16)
    v_ref[...] = kv[:, MLA_NOPE_WIDTH:].astype(BF16)

    small = _dot_nt(h, ws_ref[...])
    fz_ref[...] = small[:, :LANES]
    kr_ref[...] = _rope(small[:, LANES:], cos_m, sin_m, MLA_ROPE_DIM).astype(BF16)


def _inproj(x2, gain, layer, w_main_t, w_small_t, tabs, gq, gkv, wq, wkv, tm=512):
    n = x2.shape[0]

    def tok(width):
        return pl.BlockSpec((tm, width), lambda i: (i, 0))

    def full(a):
        return pl.BlockSpec(a.shape, lambda i: (0, 0))

    def of_layer(a):
        return pl.BlockSpec((None,) + a.shape[1:], lambda i: (layer, 0, 0))

    widths = (COL_MLA, LANES, MLA_NOPE_WIDTH, LANES, MLA_NOPE_WIDTH, LANES, MLA_WIDTH)
    dtypes = (BF16, F32, BF16, BF16, BF16, BF16, BF16)
    return pl.pallas_call(
        _inproj_kernel,
        grid=(n // tm,),
        in_specs=[tok(D_MODEL), full(gain), of_layer(w_main_t), of_layer(w_small_t)]
        + [tok(LANES)] * 4 + [full(gq), full(gkv), full(wq), full(wkv)],
        out_specs=[tok(w) for w in widths],
        out_shape=[jax.ShapeDtypeStruct((n, w), d) for w, d in zip(widths, dtypes)],
        name="inproj",
        compiler_params=_cparams(("parallel",)),
    )(x2, gain, w_main_t, w_small_t, *tabs, gq, gkv, wq, wkv)


CUMSUM_BLOCK = 256
DECAY_PART_STRIDE = 8
DECAY_PARTS = 3


def _split3(a):
    hi = a.astype(BF16)
    r1 = a - hi.astype(F32)
    mid = r1.astype(BF16)
    lo = (r1 - mid.astype(F32)).astype(BF16)
    return hi, mid, lo


def _decay_kernel(z_ref, b_ref, o_ref):
    t = z_ref.shape[0]
    row = lax.broadcasted_iota(jnp.int32, (CUMSUM_BLOCK, CUMSUM_BLOCK), 0)
    col = lax.broadcasted_iota(jnp.int32, (CUMSUM_BLOCK, CUMSUM_BLOCK), 1)
    tri = jnp.where(col <= row, 1.0, 0.0).astype(BF16)
    is_head = _lane_iota() < FOX_HEADS
    carry = jnp.zeros((1, LANES), F32)
    for blk in range(t // CUMSUM_BLOCK):
        sl = slice(blk * CUMSUM_BLOCK, (blk + 1) * CUMSUM_BLOCK)
        z = z_ref[sl, :] + b_ref[...]
        logf = -(jnp.maximum(-z, 0.0) + jnp.log1p(jnp.exp(-jnp.abs(z))))
        hi, mid, lo = _split3(logf)
        cs = _dot(tri, hi) + _dot(tri, mid) + _dot(tri, lo) + carry
        carry = cs[CUMSUM_BLOCK - 1:CUMSUM_BLOCK, :]
        parts = _split3(jnp.where(is_head, cs, 0.0))
        packed = parts[0].astype(F32)
        for j in range(1, DECAY_PARTS):
            packed = packed + pltpu.roll(parts[j].astype(F32), j * DECAY_PART_STRIDE, axis=1)
        o_ref[sl, :] = packed.astype(BF16)


def _decay(small3, bias):
    b, t, _ = small3.shape
    return pl.pallas_call(
        _decay_kernel,
        grid=(b,),
        in_specs=[pl.BlockSpec((None, t, LANES), lambda i: (i, 0, 0)),
                  pl.BlockSpec((1, LANES), lambda i: (0, 0))],
        out_specs=pl.BlockSpec((None, t, LANES), lambda i: (i, 0, 0)),
        out_shape=jax.ShapeDtypeStruct((b, t, LANES), BF16),
        name="decay",
        compiler_params=_cparams(("parallel",)),
    )(small3, bias)


def _pair_softmax(s, m_prev, ok_cols):
    nc = s.shape[1] // LANES
    cols = [s[:, c * LANES:(c + 1) * LANES] for c in range(nc)]
    if ok_cols is not None:
        cols = [jnp.where(ok, col, NEG_INF) for ok, col in zip(ok_cols, cols)]
    cmax = cols[0]
    for col in cols[1:]:
        cmax = jnp.maximum(cmax, col)
    m_new = jnp.maximum(m_prev, jnp.max(cmax, axis=1, keepdims=True))
    alpha = jnp.exp(m_prev - m_new)
    p = jnp.concatenate([jnp.exp(col - m_new).astype(BF16) for col in cols], axis=1)
    return m_new, alpha, p


def _causal_pair_attention(qs, keys_rows, vals_rows, qi, o_ref, tq):
    lane = _lane_iota()
    head0 = lane < HEAD_DIM

    def scores(j):
        return _dot_nt(qs, keys_rows(j * tq, (j + 1) * tq))

    def weighted_values(p, j):
        v = vals_rows(j * tq, (j + 1) * tq)
        one = jnp.ones_like(v)
        pv0 = _dot(p[:tq], jnp.where(head0, v, one))
        pv1 = _dot(p[tq:], jnp.where(head0, one, v))
        return jnp.concatenate([pv0, pv1], axis=0)

    s = scores(0)
    m_prev = jnp.full((2 * tq, LANES), NEG_INF, F32)
    acc = jnp.zeros((2 * tq, LANES), F32)
    for j in range(qi):
        s_next = scores(j + 1)
        m_prev, alpha, p = _pair_softmax(s, m_prev, None)
        acc = alpha * acc + weighted_values(p, j)
        s = s_next

    r_i = lax.broadcasted_iota(jnp.int32, (tq, LANES), 0)
    c_i = lax.broadcasted_iota(jnp.int32, (tq, LANES), 1)
    ok_cols = []
    for c in range(tq // LANES):
        ok = c_i + c * LANES <= r_i
        ok_cols.append(jnp.concatenate([ok, ok], axis=0))
    _, alpha, p = _pair_softmax(s, m_prev, ok_cols)
    acc = alpha * acc + weighted_values(p, qi)
    out = acc / pltpu.roll(acc, HEAD_DIM, axis=1)
    o_ref[...] = jnp.where(head0, out[:tq], out[tq:]).astype(o_ref.dtype)


def _fox_kernel(q_ref, k_ref, v_ref, d_ref, o_ref, *, tq):
    p = pl.program_id(1)
    lane = _lane_iota()

    def keys_at(lo, hi):
        return jnp.concatenate([k_ref[lo:hi, :], d_ref[lo:hi, :]], axis=1)

    def vals_at(lo, hi):
        return v_ref[lo:hi, :]

    for qi in range(q_ref.shape[0] // tq):
        q = q_ref[qi * tq:(qi + 1) * tq, :]
        zero = jnp.zeros_like(q)
        halves = []
        for h in range(2):
            head = 2 * p + h
            pick = (lane % DECAY_PART_STRIDE == head) & (lane < DECAY_PARTS * DECAY_PART_STRIDE)
            neg = jnp.broadcast_to(jnp.where(pick, -1.0, 0.0).astype(BF16), q.shape)
            q_h = jnp.where((lane // HEAD_DIM) == h, q, zero)
            halves.append(jnp.concatenate([q_h, neg], axis=1))
        qs = jnp.concatenate(halves, axis=0)
        _causal_pair_attention(qs, keys_at, vals_at, qi, o_ref.at[qi * tq:(qi + 1) * tq, :], tq)


def _fox_attention(main3, dcols, tq=512):
    b, t, _ = main3.shape
    cb = COL_FOX // LANES
    npair = FOX_HEADS // 2

    def seq(col):
        return pl.BlockSpec((None, t, LANES), lambda bi, p: (bi, 0, col(p)))

    return pl.pallas_call(
        functools.partial(_fox_kernel, tq=tq),
        grid=(b, npair),
        in_specs=[
            seq(lambda p: cb + p),
            seq(lambda p: cb + npair + p),
            seq(lambda p: cb + 2 * npair + p),
            seq(lambda p: 0),
        ],
        out_specs=pl.BlockSpec((None, t, LANES), lambda bi, p: (bi, 0, p)),
        out_shape=jax.ShapeDtypeStruct((b, t, FOX_WIDTH), BF16),
        name="fox_attention",
        compiler_params=_cparams(("parallel", "parallel")),
    )(main3, main3, main3, dcols)


def _mla_kernel(qn_ref, qr_ref, kn_ref, kr_ref, v_ref, o_ref, *, tq):
    p = pl.program_id(1)
    lane = _lane_iota()

    def keys_at(lo, hi):
        return jnp.concatenate([kn_ref[lo:hi, :], kr_ref[lo:hi, :]], axis=1)

    def vals_at(lo, hi):
        return v_ref[lo:hi, :]

    for qi in range(qn_ref.shape[0] // tq):
        qn = qn_ref[qi * tq:(qi + 1) * tq, :]
        qr = qr_ref[qi * tq:(qi + 1) * tq, :]
        zero = jnp.zeros_like(qn)
        halves = []
        for h in range(2):
            nope_h = jnp.where((lane // MLA_NOPE_DIM) == h, qn, zero)
            rope_h = jnp.where((lane // MLA_ROPE_DIM) == 2 * p + h, qr, zero)
            halves.append(jnp.concatenate([nope_h, rope_h], axis=1))
        qs = jnp.concatenate(halves, axis=0)
        _causal_pair_attention(qs, keys_at, vals_at, qi, o_ref.at[qi * tq:(qi + 1) * tq, :], tq)


def _mla_attention(qn, qr, kn, kr4, vv, tq=512):
    b, t, _ = qn.shape
    npair = MLA_HEADS // 2
    pair = pl.BlockSpec((None, t, LANES), lambda bi, p: (bi, 0, p))
    shared = pl.BlockSpec((None, t, LANES), lambda bi, p: (bi, 0, 0))
    return pl.pallas_call(
        functools.partial(_mla_kernel, tq=tq),
        grid=(b, npair),
        in_specs=[pair, shared, pair, shared, pair],
        out_specs=pair,
        out_shape=jax.ShapeDtypeStruct((b, t, MLA_WIDTH), BF16),
        name="mla_attention",
        compiler_params=_cparams(("parallel", "parallel")),
    )(qn, qr, kn, kr4, vv)


DIL_UNROLL = 8


def _dil_kernel(q_ref, k_ref, v_ref, o_ref, qf, kf, vf, qb, kb, vb, accb, mb, lb):
    t = q_ref.shape[0]
    blk = DIL_BLOCK
    lane = _lane_iota()
    head0 = lane < HEAD_DIM
    qf[...] = q_ref[...].astype(F32)
    kf[...] = k_ref[...].astype(F32)
    vf[...] = v_ref[...].astype(F32)

    for g, (_, rate) in enumerate(DIL_PAIRS):
        length = t // rate
        kb[g, 0:blk, :] = jnp.zeros((blk, LANES), BF16)
        vb[g, 0:blk, :] = jnp.zeros((blk, LANES), BF16)
        if rate == 1:
            qb[g, blk:, :] = q_ref[...]
            kb[g, blk:, :] = k_ref[...]
            vb[g, blk:, :] = v_ref[...]
            continue
        for res in range(rate):
            dst = slice(blk + res * length, blk + (res + 1) * length)
            src = pl.ds(res, length, stride=rate)
            qb[g, dst, :] = qf[src, :].astype(BF16)
            kb[g, dst, :] = kf[src, :].astype(BF16)
            vb[g, dst, :] = vf[src, :].astype(BF16)

    qi2 = lax.broadcasted_iota(jnp.int32, (blk, 2 * blk), 0)
    kj2 = lax.broadcasted_iota(jnp.int32, (blk, 2 * blk), 1)
    band = (kj2 >= qi2) & (kj2 <= qi2 + blk)
    bias_full = jnp.where(band, 0.0, NEG_INF)
    bias_first = jnp.where(band & (kj2 >= blk), 0.0, NEG_INF)

    for g, (window, rate) in enumerate(DIL_PAIRS):
        assert window // rate == blk
        nb = t // rate // blk
        nblocks = t // blk

        def body(j, carry, g=g, nb=nb):
            base = pl.multiple_of(j * blk, blk)
            q = qb[g, pl.ds(base + blk, blk), :]
            zero = jnp.zeros_like(q)
            qs = jnp.concatenate([jnp.where(head0, q, zero), jnp.where(head0, zero, q)], axis=0)
            keys = kb[g, pl.ds(base, 2 * blk), :]
            vals = vb[g, pl.ds(base, 2 * blk), :]
            bias = bias_first if nb == 1 else jnp.where((j % nb) != 0, bias_full, bias_first)
            s = _dot_nt(qs, keys)
            ps, ms = [], []
            for h in range(2):
                sh = s[h * blk:(h + 1) * blk] + bias
                m = jnp.max(sh, axis=1, keepdims=True)
                ps.append(jnp.exp(sh - m).astype(BF16))
                ms.append(m)
            pv = _dot(jnp.concatenate(ps, axis=0),
                      jnp.concatenate([vals, jnp.ones_like(vals)], axis=1))
            rows = pl.ds(base, blk)
            accb[g, rows, :] = jnp.where(head0, pv[:blk, :LANES], pv[blk:, :LANES])
            mb[g, rows, :] = jnp.where(head0, ms[0], ms[1])
            lb[g, rows, :] = jnp.where(head0, pv[:blk, LANES:], pv[blk:, LANES:])
            return carry

        lax.fori_loop(0, nblocks, body, 0, unroll=DIL_UNROLL)

    rate_max = max(rate for _, rate in DIL_PAIRS)
    length = t // rate_max
    for res in range(rate_max):
        m_all, l_all, a_all = [], [], []
        for g, (_, rate) in enumerate(DIL_PAIRS):
            start = (res % rate) * (t // rate) + res // rate
            step = rate_max // rate
            rows = pl.ds(start, length) if step == 1 else pl.ds(start, length, stride=step)
            m_all.append(mb[g, rows, :])
            l_all.append(lb[g, rows, :])
            a_all.append(accb[g, rows, :])
        m_max = jnp.maximum(jnp.maximum(m_all[0], m_all[1]), m_all[2])
        ws = [jnp.exp(m - m_max) for m in m_all]
        num = ws[0] * a_all[0] + ws[1] * a_all[1] + ws[2] * a_all[2]
        den = ws[0] * l_all[0] + ws[1] * l_all[1] + ws[2] * l_all[2]
        o_ref[pl.ds(res, length, stride=rate_max), :] = num / den


def _dil_attention(main3):
    b, t, _ = main3.shape
    assert all(t % (rate * DIL_BLOCK) == 0 for _, rate in DIL_PAIRS)
    npair = DIL_HEADS // 2
    nbr = len(DIL_PAIRS)

    def col(start):
        return pl.BlockSpec((None, t, LANES), lambda bi, p: (bi, 0, start // LANES + p))

    return pl.pallas_call(
        _dil_kernel,
        grid=(b, npair),
        in_specs=[col(COL_DIL_QK), col(COL_DIL_QK + DIL_WIDTH), col(COL_DIL_V)],
        out_specs=pl.BlockSpec((None, t, LANES), lambda bi, p: (bi, 0, p)),
        out_shape=jax.ShapeDtypeStruct((b, t, DIL_WIDTH), F32),
        name="dilated_attention",
        scratch_shapes=[pltpu.VMEM((t, LANES), F32)] * 3
        + [pltpu.VMEM((nbr, t + DIL_BLOCK, LANES), BF16)] * 3
        + [pltpu.VMEM((nbr, t, LANES), F32)] * 3,
        compiler_params=_cparams(("parallel", "parallel")),
    )(main3, main3, main3)


def _merge_kernel(x_ref, gf_ref, gd_ref, gm_ref, of_ref, od_ref, om_ref,
                  wf_ref, wd_ref, wm_ref, wo_ref, o_ref):
    merged = (jax.nn.sigmoid(gf_ref[...].astype(F32)) * _dot(of_ref[...], wf_ref[...])
              + jax.nn.sigmoid(gd_ref[...].astype(F32)) * _dot(od_ref[...].astype(BF16), wd_ref[...])
              + jax.nn.sigmoid(gm_ref[...].astype(F32)) * _dot(om_ref[...], wm_ref[...]))
    o_ref[...] = x_ref[...] + _dot(merged.astype(BF16), wo_ref[...])


def _merge(x2, main2, o_fox, o_dil, o_mla, layer, wf, wd, wm, wo, tm=512):
    n = x2.shape[0]

    def tok(width, col=0):
        return pl.BlockSpec((tm, width), lambda i: (i, col))

    def full(w):
        return pl.BlockSpec((None,) + w.shape[1:], lambda i: (layer, 0, 0))

    return pl.pallas_call(
        _merge_kernel,
        grid=(n // tm,),
        in_specs=[tok(D_MODEL), tok(D_MODEL, 0), tok(D_MODEL, 1), tok(D_MODEL, 2),
                  tok(FOX_WIDTH), tok(DIL_WIDTH), tok(MLA_WIDTH),
                  full(wf), full(wd), full(wm), full(wo)],
        out_specs=tok(D_MODEL),
        out_shape=jax.ShapeDtypeStruct((n, D_MODEL), F32),
        name="merge_outproj",
        compiler_params=_cparams(("parallel",)),
    )(x2, main2, main2, main2, o_fox, o_dil, o_mla, wf, wd, wm, wo)


def _ffn_kernel(x_ref, g_ref, wg_ref, wu_ref, wd_ref, p_ref, pg_ref, wpg_ref, wpp_ref,
                o_ref, h_sc, acc_sc):
    c = pl.program_id(1)

    @pl.when(c == 0)
    def _():
        x = x_ref[...]
        h_sc[...] = _rms(x, g_ref[...]).astype(BF16)
        acc_sc[...] = x

    h = h_sc[...]
    gate = _dot(h, wg_ref[...])
    up = _dot(h, wu_ref[...])
    act = gate * jax.nn.sigmoid(gate) * up
    acc_sc[...] += _dot(act.astype(BF16), wd_ref[...])

    @pl.when(c == pl.num_programs(1) - 1)
    def _():
        x = acc_sc[...]
        hp = _rms(x, pg_ref[...]).astype(BF16)
        gate_p = jax.nn.sigmoid(_dot(hp, wpg_ref[...]))
        o_ref[...] = x + gate_p * _dot(p_ref[...].astype(BF16), wpp_ref[...])


def _ffn_ple(x2, gain, j, wg, wu, wd, layer, p3, ple_gain, wpg, wpp, tm=512):
    n = x2.shape[0]
    tok = pl.BlockSpec((tm, D_MODEL), lambda i, c: (i, 0))
    vec = pl.BlockSpec((1, D_MODEL), lambda i, c: (0, 0))
    return pl.pallas_call(
        _ffn_kernel,
        grid=(n // tm, D_FF // D_FF_EXPERT),
        in_specs=[tok, vec,
                  pl.BlockSpec((None, D_MODEL, D_FF_EXPERT), lambda i, c: (j, 0, c)),
                  pl.BlockSpec((None, D_MODEL, D_FF_EXPERT), lambda i, c: (j, 0, c)),
                  pl.BlockSpec((None, D_FF_EXPERT, D_MODEL), lambda i, c: (j, c, 0)),
                  pl.BlockSpec((None, tm, PLE_DIM), lambda i, c: (layer, i, 0)), vec,
                  pl.BlockSpec((None,) + wpg.shape[1:], lambda i, c: (layer, 0, 0)),
                  pl.BlockSpec((None,) + wpp.shape[1:], lambda i, c: (layer, 0, 0))],
        out_specs=tok,
        out_shape=jax.ShapeDtypeStruct((n, D_MODEL), F32),
        name="ffn_dense_ple",
        scratch_shapes=[pltpu.VMEM((tm, D_MODEL), BF16), pltpu.VMEM((tm, D_MODEL), F32)],
        compiler_params=_cparams(("parallel", "arbitrary")),
    )(x2, gain, wg, wu, wd, p3, ple_gain, wpg, wpp)


MOE_BLOCK = 2048
MOE_TILE = 256
MOE_TILES = -(-(2 * MOE_BLOCK + N_EXPERTS * (MOE_TILE - 1)) // MOE_TILE)
MOE_META_ROWS = 32
assert MOE_TILES <= MOE_META_ROWS
MOE_CHUNK = 256
MOE_CHUNKS = MOE_BLOCK // MOE_CHUNK
MOE_WINDOW = 5
MOE_VMEM_LIMIT = 60 * 1024 * 1024


def _router_kernel(x_ref, g_ref, wh_ref, wm_ref, wl_ref, h_ref, rc_ref, rr_ref, meta_ref):
    h = _rms(x_ref[...], g_ref[...])
    h_ref[...] = h.astype(BF16)
    h_hi, h_mid, h_lo = _split3(h)
    w_hi, w_mid, w_lo = wh_ref[...], wm_ref[...], wl_ref[...]
    logits = (_dot(h_hi, w_hi) + (_dot(h_hi, w_mid) + _dot(h_mid, w_hi))
              + (_dot(h_hi, w_lo) + _dot(h_mid, w_mid) + _dot(h_lo, w_hi)))
    lane = _lane_iota()
    is_expert = lane < N_EXPERTS
    logits = jnp.where(is_expert, logits, NEG_INF)
    v1 = jnp.max(logits, axis=1, keepdims=True)
    i1 = jnp.min(jnp.where(logits == v1, lane, LANES), axis=1, keepdims=True)
    first = lane == i1
    rest = jnp.where(first, NEG_INF, logits)
    v2 = jnp.max(rest, axis=1, keepdims=True)
    i2 = jnp.min(jnp.where(rest == v2, lane, LANES), axis=1, keepdims=True)
    second = lane == i2
    e2 = jnp.exp(v2 - v1)
    w1 = 1.0 / (1.0 + e2)
    w2 = e2 / (1.0 + e2)

    sel = jnp.where(first, 1.0, jnp.where(second, 1.0, 0.0))
    row = lax.broadcasted_iota(jnp.int32, (CUMSUM_BLOCK, CUMSUM_BLOCK), 0)
    col = lax.broadcasted_iota(jnp.int32, (CUMSUM_BLOCK, CUMSUM_BLOCK), 1)
    tri = jnp.where(col < row, 1.0, 0.0).astype(BF16)
    carry = jnp.zeros((1, LANES), F32)
    ranks = []
    for blk in range(sel.shape[0] // CUMSUM_BLOCK):
        part = sel[blk * CUMSUM_BLOCK:(blk + 1) * CUMSUM_BLOCK]
        ranks.append(_dot(tri, part.astype(BF16)) + carry)
        carry = carry + jnp.sum(part, axis=0, keepdims=True)
    rank = jnp.concatenate(ranks, axis=0)
    padded = jnp.ceil(carry / MOE_TILE) * MOE_TILE
    er = lax.broadcasted_iota(jnp.int32, (LANES, LANES), 0)
    ec = lax.broadcasted_iota(jnp.int32, (LANES, LANES), 1)
    before = jnp.where(er < ec, 1.0, 0.0).astype(BF16)
    start = _dot(jnp.broadcast_to(padded, (8, LANES)).astype(BF16), before)[0:1]
    slot = start + rank
    dest1 = jnp.sum(jnp.where(first, slot, 0.0), axis=1, keepdims=True)
    dest2 = jnp.sum(jnp.where(second, slot, 0.0), axis=1, keepdims=True)
    routing = jnp.where(lane == 0, dest1, jnp.where(lane == 1, dest2,
                        jnp.where(lane == 2, w1, jnp.where(lane == 3, w2, 0.0))))
    rc_ref[...] = routing
    rrow = routing.T[0:8]
    for c in range(MOE_BLOCK // MOE_CHUNK):
        rr_ref[c] = rrow[:, c * MOE_CHUNK:(c + 1) * MOE_CHUNK]

    end = start + padded
    tile_row = lax.broadcasted_iota(jnp.int32, (MOE_META_ROWS, LANES), 0).astype(F32) * MOE_TILE
    passed = jnp.sum(jnp.where(is_expert, jnp.where(tile_row >= end, 1.0, 0.0), 0.0),
                     axis=1, keepdims=True)
    last = jnp.max(jnp.where(is_expert, jnp.where(padded > 0, lane.astype(F32), 0.0), 0.0),
                   axis=1, keepdims=True)
    total = jnp.sum(jnp.where(is_expert, padded, 0.0), axis=1, keepdims=True)
    active = jnp.where(tile_row < total, 1.0, 0.0)
    nblock = rrow.shape[1]
    tok = lax.broadcasted_iota(jnp.int32, (MOE_META_ROWS, nblock), 1).astype(F32)
    tile_id = lax.broadcasted_iota(jnp.int32, (MOE_META_ROWS, nblock), 0).astype(F32)
    in_tile = jnp.where(jnp.floor(rrow[0:1] / MOE_TILE) == tile_id, 1.0,
                        jnp.where(jnp.floor(rrow[1:2] / MOE_TILE) == tile_id, 1.0, 0.0))
    first_tok = jnp.min(jnp.where(in_tile > 0, tok, float(nblock)), axis=1, keepdims=True)
    last_tok = jnp.max(jnp.where(in_tile > 0, tok, -1.0), axis=1, keepdims=True)
    chunk_lo = jnp.floor(first_tok / MOE_CHUNK)
    chunk_hi = jnp.floor(last_tok / MOE_CHUNK)
    meta = jnp.where(lane == 0, jnp.minimum(passed, last),
                     jnp.where(lane == 1, active,
                               jnp.where(lane == 2, chunk_lo, jnp.where(lane == 3, chunk_hi, 0.0))))
    meta_ref[...] = meta.astype(jnp.int32)


def _router(x2, gain, w_router):
    n = x2.shape[0]
    nblk = n // MOE_BLOCK
    w_pad = jnp.zeros((D_MODEL, LANES), F32).at[:, :N_EXPERTS].set(w_router)
    w_hi = w_pad.astype(BF16)
    r1 = w_pad - w_hi.astype(F32)
    w_mid = r1.astype(BF16)
    w_lo = (r1 - w_mid.astype(F32)).astype(BF16)
    wspec = pl.BlockSpec((D_MODEL, LANES), lambda i: (0, 0))
    return pl.pallas_call(
        _router_kernel,
        grid=(nblk,),
        in_specs=[pl.BlockSpec((MOE_BLOCK, D_MODEL), lambda i: (i, 0)),
                  pl.BlockSpec((1, D_MODEL), lambda i: (0, 0)), wspec, wspec, wspec],
        out_specs=[pl.BlockSpec((MOE_BLOCK, D_MODEL), lambda i: (i, 0)),
                   pl.BlockSpec((MOE_BLOCK, LANES), lambda i: (i, 0)),
                   pl.BlockSpec((None, MOE_CHUNKS, 8, MOE_CHUNK), lambda i: (i, 0, 0, 0)),
                   pl.BlockSpec((None, MOE_META_ROWS, LANES), lambda i: (i, 0, 0))],
        out_shape=[jax.ShapeDtypeStruct((n, D_MODEL), BF16),
                   jax.ShapeDtypeStruct((n, LANES), F32),
                   jax.ShapeDtypeStruct((nblk, MOE_CHUNKS, 8, MOE_CHUNK), F32),
                   jax.ShapeDtypeStruct((nblk, MOE_META_ROWS, LANES), jnp.int32)],
        name="router",
        compiler_params=_cparams(("parallel",)),
    )(x2, gain, w_hi, w_mid, w_lo)


def _moe_kernel(te_ref, ta_ref, lo_ref, hi_ref, h_ref, rrow_ref, rcol_ref, wg_ref, wu_ref, wd_ref,
                o_ref):
    b = pl.program_id(0)
    r = pl.program_id(1)

    @pl.when(r == 0)
    def _():
        o_ref[...] = jnp.zeros(o_ref.shape, F32)

    def tile_work(first_chunk, nchunks):
        base = (r * MOE_TILE).astype(F32)
        slot_col = lax.broadcasted_iota(jnp.int32, (MOE_TILE, 1), 0).astype(F32) + base
        slot_row = lax.broadcasted_iota(jnp.int32, (1, MOE_TILE), 1).astype(F32) + base
        routing = rrow_ref[pl.ds(first_chunk, nchunks)]
        onehots = []
        weight = jnp.zeros((MOE_TILE, 1), F32)
        for c in range(nchunks):
            hit1 = routing[c, 0:1, :] == slot_col
            hit2 = routing[c, 1:2, :] == slot_col
            onehots.append(jnp.where(hit1, 1.0, jnp.where(hit2, 1.0, 0.0)).astype(BF16))
            weight = weight + jnp.sum(
                jnp.where(hit1, routing[c, 2:3, :], jnp.where(hit2, routing[c, 3:4, :], 0.0)),
                axis=1, keepdims=True)
        toks = pl.ds(pl.multiple_of(first_chunk * MOE_CHUNK, MOE_CHUNK), nchunks * MOE_CHUNK)
        xg = _dot(jnp.concatenate(onehots, axis=1), h_ref[toks, :]).astype(BF16)
        gate = _dot(xg, wg_ref[...])
        up = _dot(xg, wu_ref[...])
        act = gate * jax.nn.sigmoid(gate) * up
        y = (_dot(act.astype(BF16), wd_ref[...]) * weight).astype(BF16)
        back = jnp.where(rcol_ref[toks, 0:1] == slot_row, 1.0,
                         jnp.where(rcol_ref[toks, 1:2] == slot_row, 1.0, 0.0)).astype(BF16)
        o_ref[toks, :] += _dot(back, y)

    active = ta_ref[b, r] == 1
    first = jnp.minimum(lo_ref[b, r], MOE_CHUNKS - MOE_WINDOW)
    short = hi_ref[b, r] < first + MOE_WINDOW

    @pl.when(active & short)
    def _():
        tile_work(first, MOE_WINDOW)

    @pl.when(active & jnp.logical_not(short))
    def _():
        tile_work(0, MOE_CHUNKS)


def _moe(h2, rrow, rcol, meta, j, wg, wu, wd):
    n = h2.shape[0]
    nblk = n // MOE_BLOCK
    tile_tables = [meta[:, :MOE_TILES, k] for k in range(4)]

    def blocked(shape, index):
        return pl.BlockSpec(shape, lambda b, r, te, ta, lo, hi: index(b))

    def wspec(shape):
        return pl.BlockSpec((None,) + shape,
                            lambda b, r, te, ta, lo, hi: (j * N_EXPERTS + te[b, r], 0, 0))

    grid_spec = pltpu.PrefetchScalarGridSpec(
        num_scalar_prefetch=len(tile_tables),
        grid=(nblk, MOE_TILES),
        in_specs=[blocked((MOE_BLOCK, D_MODEL), lambda b: (b, 0)),
                  blocked((None, MOE_CHUNKS, 8, MOE_CHUNK), lambda b: (b, 0, 0, 0)),
                  blocked((MOE_BLOCK, LANES), lambda b: (b, 0)),
                  wspec((D_MODEL, D_FF_EXPERT)), wspec((D_MODEL, D_FF_EXPERT)),
                  wspec((D_FF_EXPERT, D_MODEL))],
        out_specs=blocked((MOE_BLOCK, D_MODEL), lambda b: (b, 0)),
    )
    return pl.pallas_call(
        _moe_kernel,
        grid_spec=grid_spec,
        out_shape=jax.ShapeDtypeStruct((n, D_MODEL), F32),
        name="moe_experts",
        compiler_params=pltpu.CompilerParams(dimension_semantics=("parallel", "arbitrary"),
                                             vmem_limit_bytes=MOE_VMEM_LIMIT),
    )(*tile_tables, h2, rrow, rcol, wg, wu, wd)


def _moe_layer(x2, gain, w_router, j, wg, wu, wd):
    h2, rcol, rrow, meta = _router(x2, gain, w_router)
    return _moe(h2, rrow, rcol, meta, j, wg, wu, wd)


def _ple_kernel(x_ref, d_ref, p_ref, g_ref, wg_ref, wp_ref, fg_ref, o_ref, *, final):
    x = x_ref[...] + d_ref[...]
    h = _rms(x, g_ref[...]).astype(BF16)
    gate = jax.nn.sigmoid(_dot(h, wg_ref[...]))
    y = x + gate * _dot(p_ref[...].astype(BF16), wp_ref[...])
    if final:
        y = _rms(y, fg_ref[...])
    o_ref[...] = y


def _ple(x2, delta, layer, p3, gain, wg, wp, final_gain, final, tm=512):
    n = x2.shape[0]
    vec = pl.BlockSpec((1, D_MODEL), lambda i: (0, 0))
    tok = pl.BlockSpec((tm, D_MODEL), lambda i: (i, 0))
    return pl.pallas_call(
        functools.partial(_ple_kernel, final=final),
        grid=(n // tm,),
        in_specs=[tok, tok,
                  pl.BlockSpec((None, tm, PLE_DIM), lambda i: (layer, i, 0)), vec,
                  pl.BlockSpec((None,) + wg.shape[1:], lambda i: (layer, 0, 0)),
                  pl.BlockSpec((None,) + wp.shape[1:], lambda i: (layer, 0, 0)), vec],
        out_specs=pl.BlockSpec((tm, D_MODEL), lambda i: (i, 0)),
        out_shape=jax.ShapeDtypeStruct((n, D_MODEL), F32),
        name="ple_final" if final else "ple",
        compiler_params=_cparams(("parallel",)),
    )(x2, delta, p3, gain, wg, wp, final_gain)


def _cast_kernel(x_ref, o_ref):
    o_ref[...] = x_ref[...].astype(BF16)


def _to_bf16(a):
    r, c = a.shape[-2:]
    a3 = a.reshape((-1, r, c))
    spec = pl.BlockSpec((None, r, c), lambda i: (i, 0, 0))
    out = pl.pallas_call(
        _cast_kernel,
        grid=(a3.shape[0],),
        in_specs=[spec],
        out_specs=spec,
        out_shape=jax.ShapeDtypeStruct(a3.shape, BF16),
        name="cast_bf16",
        compiler_params=_cparams(("parallel",)),
    )(a3)
    return out.reshape(a.shape)


IN_SIZES = (FOX_WIDTH, FOX_WIDTH, FOX_WIDTH, FOX_HEADS, DIL_WIDTH, DIL_WIDTH, DIL_WIDTH,
            MLA_Q_RANK, MLA_KV_RANK, MLA_ROPE_DIM, GATE_WIDTH)
IN_WIDTH = sum(IN_SIZES)
IN_PREP_COLS = 256


def _inproj_weight_kernel(w_ref, main_ref, small_ref):
    offs = [0]
    for s in IN_SIZES:
        offs.append(offs[-1] + s)
    fq, fk, fv, fz, dq, dk, dv, cq, ckv, kr, gates = [(offs[i], offs[i + 1])
                                                      for i in range(len(IN_SIZES))]
    scale = HEAD_DIM ** -0.5
    row = 0
    for (lo, hi), mult in ((gates, 1.0), (dq, scale), (dk, 1.0), (fq, scale), (fk, 1.0), (fv, 1.0),
                           (dv, 1.0), (cq, 1.0), (ckv, 1.0)):
        main_ref[row:row + hi - lo, :] = (w_ref[lo:hi, :] * mult).astype(BF16)
        row += hi - lo
    small_ref[...] = jnp.zeros(small_ref.shape, BF16)
    small_ref[0:FOX_HEADS, :] = w_ref[fz[0]:fz[1], :].astype(BF16)
    for h in range(MLA_HEADS):
        r0 = LANES + h * MLA_ROPE_DIM
        small_ref[r0:r0 + MLA_ROPE_DIM, :] = w_ref[kr[0]:kr[1], :].astype(BF16)


def _prep_in_weights(w_in):
    depth = w_in.shape[0]
    cb = IN_PREP_COLS
    w_t = jnp.swapaxes(w_in, 1, 2)

    def blk(rows):
        return pl.BlockSpec((None, rows, cb), lambda l, i: (l, 0, i))

    return pl.pallas_call(
        _inproj_weight_kernel,
        grid=(depth, D_MODEL // cb),
        in_specs=[blk(IN_WIDTH)],
        out_specs=[blk(MAIN_WIDTH), blk(SMALL_WIDTH)],
        out_shape=[jax.ShapeDtypeStruct((depth, MAIN_WIDTH, D_MODEL), BF16),
                   jax.ShapeDtypeStruct((depth, SMALL_WIDTH, D_MODEL), BF16)],
        name="inproj_weights",
        compiler_params=_cparams(("parallel", "parallel")),
    )(w_t)


def _prep_mla_weights(w_uq, w_ukv):
    depth = w_uq.shape[0]
    uq = w_uq.reshape(depth, MLA_Q_RANK, MLA_HEADS, MLA_NOPE_DIM + MLA_ROPE_DIM)
    wq = jnp.concatenate([uq[..., :MLA_NOPE_DIM].reshape(depth, MLA_Q_RANK, -1),
                          uq[..., MLA_NOPE_DIM:].reshape(depth, MLA_Q_RANK, -1)], axis=-1)
    ukv = w_ukv.reshape(depth, MLA_KV_RANK, MLA_HEADS, MLA_NOPE_DIM + MLA_V_DIM)
    wkv = jnp.concatenate([ukv[..., :MLA_NOPE_DIM].reshape(depth, MLA_KV_RANK, -1),
                           ukv[..., MLA_NOPE_DIM:].reshape(depth, MLA_KV_RANK, -1)], axis=-1)
    return wq.astype(BF16), wkv.astype(BF16)


def kernel(x, p, positions, mix_norm, w_in, b_forget, mla_q_norm, mla_kv_norm, w_uq, w_ukv,
           w_br_fox, w_br_dil, w_br_mla, w_out, ffn_norm, w_ffn_gate, w_ffn_up, w_ffn_down,
           w_router, w_exp_gate, w_exp_up, w_exp_down, ple_norm, w_ple_gate, w_ple_proj,
           final_norm):
    b, t, _ = x.shape
    n = b * t
    depth = w_in.shape[0]
    assert depth % 2 == 0

    w_main, w_small = _prep_in_weights(w_in)
    wq_all, wkv_all = _prep_mla_weights(w_uq, w_ukv)
    (w_br_fox, w_br_dil, w_br_mla, w_out, w_ffn_gate, w_ffn_up, w_ffn_down, w_exp_gate, w_exp_up,
     w_exp_down, w_ple_gate, w_ple_proj) = [
        _to_bf16(w) for w in (w_br_fox, w_br_dil, w_br_mla, w_out, w_ffn_gate, w_ffn_up, w_ffn_down,
                              w_exp_gate, w_exp_up, w_exp_down, w_ple_gate, w_ple_proj)]
    w_exp_gate, w_exp_up, w_exp_down = [w.reshape((-1,) + w.shape[2:])
                                        for w in (w_exp_gate, w_exp_up, w_exp_down)]
    p3 = p.reshape(depth, n, PLE_DIM)
    bias_all = jnp.concatenate(
        [b_forget.astype(F32), jnp.zeros((depth, LANES - FOX_HEADS), F32)], axis=-1)
    tabs = [tab.reshape(n, LANES) for tab in _rope_tables(positions)]
    final_gain = final_norm.reshape(1, D_MODEL)

    x2 = x.reshape(n, D_MODEL)
    for i in range(depth):
        main2, fz, qn, qr, kn, kr4, vv = _inproj(
            x2, mix_norm[i].reshape(1, D_MODEL), i, w_main, w_small, tabs,
            mla_q_norm[i].reshape(1, -1), mla_kv_norm[i].reshape(1, -1), wq_all[i], wkv_all[i])
        main3 = main2.reshape(b, t, COL_MLA)

        dcols = _decay(fz.reshape(b, t, LANES), bias_all[i].reshape(1, LANES))
        o_fox = _fox_attention(main3, dcols)
        o_dil = _dil_attention(main3)
        o_mla = _mla_attention(*[a.reshape(b, t, -1) for a in (qn, qr, kn, kr4, vv)])

        x2 = _merge(x2, main2, o_fox.reshape(n, -1), o_dil.reshape(n, -1), o_mla.reshape(n, -1),
                    i, w_br_fox, w_br_dil, w_br_mla, w_out)

        j = i // 2
        gain = ffn_norm[i].reshape(1, D_MODEL)
        ple_args = (i, p3, ple_norm[i].reshape(1, D_MODEL), w_ple_gate, w_ple_proj)
        if i % 2 == 0:
            x2 = _ffn_ple(x2, gain, j, w_ffn_gate, w_ffn_up, w_ffn_down, *ple_args)
        else:
            delta = _moe_layer(x2, gain, w_router[j], j, w_exp_gate, w_exp_up, w_exp_down)
            x2 = _ple(x2, delta, *ple_args, final_gain, final=(i == depth - 1))
    return x2.reshape(b, t, D_MODEL)
```

```python
import functools

import jax
import jax.numpy as jnp
from jax import lax
from jax.experimental import pallas as pl
from jax.experimental.pallas import tpu as pltpu

F32 = jnp.float32
BF16 = jnp.bfloat16

D_MODEL = 1024
HEAD_DIM = 64
ROPE_THETA = 10000.0
NORM_EPS = 1e-6
FOX_HEADS = 6
FOX_WIDTH = FOX_HEADS * HEAD_DIM
DIL_HEADS = 6
DIL_WIDTH = DIL_HEADS * HEAD_DIM
DIL_PAIRS = ((128, 1), (512, 4), (2048, 16))
DIL_BLOCK = 128
MLA_HEADS = 4
MLA_Q_RANK = 256
MLA_KV_RANK = 256
MLA_NOPE_DIM = 64
MLA_ROPE_DIM = 32
MLA_V_DIM = 64
MLA_WIDTH = MLA_HEADS * MLA_V_DIM
N_BRANCHES = 3
D_FF = 2816
N_EXPERTS = 8
D_FF_EXPERT = 1408
PLE_DIM = 256

LANES = 128
GATE_WIDTH = N_BRANCHES * D_MODEL
COL_GATES = 0
COL_DIL_QK = GATE_WIDTH
COL_FOX = COL_DIL_QK + 2 * DIL_WIDTH
COL_DIL_V = COL_FOX + 3 * FOX_WIDTH
COL_MLA = COL_DIL_V + DIL_WIDTH
MAIN_WIDTH = COL_MLA + MLA_Q_RANK + MLA_KV_RANK
MXU_TILE = 256
assert all(c % MXU_TILE == 0 for c in (COL_DIL_QK, COL_FOX, COL_MLA, MAIN_WIDTH))
SMALL_WIDTH = 2 * LANES
VMEM_LIMIT = 56 * 1024 * 1024
VMEM_LIMIT_LARGE = 60 * 1024 * 1024

NEG_INF = float("-inf")


def _cparams(sem):
    return pltpu.CompilerParams(dimension_semantics=sem, vmem_limit_bytes=VMEM_LIMIT)


def _rms(x, gain):
    return x * lax.rsqrt(jnp.mean(x * x, axis=-1, keepdims=True) + NORM_EPS) * gain


def _dot(a, b):
    return jnp.dot(a, b, preferred_element_type=F32)


def _dot_nt(a, b):
    return lax.dot_general(a, b, (((1,), (1,)), ((), ())), preferred_element_type=F32)


def _lane_iota():
    return lax.broadcasted_iota(jnp.int32, (1, LANES), 1)


def _rope_tab_kernel(pos_ref, fd_ref, fm_ref, cd_ref, sd_ref, cm_ref, sm_ref):
    pos = pos_ref[...].astype(F32)
    lane = _lane_iota()
    ang = pos * fd_ref[...]
    cd_ref[...] = jnp.cos(ang)
    sd_ref[...] = jnp.sin(ang) * jnp.where((lane % HEAD_DIM) < HEAD_DIM // 2, -1.0, 1.0)
    ang = pos * fm_ref[...]
    cm_ref[...] = jnp.cos(ang)
    sm_ref[...] = jnp.sin(ang) * jnp.where((lane % MLA_ROPE_DIM) < MLA_ROPE_DIM // 2, -1.0, 1.0)


def _rope_tables(positions):
    b, t = positions.shape
    half_d = HEAD_DIM // 2
    half_m = MLA_ROPE_DIM // 2
    inv_d = ROPE_THETA ** (-jnp.arange(half_d, dtype=F32) / half_d)
    inv_m = ROPE_THETA ** (-jnp.arange(half_m, dtype=F32) / half_m)
    fd = jnp.tile(inv_d, LANES // half_d)[None, :]
    fm = jnp.tile(inv_m, LANES // half_m)[None, :]
    tab = jax.ShapeDtypeStruct((b, t, LANES), F32)
    tab_spec = pl.BlockSpec((None, t, LANES), lambda i: (i, 0, 0))
    vec_spec = pl.BlockSpec((1, LANES), lambda i: (0, 0))
    return pl.pallas_call(
        _rope_tab_kernel,
        grid=(b,),
        in_specs=[pl.BlockSpec((None, t, 1), lambda i: (i, 0, 0)), vec_spec, vec_spec],
        out_specs=[tab_spec] * 4,
        out_shape=[tab] * 4,
        name="rope_tables",
        compiler_params=_cparams(("parallel",)),
    )(positions.reshape(b, t, 1), fd, fm)


def _swap_halves(x, width):
    lane = _lane_iota()
    half = width // 2
    return jnp.where((lane % width) < half,
                     pltpu.roll(x, LANES - half, axis=1),
                     pltpu.roll(x, half, axis=1))


def _rope(x, cos, sin_signed, width):
    return x * cos + _swap_halves(x, width) * sin_signed


def _col_chunks(width, step=512):
    return [(c, min(step, width - c)) for c in range(0, width, step)]


MLA_NOPE_WIDTH = MLA_HEADS * MLA_NOPE_DIM
MLA_SCALE = (MLA_NOPE_DIM + MLA_ROPE_DIM) ** -0.5


def _inproj_kernel(x_ref, g_ref, w_ref, ws_ref, cd_ref, sd_ref, cm_ref, sm_ref,
                   gq_ref, gkv_ref, wq_ref, wkv_ref,
                   o_ref, fz_ref, qn_ref, qr_ref, kn_ref, kr_ref, v_ref):
    h = _rms(x_ref[...], g_ref[...]).astype(BF16)
    for lo, hi in ((0, COL_DIL_QK), (COL_FOX, COL_MLA)):
        for c, n in _col_chunks(hi - lo):
            o_ref[:, lo + c:lo + c + n] = _dot_nt(h, w_ref[lo + c:lo + c + n, :]).astype(BF16)
    cos_d = cd_ref[...]
    sin_d = sd_ref[...]
    for c0 in range(COL_DIL_QK, COL_FOX, MXU_TILE):
        y = _dot_nt(h, w_ref[c0:c0 + MXU_TILE, :])
        for c in range(0, MXU_TILE, LANES):
            o_ref[:, c0 + c:c0 + c + LANES] = _rope(y[:, c:c + LANES], cos_d, sin_d,
                                                    HEAD_DIM).astype(BF16)

    cos_m = cm_ref[...]
    sin_m = sm_ref[...]
    lat = _dot_nt(h, w_ref[COL_MLA:MAIN_WIDTH, :])
    hq = _rms(lat[:, :MLA_Q_RANK], gq_ref[...]).astype(BF16)
    q = _dot(hq, wq_ref[...])
    qn_ref[...] = (q[:, :MLA_NOPE_WIDTH] * MLA_SCALE).astype(BF16)
    qr_ref[...] = (_rope(q[:, MLA_NOPE_WIDTH:], cos_m, sin_m, MLA_ROPE_DIM) * MLA_SCALE).astype(BF16)
    hkv = _rms(lat[:, MLA_Q_RANK:], gkv_ref[...]).astype(BF16)
    kv = _dot(hkv, wkv_ref[...])
    kn_ref[...] = kv[:, :MLA_NOPE_WIDTH].astype(BF16)
    v_ref[...] = kv[:, MLA_NOPE_WIDTH:].astype(BF16)

    small = _dot_nt(h, ws_ref[...])
    fz_ref[...] = small[:, :LANES]
    kr_ref[...] = _rope(small[:, LANES:], cos_m, sin_m, MLA_ROPE_DIM).astype(BF16)


def _inproj(x2, gain, layer, w_main_t, w_small_t, tabs, gq, gkv, wq, wkv, tm=512):
    n = x2.shape[0]

    def tok(width):
        return pl.BlockSpec((tm, width), lambda i: (i, 0))

    def full(a):
        return pl.BlockSpec(a.shape, lambda i: (0, 0))

    def of_layer(a):
        return pl.BlockSpec((None,) + a.shape[1:], lambda i: (layer, 0, 0))

    widths = (COL_MLA, LANES, MLA_NOPE_WIDTH, LANES, MLA_NOPE_WIDTH, LANES, MLA_WIDTH)
    dtypes = (BF16, F32, BF16, BF16, BF16, BF16, BF16)
    return pl.pallas_call(
        _inproj_kernel,
        grid=(n // tm,),
        in_specs=[tok(D_MODEL), full(gain), of_layer(w_main_t), of_layer(w_small_t)]
        + [tok(LANES)] * 4 + [full(gq), full(gkv), full(wq), full(wkv)],
        out_specs=[tok(w) for w in widths],
        out_shape=[jax.ShapeDtypeStruct((n, w), d) for w, d in zip(widths, dtypes)],
        name="inproj",
        compiler_params=_cparams(("parallel",)),
    )(x2, gain, w_main_t, w_small_t, *tabs, gq, gkv, wq, wkv)


CUMSUM_BLOCK = 256
DECAY_PART_STRIDE = 8
DECAY_PARTS = 3


def _split3(a):
    hi = a.astype(BF16)
    r1 = a - hi.astype(F32)
    mid = r1.astype(BF16)
    lo = (r1 - mid.astype(F32)).astype(BF16)
    return hi, mid, lo


def _decay_kernel(z_ref, b_ref, o_ref):
    t = z_ref.shape[0]
    row = lax.broadcasted_iota(jnp.int32, (CUMSUM_BLOCK, CUMSUM_BLOCK), 0)
    col = lax.broadcasted_iota(jnp.int32, (CUMSUM_BLOCK, CUMSUM_BLOCK), 1)
    tri = jnp.where(col <= row, 1.0, 0.0).astype(BF16)
    is_head = _lane_iota() < FOX_HEADS
    carry = jnp.zeros((1, LANES), F32)
    for blk in range(t // CUMSUM_BLOCK):
        sl = slice(blk * CUMSUM_BLOCK, (blk + 1) * CUMSUM_BLOCK)
        z = z_ref[sl, :] + b_ref[...]
        logf = -(jnp.maximum(-z, 0.0) + jnp.log1p(jnp.exp(-jnp.abs(z))))
        hi, mid, lo = _split3(logf)
        cs = _dot(tri, hi) + _dot(tri, mid) + _dot(tri, lo) + carry
        carry = cs[CUMSUM_BLOCK - 1:CUMSUM_BLOCK, :]
        parts = _split3(jnp.where(is_head, cs, 0.0))
        packed = parts[0].astype(F32)
        for j in range(1, DECAY_PARTS):
            packed = packed + pltpu.roll(parts[j].astype(F32), j * DECAY_PART_STRIDE, axis=1)
        o_ref[sl, :] = packed.astype(BF16)


def _decay(small3, bias):
    b, t, _ = small3.shape
    return pl.pallas_call(
        _decay_kernel,
        grid=(b,),
        in_specs=[pl.BlockSpec((None, t, LANES), lambda i: (i, 0, 0)),
                  pl.BlockSpec((1, LANES), lambda i: (0, 0))],
        out_specs=pl.BlockSpec((None, t, LANES), lambda i: (i, 0, 0)),
        out_shape=jax.ShapeDtypeStruct((b, t, LANES), BF16),
        name="decay",
        compiler_params=_cparams(("parallel",)),
    )(small3, bias)


def _pair_softmax(s, m_prev, ok_cols):
    nc = s.shape[1] // LANES
    cols = [s[:, c * LANES:(c + 1) * LANES] for c in range(nc)]
    if ok_cols is not None:
        cols = [jnp.where(ok, col, NEG_INF) for ok, col in zip(ok_cols, cols)]
    cmax = cols[0]
    for col in cols[1:]:
        cmax = jnp.maximum(cmax, col)
    m_new = jnp.maximum(m_prev, jnp.max(cmax, axis=1, keepdims=True))
    alpha = jnp.exp(m_prev - m_new)
    p = jnp.concatenate([jnp.exp(col - m_new).astype(BF16) for col in cols], axis=1)
    return m_new, alpha, p


def _causal_pair_attention(qs, keys_rows, vals_rows, qi, o_ref, tq):
    lane = _lane_iota()
    head0 = lane < HEAD_DIM

    def scores(j):
        return _dot_nt(qs, keys_rows(j * tq, (j + 1) * tq))

    def weighted_values(p, j):
        v = vals_rows(j * tq, (j + 1) * tq)
        one = jnp.ones_like(v)
        pv0 = _dot(p[:tq], jnp.where(head0, v, one))
        pv1 = _dot(p[tq:], jnp.where(head0, one, v))
        return jnp.concatenate([pv0, pv1], axis=0)

    s = scores(0)
    m_prev = jnp.full((2 * tq, LANES), NEG_INF, F32)
    acc = jnp.zeros((2 * tq, LANES), F32)
    for j in range(qi):
        s_next = scores(j + 1)
        m_prev, alpha, p = _pair_softmax(s, m_prev, None)
        acc = alpha * acc + weighted_values(p, j)
        s = s_next

    r_i = lax.broadcasted_iota(jnp.int32, (tq, LANES), 0)
    c_i = lax.broadcasted_iota(jnp.int32, (tq, LANES), 1)
    ok_cols = []
    for c in range(tq // LANES):
        ok = c_i + c * LANES <= r_i
        ok_cols.append(jnp.concatenate([ok, ok], axis=0))
    _, alpha, p = _pair_softmax(s, m_prev, ok_cols)
    acc = alpha * acc + weighted_values(p, qi)
    out = acc / pltpu.roll(acc, HEAD_DIM, axis=1)
    o_ref[...] = jnp.where(head0, out[:tq], out[tq:]).astype(o_ref.dtype)


def _fox_kernel(q_ref, k_ref, v_ref, d_ref, o_ref, *, tq):
    p = pl.program_id(1)
    lane = _lane_iota()

    def keys_at(lo, hi):
        return jnp.concatenate([k_ref[lo:hi, :], d_ref[lo:hi, :]], axis=1)

    def vals_at(lo, hi):
        return v_ref[lo:hi, :]

    for qi in range(q_ref.shape[0] // tq):
        q = q_ref[qi * tq:(qi + 1) * tq, :]
        zero = jnp.zeros_like(q)
        halves = []
        for h in range(2):
            head = 2 * p + h
            pick = (lane % DECAY_PART_STRIDE == head) & (lane < DECAY_PARTS * DECAY_PART_STRIDE)
            neg = jnp.broadcast_to(jnp.where(pick, -1.0, 0.0).astype(BF16), q.shape)
            q_h = jnp.where((lane // HEAD_DIM) == h, q, zero)
            halves.append(jnp.concatenate([q_h, neg], axis=1))
        qs = jnp.concatenate(halves, axis=0)
        _causal_pair_attention(qs, keys_at, vals_at, qi, o_ref.at[qi * tq:(qi + 1) * tq, :], tq)


def _fox_attention(main3, dcols, tq=512):
    b, t, _ = main3.shape
    cb = COL_FOX // LANES
    npair = FOX_HEADS // 2

    def seq(col):
        return pl.BlockSpec((None, t, LANES), lambda bi, p: (bi, 0, col(p)))

    return pl.pallas_call(
        functools.partial(_fox_kernel, tq=tq),
        grid=(b, npair),
        in_specs=[
            seq(lambda p: cb + p),
            seq(lambda p: cb + npair + p),
            seq(lambda p: cb + 2 * npair + p),
            seq(lambda p: 0),
        ],
        out_specs=pl.BlockSpec((None, t, LANES), lambda bi, p: (bi, 0, p)),
        out_shape=jax.ShapeDtypeStruct((b, t, FOX_WIDTH), BF16),
        name="fox_attention",
        compiler_params=_cparams(("parallel", "parallel")),
    )(main3, main3, main3, dcols)


def _mla_kernel(qn_ref, qr_ref, kn_ref, kr_ref, v_ref, o_ref, *, tq):
    p = pl.program_id(1)
    lane = _lane_iota()

    def keys_at(lo, hi):
        return jnp.concatenate([kn_ref[lo:hi, :], kr_ref[lo:hi, :]], axis=1)

    def vals_at(lo, hi):
        return v_ref[lo:hi, :]

    for qi in range(qn_ref.shape[0] // tq):
        qn = qn_ref[qi * tq:(qi + 1) * tq, :]
        qr = qr_ref[qi * tq:(qi + 1) * tq, :]
        zero = jnp.zeros_like(qn)
        halves = []
        for h in range(2):
            nope_h = jnp.where((lane // MLA_NOPE_DIM) == h, qn, zero)
            rope_h = jnp.where((lane // MLA_ROPE_DIM) == 2 * p + h, qr, zero)
            halves.append(jnp.concatenate([nope_h, rope_h], axis=1))
        qs = jnp.concatenate(halves, axis=0)
        _causal_pair_attention(qs, keys_at, vals_at, qi, o_ref.at[qi * tq:(qi + 1) * tq, :], tq)


def _mla_attention(qn, qr, kn, kr4, vv, tq=512):
    b, t, _ = qn.shape
    npair = MLA_HEADS // 2
    pair = pl.BlockSpec((None, t, LANES), lambda bi, p: (bi, 0, p))
    shared = pl.BlockSpec((None, t, LANES), lambda bi, p: (bi, 0, 0))
    return pl.pallas_call(
        functools.partial(_mla_kernel, tq=tq),
        grid=(b, npair),
        in_specs=[pair, shared, pair, shared, pair],
        out_specs=pair,
        out_shape=jax.ShapeDtypeStruct((b, t, MLA_WIDTH), BF16),
        name="mla_attention",
        compiler_params=_cparams(("parallel", "parallel")),
    )(qn, qr, kn, kr4, vv)


DIL_UNROLL = 8


def _dil_kernel(q_ref, k_ref, v_ref, o_ref, qf, kf, vf, qb, kb, vb, accb, mb, lb):
    t = q_ref.shape[0]
    blk = DIL_BLOCK
    lane = _lane_iota()
    head0 = lane < HEAD_DIM
    qf[...] = q_ref[...].astype(F32)
    kf[...] = k_ref[...].astype(F32)
    vf[...] = v_ref[...].astype(F32)

    for g, (_, rate) in enumerate(DIL_PAIRS):
        length = t // rate
        kb[g, 0:blk, :] = jnp.zeros((blk, LANES), BF16)
        vb[g, 0:blk, :] = jnp.zeros((blk, LANES), BF16)
        if rate == 1:
            qb[g, blk:, :] = q_ref[...]
            kb[g, blk:, :] = k_ref[...]
            vb[g, blk:, :] = v_ref[...]
            continue
        for res in range(rate):
            dst = slice(blk + res * length, blk + (res + 1) * length)
            src = pl.ds(res, length, stride=rate)
            qb[g, dst, :] = qf[src, :].astype(BF16)
            kb[g, dst, :] = kf[src, :].astype(BF16)
            vb[g, dst, :] = vf[src, :].astype(BF16)

    qi2 = lax.broadcasted_iota(jnp.int32, (blk, 2 * blk), 0)
    kj2 = lax.broadcasted_iota(jnp.int32, (blk, 2 * blk), 1)
    band = (kj2 >= qi2) & (kj2 <= qi2 + blk)
    bias_full = jnp.where(band, 0.0, NEG_INF)
    bias_first = jnp.where(band & (kj2 >= blk), 0.0, NEG_INF)

    for g, (window, rate) in enumerate(DIL_PAIRS):
        assert window // rate == blk
        nb = t // rate // blk
        nblocks = t // blk

        def body(j, carry, g=g, nb=nb):
            base = pl.multiple_of(j * blk, blk)
            q = qb[g, pl.ds(base + blk, blk), :]
            zero = jnp.zeros_like(q)
            qs = jnp.concatenate([jnp.where(head0, q, zero), jnp.where(head0, zero, q)], axis=0)
            keys = kb[g, pl.ds(base, 2 * blk), :]
            vals = vb[g, pl.ds(base, 2 * blk), :]
            bias = bias_first if nb == 1 else jnp.where((j % nb) != 0, bias_full, bias_first)
            s = _dot_nt(qs, keys)
            ps, ms = [], []
            for h in range(2):
                sh = s[h * blk:(h + 1) * blk] + bias
                m = jnp.max(sh, axis=1, keepdims=True)
                ps.append(jnp.exp(sh - m).astype(BF16))
                ms.append(m)
            pv = _dot(jnp.concatenate(ps, axis=0),
                      jnp.concatenate([vals, jnp.ones_like(vals)], axis=1))
            rows = pl.ds(base, blk)
            accb[g, rows, :] = jnp.where(head0, pv[:blk, :LANES], pv[blk:, :LANES])
            mb[g, rows, :] = jnp.where(head0, ms[0], ms[1])
            lb[g, rows, :] = jnp.where(head0, pv[:blk, LANES:], pv[blk:, LANES:])
            return carry

        lax.fori_loop(0, nblocks, body, 0, unroll=DIL_UNROLL)

    rate_max = max(rate for _, rate in DIL_PAIRS)
    length = t // rate_max
    for res in range(rate_max):
        m_all, l_all, a_all = [], [], []
        for g, (_, rate) in enumerate(DIL_PAIRS):
            start = (res % rate) * (t // rate) + res // rate
            step = rate_max // rate
            rows = pl.ds(start, length) if step == 1 else pl.ds(start, length, stride=step)
            m_all.append(mb[g, rows, :])
            l_all.append(lb[g, rows, :])
            a_all.append(accb[g, rows, :])
        m_max = jnp.maximum(jnp.maximum(m_all[0], m_all[1]), m_all[2])
        ws = [jnp.exp(m - m_max) for m in m_all]
        num = ws[0] * a_all[0] + ws[1] * a_all[1] + ws[2] * a_all[2]
        den = ws[0] * l_all[0] + ws[1] * l_all[1] + ws[2] * l_all[2]
        o_ref[pl.ds(res, length, stride=rate_max), :] = num / den


def _dil_attention(main3):
    b, t, _ = main3.shape
    assert all(t % (rate * DIL_BLOCK) == 0 for _, rate in DIL_PAIRS)
    npair = DIL_HEADS // 2
    nbr = len(DIL_PAIRS)

    def col(start):
        return pl.BlockSpec((None, t, LANES), lambda bi, p: (bi, 0, start // LANES + p))

    return pl.pallas_call(
        _dil_kernel,
        grid=(b, npair),
        in_specs=[col(COL_DIL_QK), col(COL_DIL_QK + DIL_WIDTH), col(COL_DIL_V)],
        out_specs=pl.BlockSpec((None, t, LANES), lambda bi, p: (bi, 0, p)),
        out_shape=jax.ShapeDtypeStruct((b, t, DIL_WIDTH), F32),
        name="dilated_attention",
        scratch_shapes=[pltpu.VMEM((t, LANES), F32)] * 3
        + [pltpu.VMEM((nbr, t + DIL_BLOCK, LANES), BF16)] * 3
        + [pltpu.VMEM((nbr, t, LANES), F32)] * 3,
        compiler_params=_cparams(("parallel", "parallel")),
    )(main3, main3, main3)


def _merge_kernel(x_ref, gf_ref, gd_ref, gm_ref, of_ref, od_ref, om_ref,
                  wf_ref, wd_ref, wm_ref, wo_ref, o_ref):
    merged = (jax.nn.sigmoid(gf_ref[...].astype(F32)) * _dot(of_ref[...], wf_ref[...])
              + jax.nn.sigmoid(gd_ref[...].astype(F32)) * _dot(od_ref[...].astype(BF16), wd_ref[...])
              + jax.nn.sigmoid(gm_ref[...].astype(F32)) * _dot(om_ref[...], wm_ref[...]))
    o_ref[...] = x_ref[...] + _dot(merged.astype(BF16), wo_ref[...])


def _merge(x2, main2, o_fox, o_dil, o_mla, layer, wf, wd, wm, wo, tm=512):
    n = x2.shape[0]

    def tok(width, col=0):
        return pl.BlockSpec((tm, width), lambda i: (i, col))

    def full(w):
        return pl.BlockSpec((None,) + w.shape[1:], lambda i: (layer, 0, 0))

    return pl.pallas_call(
        _merge_kernel,
        grid=(n // tm,),
        in_specs=[tok(D_MODEL), tok(D_MODEL, 0), tok(D_MODEL, 1), tok(D_MODEL, 2),
                  tok(FOX_WIDTH), tok(DIL_WIDTH), tok(MLA_WIDTH),
                  full(wf), full(wd), full(wm), full(wo)],
        out_specs=tok(D_MODEL),
        out_shape=jax.ShapeDtypeStruct((n, D_MODEL), F32),
        name="merge_outproj",
        compiler_params=_cparams(("parallel",)),
    )(x2, main2, main2, main2, o_fox, o_dil, o_mla, wf, wd, wm, wo)


def _ffn_kernel(x_ref, g_ref, wg_ref, wu_ref, wd_ref, p_ref, pg_ref, wpg_ref, wpp_ref, o_ref):
    x = x_ref[...]
    h = _rms(x, g_ref[...]).astype(BF16)
    gate = _dot(h, wg_ref[...])
    up = _dot(h, wu_ref[...])
    act = gate * jax.nn.sigmoid(gate) * up
    x = x + _dot(act.astype(BF16), wd_ref[...])
    hp = _rms(x, pg_ref[...]).astype(BF16)
    gate_p = jax.nn.sigmoid(_dot(hp, wpg_ref[...]))
    o_ref[...] = x + gate_p * _dot(p_ref[...].astype(BF16), wpp_ref[...])


def _ffn_ple(x2, gain, j, wg, wu, wd, layer, p3, ple_gain, wpg, wpp, tm=256):
    n = x2.shape[0]
    tok = pl.BlockSpec((tm, D_MODEL), lambda i: (i, 0))
    vec = pl.BlockSpec((1, D_MODEL), lambda i: (0, 0))

    def of(index, w):
        return pl.BlockSpec((None,) + w.shape[1:], lambda i: (index, 0, 0))

    return pl.pallas_call(
        _ffn_kernel,
        grid=(n // tm,),
        in_specs=[tok, vec, of(j, wg), of(j, wu), of(j, wd),
                  pl.BlockSpec((None, tm, PLE_DIM), lambda i: (layer, i, 0)), vec,
                  of(layer, wpg), of(layer, wpp)],
        out_specs=tok,
        out_shape=jax.ShapeDtypeStruct((n, D_MODEL), F32),
        name="ffn_dense_ple",
        compiler_params=pltpu.CompilerParams(dimension_semantics=("parallel",),
                                             vmem_limit_bytes=VMEM_LIMIT_LARGE),
    )(x2, gain, wg, wu, wd, p3, ple_gain, wpg, wpp)


MOE_BLOCK = 2048
MOE_TILE = 256
MOE_TILES = -(-(2 * MOE_BLOCK + N_EXPERTS * (MOE_TILE - 1)) // MOE_TILE)
MOE_META_ROWS = 32
assert MOE_TILES <= MOE_META_ROWS
MOE_CHUNK = 256
MOE_CHUNKS = MOE_BLOCK // MOE_CHUNK
MOE_WINDOW = 5


def _router_kernel(x_ref, g_ref, wh_ref, wm_ref, wl_ref, h_ref, rc_ref, rr_ref, meta_ref):
    h = _rms(x_ref[...], g_ref[...])
    h_ref[...] = h.astype(BF16)
    h_hi, h_mid, h_lo = _split3(h)
    w_hi, w_mid, w_lo = wh_ref[...], wm_ref[...], wl_ref[...]
    logits = (_dot(h_hi, w_hi) + (_dot(h_hi, w_mid) + _dot(h_mid, w_hi))
              + (_dot(h_hi, w_lo) + _dot(h_mid, w_mid) + _dot(h_lo, w_hi)))
    lane = _lane_iota()
    is_expert = lane < N_EXPERTS
    logits = jnp.where(is_expert, logits, NEG_INF)
    v1 = jnp.max(logits, axis=1, keepdims=True)
    i1 = jnp.min(jnp.where(logits == v1, lane, LANES), axis=1, keepdims=True)
    first = lane == i1
    rest = jnp.where(first, NEG_INF, logits)
    v2 = jnp.max(rest, axis=1, keepdims=True)
    i2 = jnp.min(jnp.where(rest == v2, lane, LANES), axis=1, keepdims=True)
    second = lane == i2
    e2 = jnp.exp(v2 - v1)
    w1 = 1.0 / (1.0 + e2)
    w2 = e2 / (1.0 + e2)

    sel = jnp.where(first, 1.0, jnp.where(second, 1.0, 0.0))
    row = lax.broadcasted_iota(jnp.int32, (CUMSUM_BLOCK, CUMSUM_BLOCK), 0)
    col = lax.broadcasted_iota(jnp.int32, (CUMSUM_BLOCK, CUMSUM_BLOCK), 1)
    tri = jnp.where(col < row, 1.0, 0.0).astype(BF16)
    carry = jnp.zeros((1, LANES), F32)
    ranks = []
    for blk in range(sel.shape[0] // CUMSUM_BLOCK):
        part = sel[blk * CUMSUM_BLOCK:(blk + 1) * CUMSUM_BLOCK]
        ranks.append(_dot(tri, part.astype(BF16)) + carry)
        carry = carry + jnp.sum(part, axis=0, keepdims=True)
    rank = jnp.concatenate(ranks, axis=0)
    padded = jnp.ceil(carry / MOE_TILE) * MOE_TILE
    er = lax.broadcasted_iota(jnp.int32, (LANES, LANES), 0)
    ec = lax.broadcasted_iota(jnp.int32, (LANES, LANES), 1)
    before = jnp.where(er < ec, 1.0, 0.0).astype(BF16)
    start = _dot(jnp.broadcast_to(padded, (8, LANES)).astype(BF16), before)[0:1]
    slot = start + rank
    dest1 = jnp.sum(jnp.where(first, slot, 0.0), axis=1, keepdims=True)
    dest2 = jnp.sum(jnp.where(second, slot, 0.0), axis=1, keepdims=True)
    routing = jnp.where(lane == 0, dest1, jnp.where(lane == 1, dest2,
                        jnp.where(lane == 2, w1, jnp.where(lane == 3, w2, 0.0))))
    rc_ref[...] = routing
    rrow = routing.T[0:8]
    for c in range(MOE_BLOCK // MOE_CHUNK):
        rr_ref[c] = rrow[:, c * MOE_CHUNK:(c + 1) * MOE_CHUNK]

    end = start + padded
    tile_row = lax.broadcasted_iota(jnp.int32, (MOE_META_ROWS, LANES), 0).astype(F32) * MOE_TILE
    passed = jnp.sum(jnp.where(is_expert, jnp.where(tile_row >= end, 1.0, 0.0), 0.0),
                     axis=1, keepdims=True)
    last = jnp.max(jnp.where(is_expert, jnp.where(padded > 0, lane.astype(F32), 0.0), 0.0),
                   axis=1, keepdims=True)
    total = jnp.sum(jnp.where(is_expert, padded, 0.0), axis=1, keepdims=True)
    active = jnp.where(tile_row < total, 1.0, 0.0)
    nblock = rrow.shape[1]
    tok = lax.broadcasted_iota(jnp.int32, (MOE_META_ROWS, nblock), 1).astype(F32)
    tile_id = lax.broadcasted_iota(jnp.int32, (MOE_META_ROWS, nblock), 0).astype(F32)
    in_tile = jnp.where(jnp.floor(rrow[0:1] / MOE_TILE) == tile_id, 1.0,
                        jnp.where(jnp.floor(rrow[1:2] / MOE_TILE) == tile_id, 1.0, 0.0))
    first_tok = jnp.min(jnp.where(in_tile > 0, tok, float(nblock)), axis=1, keepdims=True)
    last_tok = jnp.max(jnp.where(in_tile > 0, tok, -1.0), axis=1, keepdims=True)
    chunk_lo = jnp.floor(first_tok / MOE_CHUNK)
    chunk_hi = jnp.floor(last_tok / MOE_CHUNK)
    meta = jnp.where(lane == 0, jnp.minimum(passed, last),
                     jnp.where(lane == 1, active,
                               jnp.where(lane == 2, chunk_lo, jnp.where(lane == 3, chunk_hi, 0.0))))
    meta_ref[...] = meta.astype(jnp.int32)


def _router(x2, gain, w_router):
    n = x2.shape[0]
    nblk = n // MOE_BLOCK
    w_pad = jnp.zeros((D_MODEL, LANES), F32).at[:, :N_EXPERTS].set(w_router)
    w_hi = w_pad.astype(BF16)
    r1 = w_pad - w_hi.astype(F32)
    w_mid = r1.astype(BF16)
    w_lo = (r1 - w_mid.astype(F32)).astype(BF16)
    wspec = pl.BlockSpec((D_MODEL, LANES), lambda i: (0, 0))
    return pl.pallas_call(
        _router_kernel,
        grid=(nblk,),
        in_specs=[pl.BlockSpec((MOE_BLOCK, D_MODEL), lambda i: (i, 0)),
                  pl.BlockSpec((1, D_MODEL), lambda i: (0, 0)), wspec, wspec, wspec],
        out_specs=[pl.BlockSpec((MOE_BLOCK, D_MODEL), lambda i: (i, 0)),
                   pl.BlockSpec((MOE_BLOCK, LANES), lambda i: (i, 0)),
                   pl.BlockSpec((None, MOE_CHUNKS, 8, MOE_CHUNK), lambda i: (i, 0, 0, 0)),
                   pl.BlockSpec((None, MOE_META_ROWS, LANES), lambda i: (i, 0, 0))],
        out_shape=[jax.ShapeDtypeStruct((n, D_MODEL), BF16),
                   jax.ShapeDtypeStruct((n, LANES), F32),
                   jax.ShapeDtypeStruct((nblk, MOE_CHUNKS, 8, MOE_CHUNK), F32),
                   jax.ShapeDtypeStruct((nblk, MOE_META_ROWS, LANES), jnp.int32)],
        name="router",
        compiler_params=_cparams(("parallel",)),
    )(x2, gain, w_hi, w_mid, w_lo)


def _moe_kernel(te_ref, ta_ref, lo_ref, hi_ref, h_ref, rrow_ref, rcol_ref, wg_ref, wu_ref, wd_ref,
                o_ref):
    b = pl.program_id(0)
    r = pl.program_id(1)

    @pl.when(r == 0)
    def _():
        o_ref[...] = jnp.zeros(o_ref.shape, F32)

    def tile_work(first_chunk, nchunks):
        base = (r * MOE_TILE).astype(F32)
        slot_col = lax.broadcasted_iota(jnp.int32, (MOE_TILE, 1), 0).astype(F32) + base
        slot_row = lax.broadcasted_iota(jnp.int32, (1, MOE_TILE), 1).astype(F32) + base
        routing = rrow_ref[pl.ds(first_chunk, nchunks)]
        onehots = []
        weight = jnp.zeros((MOE_TILE, 1), F32)
        for c in range(nchunks):
            hit1 = routing[c, 0:1, :] == slot_col
            hit2 = routing[c, 1:2, :] == slot_col
            onehots.append(jnp.where(hit1, 1.0, jnp.where(hit2, 1.0, 0.0)).astype(BF16))
            weight = weight + jnp.sum(
                jnp.where(hit1, routing[c, 2:3, :], jnp.where(hit2, routing[c, 3:4, :], 0.0)),
                axis=1, keepdims=True)
        toks = pl.ds(pl.multiple_of(first_chunk * MOE_CHUNK, MOE_CHUNK), nchunks * MOE_CHUNK)
        xg = _dot(jnp.concatenate(onehots, axis=1), h_ref[toks, :]).astype(BF16)
        gate = _dot(xg, wg_ref[...])
        up = _dot(xg, wu_ref[...])
        act = gate * jax.nn.sigmoid(gate) * up
        y = (_dot(act.astype(BF16), wd_ref[...]) * weight).astype(BF16)
        back = jnp.where(rcol_ref[toks, 0:1] == slot_row, 1.0,
                         jnp.where(rcol_ref[toks, 1:2] == slot_row, 1.0, 0.0)).astype(BF16)
        o_ref[toks, :] += _dot(back, y)

    active = ta_ref[b, r] == 1
    first = jnp.minimum(lo_ref[b, r], MOE_CHUNKS - MOE_WINDOW)
    short = hi_ref[b, r] < first + MOE_WINDOW

    @pl.when(active & short)
    def _():
        tile_work(first, MOE_WINDOW)

    @pl.when(active & jnp.logical_not(short))
    def _():
        tile_work(0, MOE_CHUNKS)


def _moe(h2, rrow, rcol, meta, j, wg, wu, wd):
    n = h2.shape[0]
    nblk = n // MOE_BLOCK
    tile_tables = [meta[:, :MOE_TILES, k] for k in range(4)]

    def blocked(shape, index):
        return pl.BlockSpec(shape, lambda b, r, te, ta, lo, hi: index(b))

    def wspec(shape):
        return pl.BlockSpec((None,) + shape,
                            lambda b, r, te, ta, lo, hi: (j * N_EXPERTS + te[b, r], 0, 0))

    grid_spec = pltpu.PrefetchScalarGridSpec(
        num_scalar_prefetch=len(tile_tables),
        grid=(nblk, MOE_TILES),
        in_specs=[blocked((MOE_BLOCK, D_MODEL), lambda b: (b, 0)),
                  blocked((None, MOE_CHUNKS, 8, MOE_CHUNK), lambda b: (b, 0, 0, 0)),
                  blocked((MOE_BLOCK, LANES), lambda b: (b, 0)),
                  wspec((D_MODEL, D_FF_EXPERT)), wspec((D_MODEL, D_FF_EXPERT)),
                  wspec((D_FF_EXPERT, D_MODEL))],
        out_specs=blocked((MOE_BLOCK, D_MODEL), lambda b: (b, 0)),
    )
    return pl.pallas_call(
        _moe_kernel,
        grid_spec=grid_spec,
        out_shape=jax.ShapeDtypeStruct((n, D_MODEL), F32),
        name="moe_experts",
        compiler_params=pltpu.CompilerParams(dimension_semantics=("parallel", "arbitrary"),
                                             vmem_limit_bytes=VMEM_LIMIT_LARGE),
    )(*tile_tables, h2, rrow, rcol, wg, wu, wd)


def _moe_layer(x2, gain, w_router, j, wg, wu, wd):
    h2, rcol, rrow, meta = _router(x2, gain, w_router)
    return _moe(h2, rrow, rcol, meta, j, wg, wu, wd)


def _ple_kernel(x_ref, d_ref, p_ref, g_ref, wg_ref, wp_ref, fg_ref, o_ref, *, final):
    x = x_ref[...] + d_ref[...]
    h = _rms(x, g_ref[...]).astype(BF16)
    gate = jax.nn.sigmoid(_dot(h, wg_ref[...]))
    y = x + gate * _dot(p_ref[...].astype(BF16), wp_ref[...])
    if final:
        y = _rms(y, fg_ref[...])
    o_ref[...] = y


def _ple(x2, delta, layer, p3, gain, wg, wp, final_gain, final, tm=512):
    n = x2.shape[0]
    vec = pl.BlockSpec((1, D_MODEL), lambda i: (0, 0))
    tok = pl.BlockSpec((tm, D_MODEL), lambda i: (i, 0))
    return pl.pallas_call(
        functools.partial(_ple_kernel, final=final),
        grid=(n // tm,),
        in_specs=[tok, tok,
                  pl.BlockSpec((None, tm, PLE_DIM), lambda i: (layer, i, 0)), vec,
                  pl.BlockSpec((None,) + wg.shape[1:], lambda i: (layer, 0, 0)),
                  pl.BlockSpec((None,) + wp.shape[1:], lambda i: (layer, 0, 0)), vec],
        out_specs=pl.BlockSpec((tm, D_MODEL), lambda i: (i, 0)),
        out_shape=jax.ShapeDtypeStruct((n, D_MODEL), F32),
        name="ple_final" if final else "ple",
        compiler_params=_cparams(("parallel",)),
    )(x2, delta, p3, gain, wg, wp, final_gain)


def _cast_kernel(x_ref, o_ref):
    o_ref[...] = x_ref[...].astype(BF16)


def _to_bf16(a):
    r, c = a.shape[-2:]
    a3 = a.reshape((-1, r, c))
    spec = pl.BlockSpec((None, r, c), lambda i: (i, 0, 0))
    out = pl.pallas_call(
        _cast_kernel,
        grid=(a3.shape[0],),
        in_specs=[spec],
        out_specs=spec,
        out_shape=jax.ShapeDtypeStruct(a3.shape, BF16),
        name="cast_bf16",
        compiler_params=_cparams(("parallel",)),
    )(a3)
    return out.reshape(a.shape)


IN_SIZES = (FOX_WIDTH, FOX_WIDTH, FOX_WIDTH, FOX_HEADS, DIL_WIDTH, DIL_WIDTH, DIL_WIDTH,
            MLA_Q_RANK, MLA_KV_RANK, MLA_ROPE_DIM, GATE_WIDTH)
IN_WIDTH = sum(IN_SIZES)
IN_PREP_COLS = 256


def _inproj_weight_kernel(w_ref, main_ref, small_ref):
    offs = [0]
    for s in IN_SIZES:
        offs.append(offs[-1] + s)
    fq, fk, fv, fz, dq, dk, dv, cq, ckv, kr, gates = [(offs[i], offs[i + 1])
                                                      for i in range(len(IN_SIZES))]
    scale = HEAD_DIM ** -0.5
    row = 0
    for (lo, hi), mult in ((gates, 1.0), (dq, scale), (dk, 1.0), (fq, scale), (fk, 1.0), (fv, 1.0),
                           (dv, 1.0), (cq, 1.0), (ckv, 1.0)):
        main_ref[row:row + hi - lo, :] = (w_ref[lo:hi, :] * mult).astype(BF16)
        row += hi - lo
    small_ref[...] = jnp.zeros(small_ref.shape, BF16)
    small_ref[0:FOX_HEADS, :] = w_ref[fz[0]:fz[1], :].astype(BF16)
    for h in range(MLA_HEADS):
        r0 = LANES + h * MLA_ROPE_DIM
        small_ref[r0:r0 + MLA_ROPE_DIM, :] = w_ref[kr[0]:kr[1], :].astype(BF16)


def _prep_in_weights(w_in):
    depth = w_in.shape[0]
    cb = IN_PREP_COLS
    w_t = jnp.swapaxes(w_in, 1, 2)

    def blk(rows):
        return pl.BlockSpec((None, rows, cb), lambda l, i: (l, 0, i))

    return pl.pallas_call(
        _inproj_weight_kernel,
        grid=(depth, D_MODEL // cb),
        in_specs=[blk(IN_WIDTH)],
        out_specs=[blk(MAIN_WIDTH), blk(SMALL_WIDTH)],
        out_shape=[jax.ShapeDtypeStruct((depth, MAIN_WIDTH, D_MODEL), BF16),
                   jax.ShapeDtypeStruct((depth, SMALL_WIDTH, D_MODEL), BF16)],
        name="inproj_weights",
        compiler_params=_cparams(("parallel", "parallel")),
    )(w_t)


def _prep_mla_weights(w_uq, w_ukv):
    depth = w_uq.shape[0]
    uq = w_uq.reshape(depth, MLA_Q_RANK, MLA_HEADS, MLA_NOPE_DIM + MLA_ROPE_DIM)
    wq = jnp.concatenate([uq[..., :MLA_NOPE_DIM].reshape(depth, MLA_Q_RANK, -1),
                          uq[..., MLA_NOPE_DIM:].reshape(depth, MLA_Q_RANK, -1)], axis=-1)
    ukv = w_ukv.reshape(depth, MLA_KV_RANK, MLA_HEADS, MLA_NOPE_DIM + MLA_V_DIM)
    wkv = jnp.concatenate([ukv[..., :MLA_NOPE_DIM].reshape(depth, MLA_KV_RANK, -1),
                           ukv[..., MLA_NOPE_DIM:].reshape(depth, MLA_KV_RANK, -1)], axis=-1)
    return wq.astype(BF16), wkv.astype(BF16)


def kernel(x, p, positions, mix_norm, w_in, b_forget, mla_q_norm, mla_kv_norm, w_uq, w_ukv,
           w_br_fox, w_br_dil, w_br_mla, w_out, ffn_norm, w_ffn_gate, w_ffn_up, w_ffn_down,
           w_router, w_exp_gate, w_exp_up, w_exp_down, ple_norm, w_ple_gate, w_ple_proj,
           final_norm):
    b, t, _ = x.shape
    n = b * t
    depth = w_in.shape[0]
    assert depth % 2 == 0

    w_main, w_small = _prep_in_weights(w_in)
    wq_all, wkv_all = _prep_mla_weights(w_uq, w_ukv)
    (w_br_fox, w_br_dil, w_br_mla, w_out, w_ffn_gate, w_ffn_up, w_ffn_down, w_exp_gate, w_exp_up,
     w_exp_down, w_ple_gate, w_ple_proj) = [
        _to_bf16(w) for w in (w_br_fox, w_br_dil, w_br_mla, w_out, w_ffn_gate, w_ffn_up, w_ffn_down,
                              w_exp_gate, w_exp_up, w_exp_down, w_ple_gate, w_ple_proj)]
    w_exp_gate, w_exp_up, w_exp_down = [w.reshape((-1,) + w.shape[2:])
                                        for w in (w_exp_gate, w_exp_up, w_exp_down)]
    p3 = p.reshape(depth, n, PLE_DIM)
    bias_all = jnp.concatenate(
        [b_forget.astype(F32), jnp.zeros((depth, LANES - FOX_HEADS), F32)], axis=-1)
    tabs = [tab.reshape(n, LANES) for tab in _rope_tables(positions)]
    final_gain = final_norm.reshape(1, D_MODEL)

    x2 = x.reshape(n, D_MODEL)
    for i in range(depth):
        main2, fz, qn, qr, kn, kr4, vv = _inproj(
            x2, mix_norm[i].reshape(1, D_MODEL), i, w_main, w_small, tabs,
            mla_q_norm[i].reshape(1, -1), mla_kv_norm[i].reshape(1, -1), wq_all[i], wkv_all[i])
        main3 = main2.reshape(b, t, COL_MLA)

        dcols = _decay(fz.reshape(b, t, LANES), bias_all[i].reshape(1, LANES))
        o_fox = _fox_attention(main3, dcols)
        o_dil = _dil_attention(main3)
        o_mla = _mla_attention(*[a.reshape(b, t, -1) for a in (qn, qr, kn, kr4, vv)])

        x2 = _merge(x2, main2, o_fox.reshape(n, -1), o_dil.reshape(n, -1), o_mla.reshape(n, -1),
                    i, w_br_fox, w_br_dil, w_br_mla, w_out)

        j = i // 2
        gain = ffn_norm[i].reshape(1, D_MODEL)
        ple_args = (i, p3, ple_norm[i].reshape(1, D_MODEL), w_ple_gate, w_ple_proj)
        if i % 2 == 0:
            x2 = _ffn_ple(x2, gain, j, w_ffn_gate, w_ffn_up, w_ffn_down, *ple_args)
        else:
            delta = _moe_layer(x2, gain, w_router[j], j, w_exp_gate, w_exp_up, w_exp_down)
            x2 = _ple(x2, delta, *ple_args, final_gain, final=(i == depth - 1))
    return x2.reshape(b, t, D_MODEL)
```

```python
import functools

import jax
import jax.numpy as jnp
from jax import lax
from jax.experimental import pallas as pl
from jax.experimental.pallas import tpu as pltpu

F32 = jnp.float32
BF16 = jnp.bfloat16

D_MODEL = 1024
HEAD_DIM = 64
ROPE_THETA = 10000.0
NORM_EPS = 1e-6
FOX_HEADS = 6
FOX_WIDTH = FOX_HEADS * HEAD_DIM
DIL_HEADS = 6
DIL_WIDTH = DIL_HEADS * HEAD_DIM
DIL_PAIRS = ((128, 1), (512, 4), (2048, 16))
DIL_BLOCK = 128
MLA_HEADS = 4
MLA_Q_RANK = 256
MLA_KV_RANK = 256
MLA_NOPE_DIM = 64
MLA_ROPE_DIM = 32
MLA_V_DIM = 64
MLA_WIDTH = MLA_HEADS * MLA_V_DIM
N_BRANCHES = 3
D_FF = 2816
N_EXPERTS = 8
D_FF_EXPERT = 1408
PLE_DIM = 256

LANES = 128
GATE_WIDTH = N_BRANCHES * D_MODEL
COL_GATES = 0
COL_DIL_QK = GATE_WIDTH
COL_FOX = COL_DIL_QK + 2 * DIL_WIDTH
COL_DIL_V = COL_FOX + 3 * FOX_WIDTH
COL_MLA = COL_DIL_V + DIL_WIDTH
MAIN_WIDTH = COL_MLA + MLA_Q_RANK + MLA_KV_RANK
MXU_TILE = 256
assert all(c % MXU_TILE == 0 for c in (COL_DIL_QK, COL_FOX, COL_MLA, MAIN_WIDTH))
SMALL_WIDTH = 2 * LANES
VMEM_LIMIT = 56 * 1024 * 1024
VMEM_LIMIT_LARGE = 60 * 1024 * 1024

NEG_INF = float("-inf")


def _cparams(sem):
    return pltpu.CompilerParams(dimension_semantics=sem, vmem_limit_bytes=VMEM_LIMIT)


def _rms(x, gain):
    return x * lax.rsqrt(jnp.mean(x * x, axis=-1, keepdims=True) + NORM_EPS) * gain


def _dot(a, b):
    return jnp.dot(a, b, preferred_element_type=F32)


def _dot_nt(a, b):
    return lax.dot_general(a, b, (((1,), (1,)), ((), ())), preferred_element_type=F32)


def _lane_iota():
    return lax.broadcasted_iota(jnp.int32, (1, LANES), 1)


def _rope_tab_kernel(pos_ref, fd_ref, fm_ref, cd_ref, sd_ref, cm_ref, sm_ref):
    pos = pos_ref[...].astype(F32)
    lane = _lane_iota()
    ang = pos * fd_ref[...]
    cd_ref[...] = jnp.cos(ang)
    sd_ref[...] = jnp.sin(ang) * jnp.where((lane % HEAD_DIM) < HEAD_DIM // 2, -1.0, 1.0)
    ang = pos * fm_ref[...]
    cm_ref[...] = jnp.cos(ang)
    sm_ref[...] = jnp.sin(ang) * jnp.where((lane % MLA_ROPE_DIM) < MLA_ROPE_DIM // 2, -1.0, 1.0)


def _rope_tables(positions):
    b, t = positions.shape
    half_d = HEAD_DIM // 2
    half_m = MLA_ROPE_DIM // 2
    inv_d = ROPE_THETA ** (-jnp.arange(half_d, dtype=F32) / half_d)
    inv_m = ROPE_THETA ** (-jnp.arange(half_m, dtype=F32) / half_m)
    fd = jnp.tile(inv_d, LANES // half_d)[None, :]
    fm = jnp.tile(inv_m, LANES // half_m)[None, :]
    tab = jax.ShapeDtypeStruct((b, t, LANES), F32)
    tab_spec = pl.BlockSpec((None, t, LANES), lambda i: (i, 0, 0))
    vec_spec = pl.BlockSpec((1, LANES), lambda i: (0, 0))
    return pl.pallas_call(
        _rope_tab_kernel,
        grid=(b,),
        in_specs=[pl.BlockSpec((None, t, 1), lambda i: (i, 0, 0)), vec_spec, vec_spec],
        out_specs=[tab_spec] * 4,
        out_shape=[tab] * 4,
        name="rope_tables",
        compiler_params=_cparams(("parallel",)),
    )(positions.reshape(b, t, 1), fd, fm)


def _swap_halves(x, width):
    lane = _lane_iota()
    half = width // 2
    return jnp.where((lane % width) < half,
                     pltpu.roll(x, LANES - half, axis=1),
                     pltpu.roll(x, half, axis=1))


def _rope(x, cos, sin_signed, width):
    return x * cos + _swap_halves(x, width) * sin_signed


def _col_chunks(width, step=512):
    return [(c, min(step, width - c)) for c in range(0, width, step)]


MLA_NOPE_WIDTH = MLA_HEADS * MLA_NOPE_DIM
MLA_SCALE = (MLA_NOPE_DIM + MLA_ROPE_DIM) ** -0.5


def _inproj_kernel(x_ref, g_ref, w_ref, ws_ref, cd_ref, sd_ref, cm_ref, sm_ref,
                   gq_ref, gkv_ref, wq_ref, wkv_ref,
                   o_ref, fz_ref, qn_ref, qr_ref, kn_ref, kr_ref, v_ref):
    h = _rms(x_ref[...], g_ref[...]).astype(BF16)
    for lo, hi in ((0, COL_DIL_QK), (COL_FOX, COL_MLA)):
        for c, n in _col_chunks(hi - lo):
            o_ref[:, lo + c:lo + c + n] = _dot_nt(h, w_ref[lo + c:lo + c + n, :]).astype(BF16)
    cos_d = cd_ref[...]
    sin_d = sd_ref[...]
    for c0 in range(COL_DIL_QK, COL_FOX, MXU_TILE):
        y = _dot_nt(h, w_ref[c0:c0 + MXU_TILE, :])
        for c in range(0, MXU_TILE, LANES):
            o_ref[:, c0 + c:c0 + c + LANES] = _rope(y[:, c:c + LANES], cos_d, sin_d,
                                                    HEAD_DIM).astype(BF16)

    cos_m = cm_ref[...]
    sin_m = sm_ref[...]
    lat = _dot_nt(h, w_ref[COL_MLA:MAIN_WIDTH, :])
    hq = _rms(lat[:, :MLA_Q_RANK], gq_ref[...]).astype(BF16)
    q = _dot(hq, wq_ref[...])
    qn_ref[...] = (q[:, :MLA_NOPE_WIDTH] * MLA_SCALE).astype(BF16)
    qr_ref[...] = (_rope(q[:, MLA_NOPE_WIDTH:], cos_m, sin_m, MLA_ROPE_DIM) * MLA_SCALE).astype(BF16)
    hkv = _rms(lat[:, MLA_Q_RANK:], gkv_ref[...]).astype(BF16)
    kv = _dot(hkv, wkv_ref[...])
    kn_ref[...] = kv[:, :MLA_NOPE_WIDTH].astype(BF16)
    v_ref[...] = kv[:, MLA_NOPE_WIDTH:].astype(BF16)

    small = _dot_nt(h, ws_ref[...])
    fz_ref[...] = small[:, :LANES]
    kr_ref[...] = _rope(small[:, LANES:], cos_m, sin_m, MLA_ROPE_DIM).astype(BF16)


def _inproj(x2, gain, layer, w_main_t, w_small_t, tabs, gq, gkv, wq, wkv, tm=512):
    n = x2.shape[0]

    def tok(width):
        return pl.BlockSpec((tm, width), lambda i: (i, 0))

    def full(a):
        return pl.BlockSpec(a.shape, lambda i: (0, 0))

    def of_layer(a):
        return pl.BlockSpec((None,) + a.shape[1:], lambda i: (layer, 0, 0))

    widths = (COL_MLA, LANES, MLA_NOPE_WIDTH, LANES, MLA_NOPE_WIDTH, LANES, MLA_WIDTH)
    dtypes = (BF16, F32, BF16, BF16, BF16, BF16, BF16)
    return pl.pallas_call(
        _inproj_kernel,
        grid=(n // tm,),
        in_specs=[tok(D_MODEL), full(gain), of_layer(w_main_t), of_layer(w_small_t)]
        + [tok(LANES)] * 4 + [full(gq), full(gkv), full(wq), full(wkv)],
        out_specs=[tok(w) for w in widths],
        out_shape=[jax.ShapeDtypeStruct((n, w), d) for w, d in zip(widths, dtypes)],
        name="inproj",
        compiler_params=_cparams(("parallel",)),
    )(x2, gain, w_main_t, w_small_t, *tabs, gq, gkv, wq, wkv)


CUMSUM_BLOCK = 256
DECAY_PART_STRIDE = 8
DECAY_PARTS = 3


def _split3(a):
    hi = a.astype(BF16)
    r1 = a - hi.astype(F32)
    mid = r1.astype(BF16)
    lo = (r1 - mid.astype(F32)).astype(BF16)
    return hi, mid, lo


def _decay_kernel(z_ref, b_ref, o_ref):
    t = z_ref.shape[0]
    row = lax.broadcasted_iota(jnp.int32, (CUMSUM_BLOCK, CUMSUM_BLOCK), 0)
    col = lax.broadcasted_iota(jnp.int32, (CUMSUM_BLOCK, CUMSUM_BLOCK), 1)
    tri = jnp.where(col <= row, 1.0, 0.0).astype(BF16)
    is_head = _lane_iota() < FOX_HEADS
    carry = jnp.zeros((1, LANES), F32)
    for blk in range(t // CUMSUM_BLOCK):
        sl = slice(blk * CUMSUM_BLOCK, (blk + 1) * CUMSUM_BLOCK)
        z = z_ref[sl, :] + b_ref[...]
        logf = -(jnp.maximum(-z, 0.0) + jnp.log1p(jnp.exp(-jnp.abs(z))))
        hi, mid, lo = _split3(logf)
        cs = _dot(tri, hi) + _dot(tri, mid) + _dot(tri, lo) + carry
        carry = cs[CUMSUM_BLOCK - 1:CUMSUM_BLOCK, :]
        parts = _split3(jnp.where(is_head, cs, 0.0))
        packed = parts[0].astype(F32)
        for j in range(1, DECAY_PARTS):
            packed = packed + pltpu.roll(parts[j].astype(F32), j * DECAY_PART_STRIDE, axis=1)
        o_ref[sl, :] = packed.astype(BF16)


def _decay(small3, bias):
    b, t, _ = small3.shape
    return pl.pallas_call(
        _decay_kernel,
        grid=(b,),
        in_specs=[pl.BlockSpec((None, t, LANES), lambda i: (i, 0, 0)),
                  pl.BlockSpec((1, LANES), lambda i: (0, 0))],
        out_specs=pl.BlockSpec((None, t, LANES), lambda i: (i, 0, 0)),
        out_shape=jax.ShapeDtypeStruct((b, t, LANES), BF16),
        name="decay",
        compiler_params=_cparams(("parallel",)),
    )(small3, bias)


def _pair_softmax(s, m_prev, ok_cols):
    nc = s.shape[1] // LANES
    cols = [s[:, c * LANES:(c + 1) * LANES] for c in range(nc)]
    if ok_cols is not None:
        cols = [jnp.where(ok, col, NEG_INF) for ok, col in zip(ok_cols, cols)]
    cmax = cols[0]
    for col in cols[1:]:
        cmax = jnp.maximum(cmax, col)
    m_new = jnp.maximum(m_prev, jnp.max(cmax, axis=1, keepdims=True))
    alpha = jnp.exp(m_prev - m_new)
    p = jnp.concatenate([jnp.exp(col - m_new).astype(BF16) for col in cols], axis=1)
    return m_new, alpha, p


def _causal_pair_attention(qs, keys_rows, vals_rows, qi, o_ref, tq):
    lane = _lane_iota()
    head0 = lane < HEAD_DIM

    def scores(j):
        return _dot_nt(qs, keys_rows(j * tq, (j + 1) * tq))

    def weighted_values(p, j):
        v = vals_rows(j * tq, (j + 1) * tq)
        one = jnp.ones_like(v)
        pv0 = _dot(p[:tq], jnp.where(head0, v, one))
        pv1 = _dot(p[tq:], jnp.where(head0, one, v))
        return jnp.concatenate([pv0, pv1], axis=0)

    s = scores(0)
    m_prev = jnp.full((2 * tq, LANES), NEG_INF, F32)
    acc = jnp.zeros((2 * tq, LANES), F32)
    for j in range(qi):
        s_next = scores(j + 1)
        m_prev, alpha, p = _pair_softmax(s, m_prev, None)
        acc = alpha * acc + weighted_values(p, j)
        s = s_next

    r_i = lax.broadcasted_iota(jnp.int32, (tq, LANES), 0)
    c_i = lax.broadcasted_iota(jnp.int32, (tq, LANES), 1)
    ok_cols = []
    for c in range(tq // LANES):
        ok = c_i + c * LANES <= r_i
        ok_cols.append(jnp.concatenate([ok, ok], axis=0))
    _, alpha, p = _pair_softmax(s, m_prev, ok_cols)
    acc = alpha * acc + weighted_values(p, qi)
    out = acc / pltpu.roll(acc, HEAD_DIM, axis=1)
    o_ref[...] = jnp.where(head0, out[:tq], out[tq:]).astype(o_ref.dtype)


def _fox_kernel(q_ref, k_ref, v_ref, d_ref, o_ref, *, tq):
    p = pl.program_id(1)
    lane = _lane_iota()

    def keys_at(lo, hi):
        return jnp.concatenate([k_ref[lo:hi, :], d_ref[lo:hi, :]], axis=1)

    def vals_at(lo, hi):
        return v_ref[lo:hi, :]

    for qi in range(q_ref.shape[0] // tq):
        q = q_ref[qi * tq:(qi + 1) * tq, :]
        zero = jnp.zeros_like(q)
        halves = []
        for h in range(2):
            head = 2 * p + h
            pick = (lane % DECAY_PART_STRIDE == head) & (lane < DECAY_PARTS * DECAY_PART_STRIDE)
            neg = jnp.broadcast_to(jnp.where(pick, -1.0, 0.0).astype(BF16), q.shape)
            q_h = jnp.where((lane // HEAD_DIM) == h, q, zero)
            halves.append(jnp.concatenate([q_h, neg], axis=1))
        qs = jnp.concatenate(halves, axis=0)
        _causal_pair_attention(qs, keys_at, vals_at, qi, o_ref.at[qi * tq:(qi + 1) * tq, :], tq)


def _fox_attention(main3, dcols, tq=512):
    b, t, _ = main3.shape
    cb = COL_FOX // LANES
    npair = FOX_HEADS // 2

    def seq(col):
        return pl.BlockSpec((None, t, LANES), lambda bi, p: (bi, 0, col(p)))

    return pl.pallas_call(
        functools.partial(_fox_kernel, tq=tq),
        grid=(b, npair),
        in_specs=[
            seq(lambda p: cb + p),
            seq(lambda p: cb + npair + p),
            seq(lambda p: cb + 2 * npair + p),
            seq(lambda p: 0),
        ],
        out_specs=pl.BlockSpec((None, t, LANES), lambda bi, p: (bi, 0, p)),
        out_shape=jax.ShapeDtypeStruct((b, t, FOX_WIDTH), BF16),
        name="fox_attention",
        compiler_params=_cparams(("parallel", "parallel")),
    )(main3, main3, main3, dcols)


def _mla_kernel(qn_ref, qr_ref, kn_ref, kr_ref, v_ref, o_ref, *, tq):
    p = pl.program_id(1)
    lane = _lane_iota()

    def keys_at(lo, hi):
        return jnp.concatenate([kn_ref[lo:hi, :], kr_ref[lo:hi, :]], axis=1)

    def vals_at(lo, hi):
        return v_ref[lo:hi, :]

    for qi in range(qn_ref.shape[0] // tq):
        qn = qn_ref[qi * tq:(qi + 1) * tq, :]
        qr = qr_ref[qi * tq:(qi + 1) * tq, :]
        zero = jnp.zeros_like(qn)
        halves = []
        for h in range(2):
            nope_h = jnp.where((lane // MLA_NOPE_DIM) == h, qn, zero)
            rope_h = jnp.where((lane // MLA_ROPE_DIM) == 2 * p + h, qr, zero)
            halves.append(jnp.concatenate([nope_h, rope_h], axis=1))
        qs = jnp.concatenate(halves, axis=0)
        _causal_pair_attention(qs, keys_at, vals_at, qi, o_ref.at[qi * tq:(qi + 1) * tq, :], tq)


def _mla_attention(qn, qr, kn, kr4, vv, tq=512):
    b, t, _ = qn.shape
    npair = MLA_HEADS // 2
    pair = pl.BlockSpec((None, t, LANES), lambda bi, p: (bi, 0, p))
    shared = pl.BlockSpec((None, t, LANES), lambda bi, p: (bi, 0, 0))
    return pl.pallas_call(
        functools.partial(_mla_kernel, tq=tq),
        grid=(b, npair),
        in_specs=[pair, shared, pair, shared, pair],
        out_specs=pair,
        out_shape=jax.ShapeDtypeStruct((b, t, MLA_WIDTH), BF16),
        name="mla_attention",
        compiler_params=_cparams(("parallel", "parallel")),
    )(qn, qr, kn, kr4, vv)


DIL_UNROLL = 8


def _dil_kernel(q_ref, k_ref, v_ref, o_ref, qf, kf, vf, qb, kb, vb, accb, mb, lb):
    t = q_ref.shape[0]
    blk = DIL_BLOCK
    lane = _lane_iota()
    head0 = lane < HEAD_DIM
    qf[...] = q_ref[...].astype(F32)
    kf[...] = k_ref[...].astype(F32)
    vf[...] = v_ref[...].astype(F32)

    for g, (_, rate) in enumerate(DIL_PAIRS):
        length = t // rate
        kb[g, 0:blk, :] = jnp.zeros((blk, LANES), BF16)
        vb[g, 0:blk, :] = jnp.zeros((blk, LANES), BF16)
        if rate == 1:
            qb[g, blk:, :] = q_ref[...]
            kb[g, blk:, :] = k_ref[...]
            vb[g, blk:, :] = v_ref[...]
            continue
        for res in range(rate):
            dst = slice(blk + res * length, blk + (res + 1) * length)
            src = pl.ds(res, length, stride=rate)
            qb[g, dst, :] = qf[src, :].astype(BF16)
            kb[g, dst, :] = kf[src, :].astype(BF16)
            vb[g, dst, :] = vf[src, :].astype(BF16)

    qi2 = lax.broadcasted_iota(jnp.int32, (blk, 2 * blk), 0)
    kj2 = lax.broadcasted_iota(jnp.int32, (blk, 2 * blk), 1)
    band = (kj2 >= qi2) & (kj2 <= qi2 + blk)
    bias_full = jnp.where(band, 0.0, NEG_INF)
    bias_first = jnp.where(band & (kj2 >= blk), 0.0, NEG_INF)

    for g, (window, rate) in enumerate(DIL_PAIRS):
        assert window // rate == blk
        nb = t // rate // blk
        nblocks = t // blk

        def body(j, carry, g=g, nb=nb):
            base = pl.multiple_of(j * blk, blk)
            q = qb[g, pl.ds(base + blk, blk), :]
            zero = jnp.zeros_like(q)
            qs = jnp.concatenate([jnp.where(head0, q, zero), jnp.where(head0, zero, q)], axis=0)
            keys = kb[g, pl.ds(base, 2 * blk), :]
            vals = vb[g, pl.ds(base, 2 * blk), :]
            bias = bias_first if nb == 1 else jnp.where((j % nb) != 0, bias_full, bias_first)
            s = _dot_nt(qs, keys)
            ps, ms = [], []
            for h in range(2):
                sh = s[h * blk:(h + 1) * blk] + bias
                m = jnp.max(sh, axis=1, keepdims=True)
                ps.append(jnp.exp(sh - m).astype(BF16))
                ms.append(m)
            pv = _dot(jnp.concatenate(ps, axis=0),
                      jnp.concatenate([vals, jnp.ones_like(vals)], axis=1))
            rows = pl.ds(base, blk)
            accb[g, rows, :] = jnp.where(head0, pv[:blk, :LANES], pv[blk:, :LANES])
            mb[g, rows, :] = jnp.where(head0, ms[0], ms[1])
            lb[g, rows, :] = jnp.where(head0, pv[:blk, LANES:], pv[blk:, LANES:])
            return carry

        lax.fori_loop(0, nblocks, body, 0, unroll=DIL_UNROLL)

    rate_max = max(rate for _, rate in DIL_PAIRS)
    length = t // rate_max
    for res in range(rate_max):
        m_all, l_all, a_all = [], [], []
        for g, (_, rate) in enumerate(DIL_PAIRS):
            start = (res % rate) * (t // rate) + res // rate
            step = rate_max // rate
            rows = pl.ds(start, length) if step == 1 else pl.ds(start, length, stride=step)
            m_all.append(mb[g, rows, :])
            l_all.append(lb[g, rows, :])
            a_all.append(accb[g, rows, :])
        m_max = jnp.maximum(jnp.maximum(m_all[0], m_all[1]), m_all[2])
        ws = [jnp.exp(m - m_max) for m in m_all]
        num = ws[0] * a_all[0] + ws[1] * a_all[1] + ws[2] * a_all[2]
        den = ws[0] * l_all[0] + ws[1] * l_all[1] + ws[2] * l_all[2]
        o_ref[pl.ds(res, length, stride=rate_max), :] = num / den


def _dil_attention(main3):
    b, t, _ = main3.shape
    assert all(t % (rate * DIL_BLOCK) == 0 for _, rate in DIL_PAIRS)
    npair = DIL_HEADS // 2
    nbr = len(DIL_PAIRS)

    def col(start):
        return pl.BlockSpec((None, t, LANES), lambda bi, p: (bi, 0, start // LANES + p))

    return pl.pallas_call(
        _dil_kernel,
        grid=(b, npair),
        in_specs=[col(COL_DIL_QK), col(COL_DIL_QK + DIL_WIDTH), col(COL_DIL_V)],
        out_specs=pl.BlockSpec((None, t, LANES), lambda bi, p: (bi, 0, p)),
        out_shape=jax.ShapeDtypeStruct((b, t, DIL_WIDTH), F32),
        name="dilated_attention",
        scratch_shapes=[pltpu.VMEM((t, LANES), F32)] * 3
        + [pltpu.VMEM((nbr, t + DIL_BLOCK, LANES), BF16)] * 3
        + [pltpu.VMEM((nbr, t, LANES), F32)] * 3,
        compiler_params=_cparams(("parallel", "parallel")),
    )(main3, main3, main3)


def _merge_kernel(x_ref, gf_ref, gd_ref, gm_ref, of_ref, od_ref, om_ref,
                  wf_ref, wd_ref, wm_ref, wo_ref, o_ref):
    merged = (jax.nn.sigmoid(gf_ref[...].astype(F32)) * _dot(of_ref[...], wf_ref[...])
              + jax.nn.sigmoid(gd_ref[...].astype(F32)) * _dot(od_ref[...].astype(BF16), wd_ref[...])
              + jax.nn.sigmoid(gm_ref[...].astype(F32)) * _dot(om_ref[...], wm_ref[...]))
    o_ref[...] = x_ref[...] + _dot(merged.astype(BF16), wo_ref[...])


def _merge(x2, main2, o_fox, o_dil, o_mla, layer, wf, wd, wm, wo, tm=512):
    n = x2.shape[0]

    def tok(width, col=0):
        return pl.BlockSpec((tm, width), lambda i: (i, col))

    def full(w):
        return pl.BlockSpec((None,) + w.shape[1:], lambda i: (layer, 0, 0))

    return pl.pallas_call(
        _merge_kernel,
        grid=(n // tm,),
        in_specs=[tok(D_MODEL), tok(D_MODEL, 0), tok(D_MODEL, 1), tok(D_MODEL, 2),
                  tok(FOX_WIDTH), tok(DIL_WIDTH), tok(MLA_WIDTH),
                  full(wf), full(wd), full(wm), full(wo)],
        out_specs=tok(D_MODEL),
        out_shape=jax.ShapeDtypeStruct((n, D_MODEL), F32),
        name="merge_outproj",
        compiler_params=_cparams(("parallel",)),
    )(x2, main2, main2, main2, o_fox, o_dil, o_mla, wf, wd, wm, wo)


def _ffn_kernel(x_ref, g_ref, wg_ref, wu_ref, wd_ref, p_ref, pg_ref, wpg_ref, wpp_ref, o_ref):
    x = x_ref[...]
    h = _rms(x, g_ref[...]).astype(BF16)
    gate = _dot(h, wg_ref[...])
    up = _dot(h, wu_ref[...])
    act = gate * jax.nn.sigmoid(gate) * up
    x = x + _dot(act.astype(BF16), wd_ref[...])
    hp = _rms(x, pg_ref[...]).astype(BF16)
    gate_p = jax.nn.sigmoid(_dot(hp, wpg_ref[...]))
    o_ref[...] = x + gate_p * _dot(p_ref[...].astype(BF16), wpp_ref[...])


def _ffn_ple(x2, gain, j, wg, wu, wd, layer, p3, ple_gain, wpg, wpp, tm=256):
    n = x2.shape[0]
    tok = pl.BlockSpec((tm, D_MODEL), lambda i: (i, 0))
    vec = pl.BlockSpec((1, D_MODEL), lambda i: (0, 0))

    def of(index, w):
        return pl.BlockSpec((None,) + w.shape[1:], lambda i: (index, 0, 0))

    return pl.pallas_call(
        _ffn_kernel,
        grid=(n // tm,),
        in_specs=[tok, vec, of(j, wg), of(j, wu), of(j, wd),
                  pl.BlockSpec((None, tm, PLE_DIM), lambda i: (layer, i, 0)), vec,
                  of(layer, wpg), of(layer, wpp)],
        out_specs=tok,
        out_shape=jax.ShapeDtypeStruct((n, D_MODEL), F32),
        name="ffn_dense_ple",
        compiler_params=pltpu.CompilerParams(dimension_semantics=("parallel",),
                                             vmem_limit_bytes=VMEM_LIMIT_LARGE),
    )(x2, gain, wg, wu, wd, p3, ple_gain, wpg, wpp)


MOE_BLOCK = 2048
MOE_TILE = 256
MOE_TILES = -(-(2 * MOE_BLOCK + N_EXPERTS * (MOE_TILE - 1)) // MOE_TILE)
MOE_META_ROWS = 32
assert MOE_TILES <= MOE_META_ROWS
MOE_CHUNK = 256
MOE_CHUNKS = MOE_BLOCK // MOE_CHUNK
MOE_WINDOW = 5


def _router_kernel(x_ref, g_ref, wc_ref, h_ref, rc_ref, rr_ref, meta_ref):
    h = _rms(x_ref[...], g_ref[...])
    h_ref[...] = h.astype(BF16)
    h_hi, h_mid, h_lo = _split3(h)
    a = _dot(h_hi, wc_ref[...])
    b = _dot(h_mid, wc_ref[:, :2 * LANES])
    c = _dot(h_lo, wc_ref[:, :LANES])
    logits = (a[:, :LANES] + (a[:, LANES:2 * LANES] + b[:, :LANES])
              + (a[:, 2 * LANES:] + b[:, LANES:] + c))
    lane = _lane_iota()
    is_expert = lane < N_EXPERTS
    logits = jnp.where(is_expert, logits, NEG_INF)
    v1 = jnp.max(logits, axis=1, keepdims=True)
    i1 = jnp.min(jnp.where(logits == v1, lane, LANES), axis=1, keepdims=True)
    first = lane == i1
    rest = jnp.where(first, NEG_INF, logits)
    v2 = jnp.max(rest, axis=1, keepdims=True)
    i2 = jnp.min(jnp.where(rest == v2, lane, LANES), axis=1, keepdims=True)
    second = lane == i2
    e2 = jnp.exp(v2 - v1)
    w1 = 1.0 / (1.0 + e2)
    w2 = e2 / (1.0 + e2)

    sel = jnp.where(first, 1.0, jnp.where(second, 1.0, 0.0))
    row = lax.broadcasted_iota(jnp.int32, (CUMSUM_BLOCK, CUMSUM_BLOCK), 0)
    col = lax.broadcasted_iota(jnp.int32, (CUMSUM_BLOCK, CUMSUM_BLOCK), 1)
    tri = jnp.where(col < row, 1.0, 0.0).astype(BF16)
    carry = jnp.zeros((1, LANES), F32)
    ranks = []
    for blk in range(sel.shape[0] // CUMSUM_BLOCK):
        part = sel[blk * CUMSUM_BLOCK:(blk + 1) * CUMSUM_BLOCK]
        ranks.append(_dot(tri, part.astype(BF16)) + carry)
        carry = carry + jnp.sum(part, axis=0, keepdims=True)
    rank = jnp.concatenate(ranks, axis=0)
    padded = jnp.ceil(carry / MOE_TILE) * MOE_TILE
    er = lax.broadcasted_iota(jnp.int32, (LANES, LANES), 0)
    ec = lax.broadcasted_iota(jnp.int32, (LANES, LANES), 1)
    before = jnp.where(er < ec, 1.0, 0.0).astype(BF16)
    start = _dot(jnp.broadcast_to(padded, (8, LANES)).astype(BF16), before)[0:1]
    slot = start + rank
    dest1 = jnp.sum(jnp.where(first, slot, 0.0), axis=1, keepdims=True)
    dest2 = jnp.sum(jnp.where(second, slot, 0.0), axis=1, keepdims=True)
    routing = jnp.where(lane == 0, dest1, jnp.where(lane == 1, dest2,
                        jnp.where(lane == 2, w1, jnp.where(lane == 3, w2, 0.0))))
    rc_ref[...] = routing
    rrow = routing.T[0:8]
    for c in range(MOE_BLOCK // MOE_CHUNK):
        rr_ref[c] = rrow[:, c * MOE_CHUNK:(c + 1) * MOE_CHUNK]

    end = start + padded
    tile_row = lax.broadcasted_iota(jnp.int32, (MOE_META_ROWS, LANES), 0).astype(F32) * MOE_TILE
    passed = jnp.sum(jnp.where(is_expert, jnp.where(tile_row >= end, 1.0, 0.0), 0.0),
                     axis=1, keepdims=True)
    last = jnp.max(jnp.where(is_expert, jnp.where(padded > 0, lane.astype(F32), 0.0), 0.0),
                   axis=1, keepdims=True)
    total = jnp.sum(jnp.where(is_expert, padded, 0.0), axis=1, keepdims=True)
    active = jnp.where(tile_row < total, 1.0, 0.0)
    nblock = rrow.shape[1]
    tok = lax.broadcasted_iota(jnp.int32, (MOE_META_ROWS, nblock), 1).astype(F32)
    tile_id = lax.broadcasted_iota(jnp.int32, (MOE_META_ROWS, nblock), 0).astype(F32)
    in_tile = jnp.where(jnp.floor(rrow[0:1] / MOE_TILE) == tile_id, 1.0,
                        jnp.where(jnp.floor(rrow[1:2] / MOE_TILE) == tile_id, 1.0, 0.0))
    first_tok = jnp.min(jnp.where(in_tile > 0, tok, float(nblock)), axis=1, keepdims=True)
    last_tok = jnp.max(jnp.where(in_tile > 0, tok, -1.0), axis=1, keepdims=True)
    chunk_lo = jnp.floor(first_tok / MOE_CHUNK)
    chunk_hi = jnp.floor(last_tok / MOE_CHUNK)
    meta = jnp.where(lane == 0, jnp.minimum(passed, last),
                     jnp.where(lane == 1, active,
                               jnp.where(lane == 2, chunk_lo, jnp.where(lane == 3, chunk_hi, 0.0))))
    meta_ref[...] = meta.astype(jnp.int32)


def _router(x2, gain, w_router):
    n = x2.shape[0]
    nblk = n // MOE_BLOCK
    w_pad = jnp.zeros((D_MODEL, LANES), F32).at[:, :N_EXPERTS].set(w_router)
    w_hi = w_pad.astype(BF16)
    r1 = w_pad - w_hi.astype(F32)
    w_mid = r1.astype(BF16)
    w_lo = (r1 - w_mid.astype(F32)).astype(BF16)
    w_cat = jnp.concatenate([w_hi, w_mid, w_lo], axis=1)
    return pl.pallas_call(
        _router_kernel,
        grid=(nblk,),
        in_specs=[pl.BlockSpec((MOE_BLOCK, D_MODEL), lambda i: (i, 0)),
                  pl.BlockSpec((1, D_MODEL), lambda i: (0, 0)),
                  pl.BlockSpec(w_cat.shape, lambda i: (0, 0))],
        out_specs=[pl.BlockSpec((MOE_BLOCK, D_MODEL), lambda i: (i, 0)),
                   pl.BlockSpec((MOE_BLOCK, LANES), lambda i: (i, 0)),
                   pl.BlockSpec((None, MOE_CHUNKS, 8, MOE_CHUNK), lambda i: (i, 0, 0, 0)),
                   pl.BlockSpec((None, MOE_META_ROWS, LANES), lambda i: (i, 0, 0))],
        out_shape=[jax.ShapeDtypeStruct((n, D_MODEL), BF16),
                   jax.ShapeDtypeStruct((n, LANES), F32),
                   jax.ShapeDtypeStruct((nblk, MOE_CHUNKS, 8, MOE_CHUNK), F32),
                   jax.ShapeDtypeStruct((nblk, MOE_META_ROWS, LANES), jnp.int32)],
        name="router",
        compiler_params=_cparams(("parallel",)),
    )(x2, gain, w_cat)


def _moe_kernel(te_ref, ta_ref, lo_ref, hi_ref, h_ref, rrow_ref, rcol_ref, wgu_ref, wd_ref, o_ref):
    b = pl.program_id(0)
    r = pl.program_id(1)

    @pl.when(r == 0)
    def _():
        o_ref[...] = jnp.zeros(o_ref.shape, F32)

    def tile_work(first_chunk, nchunks):
        base = (r * MOE_TILE).astype(F32)
        slot_col = lax.broadcasted_iota(jnp.int32, (MOE_TILE, 1), 0).astype(F32) + base
        slot_row = lax.broadcasted_iota(jnp.int32, (1, MOE_TILE), 1).astype(F32) + base
        routing = rrow_ref[pl.ds(first_chunk, nchunks)]
        onehots = []
        weight = jnp.zeros((MOE_TILE, 1), F32)
        for c in range(nchunks):
            hit1 = routing[c, 0:1, :] == slot_col
            hit2 = routing[c, 1:2, :] == slot_col
            onehots.append(jnp.where(hit1, 1.0, jnp.where(hit2, 1.0, 0.0)).astype(BF16))
            weight = weight + jnp.sum(
                jnp.where(hit1, routing[c, 2:3, :], jnp.where(hit2, routing[c, 3:4, :], 0.0)),
                axis=1, keepdims=True)
        toks = pl.ds(pl.multiple_of(first_chunk * MOE_CHUNK, MOE_CHUNK), nchunks * MOE_CHUNK)
        xg = _dot(jnp.concatenate(onehots, axis=1), h_ref[toks, :]).astype(BF16)
        gate_up = _dot(xg, wgu_ref[...])
        gate = gate_up[:, :D_FF_EXPERT]
        up = gate_up[:, D_FF_EXPERT:]
        act = gate * jax.nn.sigmoid(gate) * up
        y = (_dot(act.astype(BF16), wd_ref[...]) * weight).astype(BF16)
        back = jnp.where(rcol_ref[toks, 0:1] == slot_row, 1.0,
                         jnp.where(rcol_ref[toks, 1:2] == slot_row, 1.0, 0.0)).astype(BF16)
        o_ref[toks, :] += _dot(back, y)

    active = ta_ref[b, r] == 1
    first = jnp.minimum(lo_ref[b, r], MOE_CHUNKS - MOE_WINDOW)
    short = hi_ref[b, r] < first + MOE_WINDOW

    @pl.when(active & short)
    def _():
        tile_work(first, MOE_WINDOW)

    @pl.when(active & jnp.logical_not(short))
    def _():
        tile_work(0, MOE_CHUNKS)


def _moe(h2, rrow, rcol, meta, j, wgu, wd):
    n = h2.shape[0]
    nblk = n // MOE_BLOCK
    tile_tables = [meta[:, :MOE_TILES, k] for k in range(4)]

    def blocked(shape, index):
        return pl.BlockSpec(shape, lambda b, r, te, ta, lo, hi: index(b))

    def wspec(shape):
        return pl.BlockSpec((None,) + shape,
                            lambda b, r, te, ta, lo, hi: (j * N_EXPERTS + te[b, r], 0, 0))

    grid_spec = pltpu.PrefetchScalarGridSpec(
        num_scalar_prefetch=len(tile_tables),
        grid=(nblk, MOE_TILES),
        in_specs=[blocked((MOE_BLOCK, D_MODEL), lambda b: (b, 0)),
                  blocked((None, MOE_CHUNKS, 8, MOE_CHUNK), lambda b: (b, 0, 0, 0)),
                  blocked((MOE_BLOCK, LANES), lambda b: (b, 0)),
                  wspec((D_MODEL, 2 * D_FF_EXPERT)), wspec((D_FF_EXPERT, D_MODEL))],
        out_specs=blocked((MOE_BLOCK, D_MODEL), lambda b: (b, 0)),
    )
    return pl.pallas_call(
        _moe_kernel,
        grid_spec=grid_spec,
        out_shape=jax.ShapeDtypeStruct((n, D_MODEL), F32),
        name="moe_experts",
        compiler_params=pltpu.CompilerParams(dimension_semantics=("parallel", "arbitrary"),
                                             vmem_limit_bytes=VMEM_LIMIT_LARGE),
    )(*tile_tables, h2, rrow, rcol, wgu, wd)


def _moe_layer(x2, gain, w_router, j, wgu, wd):
    h2, rcol, rrow, meta = _router(x2, gain, w_router)
    return _moe(h2, rrow, rcol, meta, j, wgu, wd)


def _ple_kernel(x_ref, d_ref, p_ref, g_ref, wg_ref, wp_ref, fg_ref, o_ref, *, final):
    x = x_ref[...] + d_ref[...]
    h = _rms(x, g_ref[...]).astype(BF16)
    gate = jax.nn.sigmoid(_dot(h, wg_ref[...]))
    y = x + gate * _dot(p_ref[...].astype(BF16), wp_ref[...])
    if final:
        y = _rms(y, fg_ref[...])
    o_ref[...] = y


def _ple(x2, delta, layer, p3, gain, wg, wp, final_gain, final, tm=1024):
    n = x2.shape[0]
    vec = pl.BlockSpec((1, D_MODEL), lambda i: (0, 0))
    tok = pl.BlockSpec((tm, D_MODEL), lambda i: (i, 0))
    return pl.pallas_call(
        functools.partial(_ple_kernel, final=final),
        grid=(n // tm,),
        in_specs=[tok, tok,
                  pl.BlockSpec((None, tm, PLE_DIM), lambda i: (layer, i, 0)), vec,
                  pl.BlockSpec((None,) + wg.shape[1:], lambda i: (layer, 0, 0)),
                  pl.BlockSpec((None,) + wp.shape[1:], lambda i: (layer, 0, 0)), vec],
        out_specs=pl.BlockSpec((tm, D_MODEL), lambda i: (i, 0)),
        out_shape=jax.ShapeDtypeStruct((n, D_MODEL), F32),
        name="ple_final" if final else "ple",
        compiler_params=_cparams(("parallel",)),
    )(x2, delta, p3, gain, wg, wp, final_gain)


def _cast_kernel(x_ref, o_ref):
    o_ref[...] = x_ref[...].astype(BF16)


def _to_bf16(a):
    r, c = a.shape[-2:]
    a3 = a.reshape((-1, r, c))
    spec = pl.BlockSpec((None, r, c), lambda i: (i, 0, 0))
    out = pl.pallas_call(
        _cast_kernel,
        grid=(a3.shape[0],),
        in_specs=[spec],
        out_specs=spec,
        out_shape=jax.ShapeDtypeStruct(a3.shape, BF16),
        name="cast_bf16",
        compiler_params=_cparams(("parallel",)),
    )(a3)
    return out.reshape(a.shape)


def _cast_pair_kernel(a_ref, b_ref, o_ref):
    width = a_ref.shape[-1]
    o_ref[:, :width] = a_ref[...].astype(BF16)
    o_ref[:, width:] = b_ref[...].astype(BF16)


def _to_bf16_side_by_side(a, b):
    r, c = a.shape[-2:]
    a3 = a.reshape((-1, r, c))
    b3 = b.reshape((-1, r, c))
    spec = pl.BlockSpec((None, r, c), lambda i: (i, 0, 0))
    return pl.pallas_call(
        _cast_pair_kernel,
        grid=(a3.shape[0],),
        in_specs=[spec, spec],
        out_specs=pl.BlockSpec((None, r, 2 * c), lambda i: (i, 0, 0)),
        out_shape=jax.ShapeDtypeStruct((a3.shape[0], r, 2 * c), BF16),
        name="cast_pair_bf16",
        compiler_params=_cparams(("parallel",)),
    )(a3, b3)


IN_SIZES = (FOX_WIDTH, FOX_WIDTH, FOX_WIDTH, FOX_HEADS, DIL_WIDTH, DIL_WIDTH, DIL_WIDTH,
            MLA_Q_RANK, MLA_KV_RANK, MLA_ROPE_DIM, GATE_WIDTH)
IN_WIDTH = sum(IN_SIZES)
IN_PREP_COLS = 256


def _inproj_weight_kernel(w_ref, main_ref, small_ref):
    offs = [0]
    for s in IN_SIZES:
        offs.append(offs[-1] + s)
    fq, fk, fv, fz, dq, dk, dv, cq, ckv, kr, gates = [(offs[i], offs[i + 1])
                                                      for i in range(len(IN_SIZES))]
    scale = HEAD_DIM ** -0.5
    row = 0
    for (lo, hi), mult in ((gates, 1.0), (dq, scale), (dk, 1.0), (fq, scale), (fk, 1.0), (fv, 1.0),
                           (dv, 1.0), (cq, 1.0), (ckv, 1.0)):
        main_ref[row:row + hi - lo, :] = (w_ref[lo:hi, :] * mult).astype(BF16)
        row += hi - lo
    small_ref[...] = jnp.zeros(small_ref.shape, BF16)
    small_ref[0:FOX_HEADS, :] = w_ref[fz[0]:fz[1], :].astype(BF16)
    for h in range(MLA_HEADS):
        r0 = LANES + h * MLA_ROPE_DIM
        small_ref[r0:r0 + MLA_ROPE_DIM, :] = w_ref[kr[0]:kr[1], :].astype(BF16)


def _prep_in_weights(w_in):
    depth = w_in.shape[0]
    cb = IN_PREP_COLS
    w_t = jnp.swapaxes(w_in, 1, 2)

    def blk(rows):
        return pl.BlockSpec((None, rows, cb), lambda l, i: (l, 0, i))

    return pl.pallas_call(
        _inproj_weight_kernel,
        grid=(depth, D_MODEL // cb),
        in_specs=[blk(IN_WIDTH)],
        out_specs=[blk(MAIN_WIDTH), blk(SMALL_WIDTH)],
        out_shape=[jax.ShapeDtypeStruct((depth, MAIN_WIDTH, D_MODEL), BF16),
                   jax.ShapeDtypeStruct((depth, SMALL_WIDTH, D_MODEL), BF16)],
        name="inproj_weights",
        compiler_params=_cparams(("parallel", "parallel")),
    )(w_t)


def _prep_mla_weights(w_uq, w_ukv):
    depth = w_uq.shape[0]
    uq = w_uq.reshape(depth, MLA_Q_RANK, MLA_HEADS, MLA_NOPE_DIM + MLA_ROPE_DIM)
    wq = jnp.concatenate([uq[..., :MLA_NOPE_DIM].reshape(depth, MLA_Q_RANK, -1),
                          uq[..., MLA_NOPE_DIM:].reshape(depth, MLA_Q_RANK, -1)], axis=-1)
    ukv = w_ukv.reshape(depth, MLA_KV_RANK, MLA_HEADS, MLA_NOPE_DIM + MLA_V_DIM)
    wkv = jnp.concatenate([ukv[..., :MLA_NOPE_DIM].reshape(depth, MLA_KV_RANK, -1),
                           ukv[..., MLA_NOPE_DIM:].reshape(depth, MLA_KV_RANK, -1)], axis=-1)
    return wq.astype(BF16), wkv.astype(BF16)


def kernel(x, p, positions, mix_norm, w_in, b_forget, mla_q_norm, mla_kv_norm, w_uq, w_ukv,
           w_br_fox, w_br_dil, w_br_mla, w_out, ffn_norm, w_ffn_gate, w_ffn_up, w_ffn_down,
           w_router, w_exp_gate, w_exp_up, w_exp_down, ple_norm, w_ple_gate, w_ple_proj,
           final_norm):
    b, t, _ = x.shape
    n = b * t
    depth = w_in.shape[0]
    assert depth % 2 == 0

    w_main, w_small = _prep_in_weights(w_in)
    wq_all, wkv_all = _prep_mla_weights(w_uq, w_ukv)
    (w_br_fox, w_br_dil, w_br_mla, w_out, w_ffn_gate, w_ffn_up, w_ffn_down, w_exp_down, w_ple_gate,
     w_ple_proj) = [
        _to_bf16(w) for w in (w_br_fox, w_br_dil, w_br_mla, w_out, w_ffn_gate, w_ffn_up, w_ffn_down,
                              w_exp_down, w_ple_gate, w_ple_proj)]
    w_exp_gate_up = _to_bf16_side_by_side(w_exp_gate, w_exp_up)
    w_exp_down = w_exp_down.reshape((-1,) + w_exp_down.shape[2:])
    p3 = p.reshape(depth, n, PLE_DIM)
    bias_all = jnp.concatenate(
        [b_forget.astype(F32), jnp.zeros((depth, LANES - FOX_HEADS), F32)], axis=-1)
    tabs = [tab.reshape(n, LANES) for tab in _rope_tables(positions)]
    final_gain = final_norm.reshape(1, D_MODEL)

    x2 = x.reshape(n, D_MODEL)
    for i in range(depth):
        main2, fz, qn, qr, kn, kr4, vv = _inproj(
            x2, mix_norm[i].reshape(1, D_MODEL), i, w_main, w_small, tabs,
            mla_q_norm[i].reshape(1, -1), mla_kv_norm[i].reshape(1, -1), wq_all[i], wkv_all[i])
        main3 = main2.reshape(b, t, COL_MLA)

        dcols = _decay(fz.reshape(b, t, LANES), bias_all[i].reshape(1, LANES))
        o_fox = _fox_attention(main3, dcols)
        o_dil = _dil_attention(main3)
        o_mla = _mla_attention(*[a.reshape(b, t, -1) for a in (qn, qr, kn, kr4, vv)])

        x2 = _merge(x2, main2, o_fox.reshape(n, -1), o_dil.reshape(n, -1), o_mla.reshape(n, -1),
                    i, w_br_fox, w_br_dil, w_br_mla, w_out)

        j = i // 2
        gain = ffn_norm[i].reshape(1, D_MODEL)
        ple_args = (i, p3, ple_norm[i].reshape(1, D_MODEL), w_ple_gate, w_ple_proj)
        if i % 2 == 0:
            x2 = _ffn_ple(x2, gain, j, w_ffn_gate, w_ffn_up, w_ffn_down, *ple_args)
        else:
            delta = _moe_layer(x2, gain, w_router[j], j, w_exp_gate_up, w_exp_down)
            x2 = _ple(x2, delta, *ple_args, final_gain, final=(i == depth - 1))
    return x2.reshape(b, t, D_MODEL)
```

```python
import functools

import jax
import jax.numpy as jnp
from jax import lax
from jax.experimental import pallas as pl
from jax.experimental.pallas import tpu as pltpu

F32 = jnp.float32
BF16 = jnp.bfloat16

D_MODEL = 1024
HEAD_DIM = 64
ROPE_THETA = 10000.0
NORM_EPS = 1e-6
FOX_HEADS = 6
FOX_WIDTH = FOX_HEADS * HEAD_DIM
DIL_HEADS = 6
DIL_WIDTH = DIL_HEADS * HEAD_DIM
DIL_PAIRS = ((128, 1), (512, 4), (2048, 16))
DIL_BLOCK = 128
MLA_HEADS = 4
MLA_Q_RANK = 256
MLA_KV_RANK = 256
MLA_NOPE_DIM = 64
MLA_ROPE_DIM = 32
MLA_V_DIM = 64
MLA_WIDTH = MLA_HEADS * MLA_V_DIM
N_BRANCHES = 3
D_FF = 2816
N_EXPERTS = 8
D_FF_EXPERT = 1408
PLE_DIM = 256

LANES = 128
GATE_WIDTH = N_BRANCHES * D_MODEL
COL_GATES = 0
COL_DIL_QK = GATE_WIDTH
COL_FOX = COL_DIL_QK + 2 * DIL_WIDTH
COL_DIL_V = COL_FOX + 3 * FOX_WIDTH
COL_MLA = COL_DIL_V + DIL_WIDTH
MAIN_WIDTH = COL_MLA + MLA_Q_RANK + MLA_KV_RANK
MXU_TILE = 256
assert all(c % MXU_TILE == 0 for c in (COL_DIL_QK, COL_FOX, COL_MLA, MAIN_WIDTH))
SMALL_WIDTH = 2 * LANES
VMEM_LIMIT = 56 * 1024 * 1024
VMEM_LIMIT_LARGE = 60 * 1024 * 1024

NEG_INF = float("-inf")


def _cparams(sem):
    return pltpu.CompilerParams(dimension_semantics=sem, vmem_limit_bytes=VMEM_LIMIT)


def _rms(x, gain):
    return x * lax.rsqrt(jnp.mean(x * x, axis=-1, keepdims=True) + NORM_EPS) * gain


def _dot(a, b):
    return jnp.dot(a, b, preferred_element_type=F32)


def _dot_nt(a, b):
    return lax.dot_general(a, b, (((1,), (1,)), ((), ())), preferred_element_type=F32)


def _lane_iota():
    return lax.broadcasted_iota(jnp.int32, (1, LANES), 1)


def _rope_tab_kernel(pos_ref, fd_ref, fm_ref, cd_ref, sd_ref, cm_ref, sm_ref):
    pos = pos_ref[...].astype(F32)
    lane = _lane_iota()
    ang = pos * fd_ref[...]
    cd_ref[...] = jnp.cos(ang)
    sd_ref[...] = jnp.sin(ang) * jnp.where((lane % HEAD_DIM) < HEAD_DIM // 2, -1.0, 1.0)
    ang = pos * fm_ref[...]
    cm_ref[...] = jnp.cos(ang)
    sm_ref[...] = jnp.sin(ang) * jnp.where((lane % MLA_ROPE_DIM) < MLA_ROPE_DIM // 2, -1.0, 1.0)


def _rope_tables(positions):
    b, t = positions.shape
    half_d = HEAD_DIM // 2
    half_m = MLA_ROPE_DIM // 2
    inv_d = ROPE_THETA ** (-jnp.arange(half_d, dtype=F32) / half_d)
    inv_m = ROPE_THETA ** (-jnp.arange(half_m, dtype=F32) / half_m)
    fd = jnp.tile(inv_d, LANES // half_d)[None, :]
    fm = jnp.tile(inv_m, LANES // half_m)[None, :]
    tab = jax.ShapeDtypeStruct((b, t, LANES), F32)
    tab_spec = pl.BlockSpec((None, t, LANES), lambda i: (i, 0, 0))
    vec_spec = pl.BlockSpec((1, LANES), lambda i: (0, 0))
    return pl.pallas_call(
        _rope_tab_kernel,
        grid=(b,),
        in_specs=[pl.BlockSpec((None, t, 1), lambda i: (i, 0, 0)), vec_spec, vec_spec],
        out_specs=[tab_spec] * 4,
        out_shape=[tab] * 4,
        name="rope_tables",
        compiler_params=_cparams(("parallel",)),
    )(positions.reshape(b, t, 1), fd, fm)


def _swap_halves(x, width):
    lane = _lane_iota()
    half = width // 2
    return jnp.where((lane % width) < half,
                     pltpu.roll(x, LANES - half, axis=1),
                     pltpu.roll(x, half, axis=1))


def _rope(x, cos, sin_signed, width):
    return x * cos + _swap_halves(x, width) * sin_signed


def _col_chunks(width, step=512):
    return [(c, min(step, width - c)) for c in range(0, width, step)]


MLA_NOPE_WIDTH = MLA_HEADS * MLA_NOPE_DIM
MLA_SCALE = (MLA_NOPE_DIM + MLA_ROPE_DIM) ** -0.5


def _inproj_kernel(x_ref, g_ref, w_ref, ws_ref, cd_ref, sd_ref, cm_ref, sm_ref,
                   gq_ref, gkv_ref, wq_ref, wkv_ref,
                   o_ref, fz_ref, qn_ref, qr_ref, kn_ref, kr_ref, v_ref):
    h = _rms(x_ref[...], g_ref[...]).astype(BF16)
    for lo, hi in ((0, COL_DIL_QK), (COL_FOX, COL_MLA)):
        for c, n in _col_chunks(hi - lo):
            o_ref[:, lo + c:lo + c + n] = _dot_nt(h, w_ref[lo + c:lo + c + n, :]).astype(BF16)
    cos_d = cd_ref[...]
    sin_d = sd_ref[...]
    for c0 in range(COL_DIL_QK, COL_FOX, MXU_TILE):
        y = _dot_nt(h, w_ref[c0:c0 + MXU_TILE, :])
        for c in range(0, MXU_TILE, LANES):
            o_ref[:, c0 + c:c0 + c + LANES] = _rope(y[:, c:c + LANES], cos_d, sin_d,
                                                    HEAD_DIM).astype(BF16)

    cos_m = cm_ref[...]
    sin_m = sm_ref[...]
    lat = _dot_nt(h, w_ref[COL_MLA:MAIN_WIDTH, :])
    hq = _rms(lat[:, :MLA_Q_RANK], gq_ref[...]).astype(BF16)
    q = _dot(hq, wq_ref[...])
    qn_ref[...] = (q[:, :MLA_NOPE_WIDTH] * MLA_SCALE).astype(BF16)
    qr_ref[...] = (_rope(q[:, MLA_NOPE_WIDTH:], cos_m, sin_m, MLA_ROPE_DIM) * MLA_SCALE).astype(BF16)
    hkv = _rms(lat[:, MLA_Q_RANK:], gkv_ref[...]).astype(BF16)
    kv = _dot(hkv, wkv_ref[...])
    kn_ref[...] = kv[:, :MLA_NOPE_WIDTH].astype(BF16)
    v_ref[...] = kv[:, MLA_NOPE_WIDTH:].astype(BF16)

    small = _dot_nt(h, ws_ref[...])
    fz_ref[...] = small[:, :LANES]
    kr_ref[...] = _rope(small[:, LANES:], cos_m, sin_m, MLA_ROPE_DIM).astype(BF16)


def _inproj(x2, gain, layer, w_main_t, w_small_t, tabs, gq, gkv, wq, wkv, tm=512):
    n = x2.shape[0]

    def tok(width):
        return pl.BlockSpec((tm, width), lambda i: (i, 0))

    def full(a):
        return pl.BlockSpec(a.shape, lambda i: (0, 0))

    def of_layer(a):
        return pl.BlockSpec((None,) + a.shape[1:], lambda i: (layer, 0, 0))

    widths = (COL_MLA, LANES, MLA_NOPE_WIDTH, LANES, MLA_NOPE_WIDTH, LANES, MLA_WIDTH)
    dtypes = (BF16, F32, BF16, BF16, BF16, BF16, BF16)
    return pl.pallas_call(
        _inproj_kernel,
        grid=(n // tm,),
        in_specs=[tok(D_MODEL), full(gain), of_layer(w_main_t), of_layer(w_small_t)]
        + [tok(LANES)] * 4 + [full(gq), full(gkv), full(wq), full(wkv)],
        out_specs=[tok(w) for w in widths],
        out_shape=[jax.ShapeDtypeStruct((n, w), d) for w, d in zip(widths, dtypes)],
        name="inproj",
        compiler_params=_cparams(("parallel",)),
    )(x2, gain, w_main_t, w_small_t, *tabs, gq, gkv, wq, wkv)


CUMSUM_BLOCK = 256
DECAY_PART_STRIDE = 8
DECAY_PARTS = 3


def _split3(a):
    hi = a.astype(BF16)
    r1 = a - hi.astype(F32)
    mid = r1.astype(BF16)
    lo = (r1 - mid.astype(F32)).astype(BF16)
    return hi, mid, lo


def _decay_kernel(z_ref, b_ref, o_ref):
    t = z_ref.shape[0]
    row = lax.broadcasted_iota(jnp.int32, (CUMSUM_BLOCK, CUMSUM_BLOCK), 0)
    col = lax.broadcasted_iota(jnp.int32, (CUMSUM_BLOCK, CUMSUM_BLOCK), 1)
    tri = jnp.where(col <= row, 1.0, 0.0).astype(BF16)
    is_head = _lane_iota() < FOX_HEADS
    carry = jnp.zeros((1, LANES), F32)
    for blk in range(t // CUMSUM_BLOCK):
        sl = slice(blk * CUMSUM_BLOCK, (blk + 1) * CUMSUM_BLOCK)
        z = z_ref[sl, :] + b_ref[...]
        logf = -(jnp.maximum(-z, 0.0) + jnp.log1p(jnp.exp(-jnp.abs(z))))
        hi, mid, lo = _split3(logf)
        cs = _dot(tri, hi) + _dot(tri, mid) + _dot(tri, lo) + carry
        carry = cs[CUMSUM_BLOCK - 1:CUMSUM_BLOCK, :]
        parts = _split3(jnp.where(is_head, cs, 0.0))
        packed = parts[0].astype(F32)
        for j in range(1, DECAY_PARTS):
            packed = packed + pltpu.roll(parts[j].astype(F32), j * DECAY_PART_STRIDE, axis=1)
        o_ref[sl, :] = packed.astype(BF16)


def _decay(small3, bias):
    b, t, _ = small3.shape
    return pl.pallas_call(
        _decay_kernel,
        grid=(b,),
        in_specs=[pl.BlockSpec((None, t, LANES), lambda i: (i, 0, 0)),
                  pl.BlockSpec((1, LANES), lambda i: (0, 0))],
        out_specs=pl.BlockSpec((None, t, LANES), lambda i: (i, 0, 0)),
        out_shape=jax.ShapeDtypeStruct((b, t, LANES), BF16),
        name="decay",
        compiler_params=_cparams(("parallel",)),
    )(small3, bias)


def _pair_softmax(s, m_prev, ok_cols):
    nc = s.shape[1] // LANES
    cols = [s[:, c * LANES:(c + 1) * LANES] for c in range(nc)]
    if ok_cols is not None:
        cols = [jnp.where(ok, col, NEG_INF) for ok, col in zip(ok_cols, cols)]
    cmax = cols[0]
    for col in cols[1:]:
        cmax = jnp.maximum(cmax, col)
    m_new = jnp.maximum(m_prev, jnp.max(cmax, axis=1, keepdims=True))
    alpha = jnp.exp(m_prev - m_new)
    p = jnp.concatenate([jnp.exp((col - m_new).astype(BF16)) for col in cols], axis=1)
    return m_new, alpha, p


def _causal_pair_attention(qs, keys_rows, vals_rows, qi, o_ref, tq):
    lane = _lane_iota()
    head0 = lane < HEAD_DIM

    def scores(j):
        return _dot_nt(qs, keys_rows(j * tq, (j + 1) * tq))

    def weighted_values(p, j):
        v = vals_rows(j * tq, (j + 1) * tq)
        one = jnp.ones_like(v)
        pv0 = _dot(p[:tq], jnp.where(head0, v, one))
        pv1 = _dot(p[tq:], jnp.where(head0, one, v))
        return jnp.concatenate([pv0, pv1], axis=0)

    s = scores(0)
    m_prev = jnp.full((2 * tq, LANES), NEG_INF, F32)
    acc = jnp.zeros((2 * tq, LANES), F32)
    for j in range(qi):
        s_next = scores(j + 1)
        m_prev, alpha, p = _pair_softmax(s, m_prev, None)
        acc = alpha * acc + weighted_values(p, j)
        s = s_next

    r_i = lax.broadcasted_iota(jnp.int32, (tq, LANES), 0)
    c_i = lax.broadcasted_iota(jnp.int32, (tq, LANES), 1)
    ok_cols = []
    for c in range(tq // LANES):
        ok = c_i + c * LANES <= r_i
        ok_cols.append(jnp.concatenate([ok, ok], axis=0))
    _, alpha, p = _pair_softmax(s, m_prev, ok_cols)
    acc = alpha * acc + weighted_values(p, qi)
    out = acc / pltpu.roll(acc, HEAD_DIM, axis=1)
    o_ref[...] = jnp.where(head0, out[:tq], out[tq:]).astype(o_ref.dtype)


def _fox_kernel(q_ref, k_ref, v_ref, d_ref, o_ref, *, tq):
    p = pl.program_id(1)
    lane = _lane_iota()

    def keys_at(lo, hi):
        return jnp.concatenate([k_ref[lo:hi, :], d_ref[lo:hi, :]], axis=1)

    def vals_at(lo, hi):
        return v_ref[lo:hi, :]

    for qi in range(q_ref.shape[0] // tq):
        q = q_ref[qi * tq:(qi + 1) * tq, :]
        zero = jnp.zeros_like(q)
        halves = []
        for h in range(2):
            head = 2 * p + h
            pick = (lane % DECAY_PART_STRIDE == head) & (lane < DECAY_PARTS * DECAY_PART_STRIDE)
            neg = jnp.broadcast_to(jnp.where(pick, -1.0, 0.0).astype(BF16), q.shape)
            q_h = jnp.where((lane // HEAD_DIM) == h, q, zero)
            halves.append(jnp.concatenate([q_h, neg], axis=1))
        qs = jnp.concatenate(halves, axis=0)
        _causal_pair_attention(qs, keys_at, vals_at, qi, o_ref.at[qi * tq:(qi + 1) * tq, :], tq)


def _fox_attention(main3, dcols, tq=512):
    b, t, _ = main3.shape
    cb = COL_FOX // LANES
    npair = FOX_HEADS // 2

    def seq(col):
        return pl.BlockSpec((None, t, LANES), lambda bi, p: (bi, 0, col(p)))

    return pl.pallas_call(
        functools.partial(_fox_kernel, tq=tq),
        grid=(b, npair),
        in_specs=[
            seq(lambda p: cb + p),
            seq(lambda p: cb + npair + p),
            seq(lambda p: cb + 2 * npair + p),
            seq(lambda p: 0),
        ],
        out_specs=pl.BlockSpec((None, t, LANES), lambda bi, p: (bi, 0, p)),
        out_shape=jax.ShapeDtypeStruct((b, t, FOX_WIDTH), BF16),
        name="fox_attention",
        compiler_params=_cparams(("parallel", "parallel")),
    )(main3, main3, main3, dcols)


def _mla_kernel(qn_ref, qr_ref, kn_ref, kr_ref, v_ref, o_ref, *, tq):
    p = pl.program_id(1)
    lane = _lane_iota()

    def keys_at(lo, hi):
        return jnp.concatenate([kn_ref[lo:hi, :], kr_ref[lo:hi, :]], axis=1)

    def vals_at(lo, hi):
        return v_ref[lo:hi, :]

    for qi in range(qn_ref.shape[0] // tq):
        qn = qn_ref[qi * tq:(qi + 1) * tq, :]
        qr = qr_ref[qi * tq:(qi + 1) * tq, :]
        zero = jnp.zeros_like(qn)
        halves = []
        for h in range(2):
            nope_h = jnp.where((lane // MLA_NOPE_DIM) == h, qn, zero)
            rope_h = jnp.where((lane // MLA_ROPE_DIM) == 2 * p + h, qr, zero)
            halves.append(jnp.concatenate([nope_h, rope_h], axis=1))
        qs = jnp.concatenate(halves, axis=0)
        _causal_pair_attention(qs, keys_at, vals_at, qi, o_ref.at[qi * tq:(qi + 1) * tq, :], tq)


def _mla_attention(qn, qr, kn, kr4, vv, tq=512):
    b, t, _ = qn.shape
    npair = MLA_HEADS // 2
    pair = pl.BlockSpec((None, t, LANES), lambda bi, p: (bi, 0, p))
    shared = pl.BlockSpec((None, t, LANES), lambda bi, p: (bi, 0, 0))
    return pl.pallas_call(
        functools.partial(_mla_kernel, tq=tq),
        grid=(b, npair),
        in_specs=[pair, shared, pair, shared, pair],
        out_specs=pair,
        out_shape=jax.ShapeDtypeStruct((b, t, MLA_WIDTH), BF16),
        name="mla_attention",
        compiler_params=_cparams(("parallel", "parallel")),
    )(qn, qr, kn, kr4, vv)


DIL_UNROLL = 8


def _dil_kernel(q_ref, k_ref, v_ref, o_ref, qf, kf, vf, qb, kb, vb, accb, mb, lb):
    t = q_ref.shape[0]
    blk = DIL_BLOCK
    lane = _lane_iota()
    head0 = lane < HEAD_DIM
    qf[...] = q_ref[...].astype(F32)
    kf[...] = k_ref[...].astype(F32)
    vf[...] = v_ref[...].astype(F32)

    for g, (_, rate) in enumerate(DIL_PAIRS):
        length = t // rate
        kb[g, 0:blk, :] = jnp.zeros((blk, LANES), BF16)
        vb[g, 0:blk, :] = jnp.zeros((blk, LANES), BF16)
        if rate == 1:
            qb[g, blk:, :] = q_ref[...]
            kb[g, blk:, :] = k_ref[...]
            vb[g, blk:, :] = v_ref[...]
            continue
        for res in range(rate):
            dst = slice(blk + res * length, blk + (res + 1) * length)
            src = pl.ds(res, length, stride=rate)
            qb[g, dst, :] = qf[src, :].astype(BF16)
            kb[g, dst, :] = kf[src, :].astype(BF16)
            vb[g, dst, :] = vf[src, :].astype(BF16)

    qi2 = lax.broadcasted_iota(jnp.int32, (blk, 2 * blk), 0)
    kj2 = lax.broadcasted_iota(jnp.int32, (blk, 2 * blk), 1)
    band = (kj2 >= qi2) & (kj2 <= qi2 + blk)
    bias_full = jnp.where(band, 0.0, NEG_INF)
    bias_first = jnp.where(band & (kj2 >= blk), 0.0, NEG_INF)

    for g, (window, rate) in enumerate(DIL_PAIRS):
        assert window // rate == blk
        nb = t // rate // blk
        nblocks = t // blk

        def body(j, carry, g=g, nb=nb):
            base = pl.multiple_of(j * blk, blk)
            q = qb[g, pl.ds(base + blk, blk), :]
            zero = jnp.zeros_like(q)
            qs = jnp.concatenate([jnp.where(head0, q, zero), jnp.where(head0, zero, q)], axis=0)
            keys = kb[g, pl.ds(base, 2 * blk), :]
            vals = vb[g, pl.ds(base, 2 * blk), :]
            bias = bias_first if nb == 1 else jnp.where((j % nb) != 0, bias_full, bias_first)
            s = _dot_nt(qs, keys)
            ps, ms = [], []
            for h in range(2):
                sh = s[h * blk:(h + 1) * blk] + bias
                m = jnp.max(sh, axis=1, keepdims=True)
                ps.append(jnp.exp((sh - m).astype(BF16)))
                ms.append(m)
            pv = _dot(jnp.concatenate(ps, axis=0),
                      jnp.concatenate([vals, jnp.ones_like(vals)], axis=1))
            rows = pl.ds(base, blk)
            accb[g, rows, :] = jnp.where(head0, pv[:blk, :LANES], pv[blk:, :LANES])
            mb[g, rows, :] = jnp.where(head0, ms[0], ms[1])
            lb[g, rows, :] = jnp.where(head0, pv[:blk, LANES:], pv[blk:, LANES:])
            return carry

        lax.fori_loop(0, nblocks, body, 0, unroll=DIL_UNROLL)

    rate_max = max(rate for _, rate in DIL_PAIRS)
    length = t // rate_max
    for res in range(rate_max):
        m_all, l_all, a_all = [], [], []
        for g, (_, rate) in enumerate(DIL_PAIRS):
            start = (res % rate) * (t // rate) + res // rate
            step = rate_max // rate
            rows = pl.ds(start, length) if step == 1 else pl.ds(start, length, stride=step)
            m_all.append(mb[g, rows, :])
            l_all.append(lb[g, rows, :])
            a_all.append(accb[g, rows, :])
        m_max = jnp.maximum(jnp.maximum(m_all[0], m_all[1]), m_all[2])
        ws = [jnp.exp(m - m_max) for m in m_all]
        num = ws[0] * a_all[0] + ws[1] * a_all[1] + ws[2] * a_all[2]
        den = ws[0] * l_all[0] + ws[1] * l_all[1] + ws[2] * l_all[2]
        o_ref[pl.ds(res, length, stride=rate_max), :] = num / den


def _dil_attention(main3):
    b, t, _ = main3.shape
    assert all(t % (rate * DIL_BLOCK) == 0 for _, rate in DIL_PAIRS)
    npair = DIL_HEADS // 2
    nbr = len(DIL_PAIRS)

    def col(start):
        return pl.BlockSpec((None, t, LANES), lambda bi, p: (bi, 0, start // LANES + p))

    return pl.pallas_call(
        _dil_kernel,
        grid=(b, npair),
        in_specs=[col(COL_DIL_QK), col(COL_DIL_QK + DIL_WIDTH), col(COL_DIL_V)],
        out_specs=pl.BlockSpec((None, t, LANES), lambda bi, p: (bi, 0, p)),
        out_shape=jax.ShapeDtypeStruct((b, t, DIL_WIDTH), F32),
        name="dilated_attention",
        scratch_shapes=[pltpu.VMEM((t, LANES), F32)] * 3
        + [pltpu.VMEM((nbr, t + DIL_BLOCK, LANES), BF16)] * 3
        + [pltpu.VMEM((nbr, t, LANES), F32)] * 3,
        compiler_params=_cparams(("parallel", "parallel")),
    )(main3, main3, main3)


def _merge_kernel(x_ref, gf_ref, gd_ref, gm_ref, of_ref, od_ref, om_ref,
                  wf_ref, wd_ref, wm_ref, wo_ref, o_ref):
    merged = (jax.nn.sigmoid(gf_ref[...].astype(F32)) * _dot(of_ref[...], wf_ref[...])
              + jax.nn.sigmoid(gd_ref[...].astype(F32)) * _dot(od_ref[...].astype(BF16), wd_ref[...])
              + jax.nn.sigmoid(gm_ref[...].astype(F32)) * _dot(om_ref[...], wm_ref[...]))
    o_ref[...] = x_ref[...] + _dot(merged.astype(BF16), wo_ref[...])


def _merge(x2, main2, o_fox, o_dil, o_mla, layer, wf, wd, wm, wo, tm=512):
    n = x2.shape[0]

    def tok(width, col=0):
        return pl.BlockSpec((tm, width), lambda i: (i, col))

    def full(w):
        return pl.BlockSpec((None,) + w.shape[1:], lambda i: (layer, 0, 0))

    return pl.pallas_call(
        _merge_kernel,
        grid=(n // tm,),
        in_specs=[tok(D_MODEL), tok(D_MODEL, 0), tok(D_MODEL, 1), tok(D_MODEL, 2),
                  tok(FOX_WIDTH), tok(DIL_WIDTH), tok(MLA_WIDTH),
                  full(wf), full(wd), full(wm), full(wo)],
        out_specs=tok(D_MODEL),
        out_shape=jax.ShapeDtypeStruct((n, D_MODEL), F32),
        name="merge_outproj",
        compiler_params=_cparams(("parallel",)),
    )(x2, main2, main2, main2, o_fox, o_dil, o_mla, wf, wd, wm, wo)


def _ffn_kernel(x_ref, g_ref, wg_ref, wu_ref, wd_ref, p_ref, pg_ref, wpg_ref, wpp_ref, o_ref):
    x = x_ref[...]
    h = _rms(x, g_ref[...]).astype(BF16)
    gate = _dot(h, wg_ref[...])
    up = _dot(h, wu_ref[...])
    act = gate * jax.nn.sigmoid(gate) * up
    x = x + _dot(act.astype(BF16), wd_ref[...])
    hp = _rms(x, pg_ref[...]).astype(BF16)
    gate_p = jax.nn.sigmoid(_dot(hp, wpg_ref[...]))
    o_ref[...] = x + gate_p * _dot(p_ref[...].astype(BF16), wpp_ref[...])


def _ffn_ple(x2, gain, j, wg, wu, wd, layer, p3, ple_gain, wpg, wpp, tm=256):
    n = x2.shape[0]
    tok = pl.BlockSpec((tm, D_MODEL), lambda i: (i, 0))
    vec = pl.BlockSpec((1, D_MODEL), lambda i: (0, 0))

    def of(index, w):
        return pl.BlockSpec((None,) + w.shape[1:], lambda i: (index, 0, 0))

    return pl.pallas_call(
        _ffn_kernel,
        grid=(n // tm,),
        in_specs=[tok, vec, of(j, wg), of(j, wu), of(j, wd),
                  pl.BlockSpec((None, tm, PLE_DIM), lambda i: (layer, i, 0)), vec,
                  of(layer, wpg), of(layer, wpp)],
        out_specs=tok,
        out_shape=jax.ShapeDtypeStruct((n, D_MODEL), F32),
        name="ffn_dense_ple",
        compiler_params=pltpu.CompilerParams(dimension_semantics=("parallel",),
                                             vmem_limit_bytes=VMEM_LIMIT_LARGE),
    )(x2, gain, wg, wu, wd, p3, ple_gain, wpg, wpp)


MOE_BLOCK = 2048
MOE_TILE = 256
MOE_TILES = -(-(2 * MOE_BLOCK + N_EXPERTS * (MOE_TILE - 1)) // MOE_TILE)
MOE_META_ROWS = 32
assert MOE_TILES <= MOE_META_ROWS
MOE_CHUNK = 256
MOE_CHUNKS = MOE_BLOCK // MOE_CHUNK
MOE_WINDOW = 5


def _router_kernel(x_ref, g_ref, wc_ref, h_ref, rc_ref, rr_ref, meta_ref):
    h = _rms(x_ref[...], g_ref[...])
    h_ref[...] = h.astype(BF16)
    h_hi, h_mid, h_lo = _split3(h)
    a = _dot(h_hi, wc_ref[...])
    b = _dot(h_mid, wc_ref[:, :2 * LANES])
    c = _dot(h_lo, wc_ref[:, :LANES])
    logits = (a[:, :LANES] + (a[:, LANES:2 * LANES] + b[:, :LANES])
              + (a[:, 2 * LANES:] + b[:, LANES:] + c))
    lane = _lane_iota()
    is_expert = lane < N_EXPERTS
    logits = jnp.where(is_expert, logits, NEG_INF)
    v1 = jnp.max(logits, axis=1, keepdims=True)
    i1 = jnp.min(jnp.where(logits == v1, lane, LANES), axis=1, keepdims=True)
    first = lane == i1
    rest = jnp.where(first, NEG_INF, logits)
    v2 = jnp.max(rest, axis=1, keepdims=True)
    i2 = jnp.min(jnp.where(rest == v2, lane, LANES), axis=1, keepdims=True)
    second = lane == i2
    e2 = jnp.exp(v2 - v1)
    w1 = 1.0 / (1.0 + e2)
    w2 = e2 / (1.0 + e2)

    sel = jnp.where(first, 1.0, jnp.where(second, 1.0, 0.0))
    row = lax.broadcasted_iota(jnp.int32, (CUMSUM_BLOCK, CUMSUM_BLOCK), 0)
    col = lax.broadcasted_iota(jnp.int32, (CUMSUM_BLOCK, CUMSUM_BLOCK), 1)
    tri = jnp.where(col < row, 1.0, 0.0).astype(BF16)
    carry = jnp.zeros((1, LANES), F32)
    ranks = []
    for blk in range(sel.shape[0] // CUMSUM_BLOCK):
        part = sel[blk * CUMSUM_BLOCK:(blk + 1) * CUMSUM_BLOCK]
        ranks.append(_dot(tri, part.astype(BF16)) + carry)
        carry = carry + jnp.sum(part, axis=0, keepdims=True)
    rank = jnp.concatenate(ranks, axis=0)
    padded = jnp.ceil(carry / MOE_TILE) * MOE_TILE
    er = lax.broadcasted_iota(jnp.int32, (LANES, LANES), 0)
    ec = lax.broadcasted_iota(jnp.int32, (LANES, LANES), 1)
    before = jnp.where(er < ec, 1.0, 0.0).astype(BF16)
    start = _dot(jnp.broadcast_to(padded, (8, LANES)).astype(BF16), before)[0:1]
    slot = start + rank
    dest1 = jnp.sum(jnp.where(first, slot, 0.0), axis=1, keepdims=True)
    dest2 = jnp.sum(jnp.where(second, slot, 0.0), axis=1, keepdims=True)
    routing = jnp.where(lane == 0, dest1, jnp.where(lane == 1, dest2,
                        jnp.where(lane == 2, w1, jnp.where(lane == 3, w2, 0.0))))
    rc_ref[...] = routing
    rrow = routing.T[0:8]
    for c in range(MOE_BLOCK // MOE_CHUNK):
        rr_ref[c] = rrow[:, c * MOE_CHUNK:(c + 1) * MOE_CHUNK]

    end = start + padded
    tile_row = lax.broadcasted_iota(jnp.int32, (MOE_META_ROWS, LANES), 0).astype(F32) * MOE_TILE
    passed = jnp.sum(jnp.where(is_expert, jnp.where(tile_row >= end, 1.0, 0.0), 0.0),
                     axis=1, keepdims=True)
    last = jnp.max(jnp.where(is_expert, jnp.where(padded > 0, lane.astype(F32), 0.0), 0.0),
                   axis=1, keepdims=True)
    total = jnp.sum(jnp.where(is_expert, padded, 0.0), axis=1, keepdims=True)
    active = jnp.where(tile_row < total, 1.0, 0.0)
    nblock = rrow.shape[1]
    tok = lax.broadcasted_iota(jnp.int32, (MOE_META_ROWS, nblock), 1).astype(F32)
    tile_id = lax.broadcasted_iota(jnp.int32, (MOE_META_ROWS, nblock), 0).astype(F32)
    in_tile = jnp.where(jnp.floor(rrow[0:1] / MOE_TILE) == tile_id, 1.0,
                        jnp.where(jnp.floor(rrow[1:2] / MOE_TILE) == tile_id, 1.0, 0.0))
    first_tok = jnp.min(jnp.where(in_tile > 0, tok, float(nblock)), axis=1, keepdims=True)
    last_tok = jnp.max(jnp.where(in_tile > 0, tok, -1.0), axis=1, keepdims=True)
    chunk_lo = jnp.floor(first_tok / MOE_CHUNK)
    chunk_hi = jnp.floor(last_tok / MOE_CHUNK)
    meta = jnp.where(lane == 0, jnp.minimum(passed, last),
                     jnp.where(lane == 1, active,
                               jnp.where(lane == 2, chunk_lo, jnp.where(lane == 3, chunk_hi, 0.0))))
    meta_ref[...] = meta.astype(jnp.int32)


def _router(x2, gain, w_router):
    n = x2.shape[0]
    nblk = n // MOE_BLOCK
    w_pad = jnp.zeros((D_MODEL, LANES), F32).at[:, :N_EXPERTS].set(w_router)
    w_hi = w_pad.astype(BF16)
    r1 = w_pad - w_hi.astype(F32)
    w_mid = r1.astype(BF16)
    w_lo = (r1 - w_mid.astype(F32)).astype(BF16)
    w_cat = jnp.concatenate([w_hi, w_mid, w_lo], axis=1)
    return pl.pallas_call(
        _router_kernel,
        grid=(nblk,),
        in_specs=[pl.BlockSpec((MOE_BLOCK, D_MODEL), lambda i: (i, 0)),
                  pl.BlockSpec((1, D_MODEL), lambda i: (0, 0)),
                  pl.BlockSpec(w_cat.shape, lambda i: (0, 0))],
        out_specs=[pl.BlockSpec((MOE_BLOCK, D_MODEL), lambda i: (i, 0)),
                   pl.BlockSpec((MOE_BLOCK, LANES), lambda i: (i, 0)),
                   pl.BlockSpec((None, MOE_CHUNKS, 8, MOE_CHUNK), lambda i: (i, 0, 0, 0)),
                   pl.BlockSpec((None, MOE_META_ROWS, LANES), lambda i: (i, 0, 0))],
        out_shape=[jax.ShapeDtypeStruct((n, D_MODEL), BF16),
                   jax.ShapeDtypeStruct((n, LANES), F32),
                   jax.ShapeDtypeStruct((nblk, MOE_CHUNKS, 8, MOE_CHUNK), F32),
                   jax.ShapeDtypeStruct((nblk, MOE_META_ROWS, LANES), jnp.int32)],
        name="router",
        compiler_params=_cparams(("parallel",)),
    )(x2, gain, w_cat)


def _moe_kernel(te_ref, ta_ref, lo_ref, hi_ref, h_ref, rrow_ref, rcol_ref, wgu_ref, wd_ref, o_ref):
    b = pl.program_id(0)
    r = pl.program_id(1)

    @pl.when(r == 0)
    def _():
        o_ref[...] = jnp.zeros(o_ref.shape, F32)

    def tile_work(first_chunk, nchunks):
        base = (r * MOE_TILE).astype(F32)
        slot_col = lax.broadcasted_iota(jnp.int32, (MOE_TILE, 1), 0).astype(F32) + base
        slot_row = lax.broadcasted_iota(jnp.int32, (1, MOE_TILE), 1).astype(F32) + base
        routing = rrow_ref[pl.ds(first_chunk, nchunks)]
        onehots = []
        weight = jnp.zeros((MOE_TILE, 1), F32)
        for c in range(nchunks):
            hit1 = routing[c, 0:1, :] == slot_col
            hit2 = routing[c, 1:2, :] == slot_col
            onehots.append(jnp.where(hit1, 1.0, jnp.where(hit2, 1.0, 0.0)).astype(BF16))
            weight = weight + jnp.sum(
                jnp.where(hit1, routing[c, 2:3, :], jnp.where(hit2, routing[c, 3:4, :], 0.0)),
                axis=1, keepdims=True)
        toks = pl.ds(pl.multiple_of(first_chunk * MOE_CHUNK, MOE_CHUNK), nchunks * MOE_CHUNK)
        xg = _dot(jnp.concatenate(onehots, axis=1), h_ref[toks, :]).astype(BF16)
        gate_up = _dot(xg, wgu_ref[...])
        gate = gate_up[:, :D_FF_EXPERT]
        up = gate_up[:, D_FF_EXPERT:]
        act = gate * jax.nn.sigmoid(gate) * up
        y = (_dot(act.astype(BF16), wd_ref[...]) * weight).astype(BF16)
        back = jnp.where(rcol_ref[toks, 0:1] == slot_row, 1.0,
                         jnp.where(rcol_ref[toks, 1:2] == slot_row, 1.0, 0.0)).astype(BF16)
        o_ref[toks, :] += _dot(back, y)

    active = ta_ref[b, r] == 1
    first = jnp.minimum(lo_ref[b, r], MOE_CHUNKS - MOE_WINDOW)
    short = hi_ref[b, r] < first + MOE_WINDOW

    @pl.when(active & short)
    def _():
        tile_work(first, MOE_WINDOW)

    @pl.when(active & jnp.logical_not(short))
    def _():
        tile_work(0, MOE_CHUNKS)


def _moe(h2, rrow, rcol, meta, j, wgu, wd):
    n = h2.shape[0]
    nblk = n // MOE_BLOCK
    tile_tables = [meta[:, :MOE_TILES, k] for k in range(4)]

    def blocked(shape, index):
        return pl.BlockSpec(shape, lambda b, r, te, ta, lo, hi: index(b))

    def wspec(shape):
        return pl.BlockSpec((None,) + shape,
                            lambda b, r, te, ta, lo, hi: (j * N_EXPERTS + te[b, r], 0, 0))

    grid_spec = pltpu.PrefetchScalarGridSpec(
        num_scalar_prefetch=len(tile_tables),
        grid=(nblk, MOE_TILES),
        in_specs=[blocked((MOE_BLOCK, D_MODEL), lambda b: (b, 0)),
                  blocked((None, MOE_CHUNKS, 8, MOE_CHUNK), lambda b: (b, 0, 0, 0)),
                  blocked((MOE_BLOCK, LANES), lambda b: (b, 0)),
                  wspec((D_MODEL, 2 * D_FF_EXPERT)), wspec((D_FF_EXPERT, D_MODEL))],
        out_specs=blocked((MOE_BLOCK, D_MODEL), lambda b: (b, 0)),
    )
    return pl.pallas_call(
        _moe_kernel,
        grid_spec=grid_spec,
        out_shape=jax.ShapeDtypeStruct((n, D_MODEL), F32),
        name="moe_experts",
        compiler_params=pltpu.CompilerParams(dimension_semantics=("parallel", "arbitrary"),
                                             vmem_limit_bytes=VMEM_LIMIT_LARGE),
    )(*tile_tables, h2, rrow, rcol, wgu, wd)


def _moe_layer(x2, gain, w_router, j, wgu, wd):
    h2, rcol, rrow, meta = _router(x2, gain, w_router)
    return _moe(h2, rrow, rcol, meta, j, wgu, wd)


def _ple_kernel(x_ref, d_ref, p_ref, g_ref, wg_ref, wp_ref, fg_ref, o_ref, *, final):
    x = x_ref[...] + d_ref[...]
    h = _rms(x, g_ref[...]).astype(BF16)
    gate = jax.nn.sigmoid(_dot(h, wg_ref[...]))
    y = x + gate * _dot(p_ref[...].astype(BF16), wp_ref[...])
    if final:
        y = _rms(y, fg_ref[...])
    o_ref[...] = y


def _ple(x2, delta, layer, p3, gain, wg, wp, final_gain, final, tm=1024):
    n = x2.shape[0]
    vec = pl.BlockSpec((1, D_MODEL), lambda i: (0, 0))
    tok = pl.BlockSpec((tm, D_MODEL), lambda i: (i, 0))
    return pl.pallas_call(
        functools.partial(_ple_kernel, final=final),
        grid=(n // tm,),
        in_specs=[tok, tok,
                  pl.BlockSpec((None, tm, PLE_DIM), lambda i: (layer, i, 0)), vec,
                  pl.BlockSpec((None,) + wg.shape[1:], lambda i: (layer, 0, 0)),
                  pl.BlockSpec((None,) + wp.shape[1:], lambda i: (layer, 0, 0)), vec],
        out_specs=pl.BlockSpec((tm, D_MODEL), lambda i: (i, 0)),
        out_shape=jax.ShapeDtypeStruct((n, D_MODEL), F32),
        name="ple_final" if final else "ple",
        compiler_params=_cparams(("parallel",)),
    )(x2, delta, p3, gain, wg, wp, final_gain)


def _cast_kernel(x_ref, o_ref):
    o_ref[...] = x_ref[...].astype(BF16)


def _to_bf16(a):
    r, c = a.shape[-2:]
    a3 = a.reshape((-1, r, c))
    spec = pl.BlockSpec((None, r, c), lambda i: (i, 0, 0))
    out = pl.pallas_call(
        _cast_kernel,
        grid=(a3.shape[0],),
        in_specs=[spec],
        out_specs=spec,
        out_shape=jax.ShapeDtypeStruct(a3.shape, BF16),
        name="cast_bf16",
        compiler_params=_cparams(("parallel",)),
    )(a3)
    return out.reshape(a.shape)


def _cast_pair_kernel(a_ref, b_ref, o_ref):
    width = a_ref.shape[-1]
    o_ref[:, :width] = a_ref[...].astype(BF16)
    o_ref[:, width:] = b_ref[...].astype(BF16)


def _to_bf16_side_by_side(a, b):
    r, c = a.shape[-2:]
    a3 = a.reshape((-1, r, c))
    b3 = b.reshape((-1, r, c))
    spec = pl.BlockSpec((None, r, c), lambda i: (i, 0, 0))
    return pl.pallas_call(
        _cast_pair_kernel,
        grid=(a3.shape[0],),
        in_specs=[spec, spec],
        out_specs=pl.BlockSpec((None, r, 2 * c), lambda i: (i, 0, 0)),
        out_shape=jax.ShapeDtypeStruct((a3.shape[0], r, 2 * c), BF16),
        name="cast_pair_bf16",
        compiler_params=_cparams(("parallel",)),
    )(a3, b3)


IN_SIZES = (FOX_WIDTH, FOX_WIDTH, FOX_WIDTH, FOX_HEADS, DIL_WIDTH, DIL_WIDTH, DIL_WIDTH,
            MLA_Q_RANK, MLA_KV_RANK, MLA_ROPE_DIM, GATE_WIDTH)
IN_WIDTH = sum(IN_SIZES)
IN_PREP_COLS = 256


def _inproj_weight_kernel(w_ref, main_ref, small_ref):
    offs = [0]
    for s in IN_SIZES:
        offs.append(offs[-1] + s)
    fq, fk, fv, fz, dq, dk, dv, cq, ckv, kr, gates = [(offs[i], offs[i + 1])
                                                      for i in range(len(IN_SIZES))]
    scale = HEAD_DIM ** -0.5
    row = 0
    for (lo, hi), mult in ((gates, 1.0), (dq, scale), (dk, 1.0), (fq, scale), (fk, 1.0), (fv, 1.0),
                           (dv, 1.0), (cq, 1.0), (ckv, 1.0)):
        main_ref[row:row + hi - lo, :] = (w_ref[lo:hi, :] * mult).astype(BF16)
        row += hi - lo
    small_ref[...] = jnp.zeros(small_ref.shape, BF16)
    small_ref[0:FOX_HEADS, :] = w_ref[fz[0]:fz[1], :].astype(BF16)
    for h in range(MLA_HEADS):
        r0 = LANES + h * MLA_ROPE_DIM
        small_ref[r0:r0 + MLA_ROPE_DIM, :] = w_ref[kr[0]:kr[1], :].astype(BF16)


def _prep_in_weights(w_in):
    depth = w_in.shape[0]
    cb = IN_PREP_COLS
    w_t = jnp.swapaxes(w_in, 1, 2)

    def blk(rows):
        return pl.BlockSpec((None, rows, cb), lambda l, i: (l, 0, i))

    return pl.pallas_call(
        _inproj_weight_kernel,
        grid=(depth, D_MODEL // cb),
        in_specs=[blk(IN_WIDTH)],
        out_specs=[blk(MAIN_WIDTH), blk(SMALL_WIDTH)],
        out_shape=[jax.ShapeDtypeStruct((depth, MAIN_WIDTH, D_MODEL), BF16),
                   jax.ShapeDtypeStruct((depth, SMALL_WIDTH, D_MODEL), BF16)],
        name="inproj_weights",
        compiler_params=_cparams(("parallel", "parallel")),
    )(w_t)


def _prep_mla_weights(w_uq, w_ukv):
    depth = w_uq.shape[0]
    uq = w_uq.reshape(depth, MLA_Q_RANK, MLA_HEADS, MLA_NOPE_DIM + MLA_ROPE_DIM)
    wq = jnp.concatenate([uq[..., :MLA_NOPE_DIM].reshape(depth, MLA_Q_RANK, -1),
                          uq[..., MLA_NOPE_DIM:].reshape(depth, MLA_Q_RANK, -1)], axis=-1)
    ukv = w_ukv.reshape(depth, MLA_KV_RANK, MLA_HEADS, MLA_NOPE_DIM + MLA_V_DIM)
    wkv = jnp.concatenate([ukv[..., :MLA_NOPE_DIM].reshape(depth, MLA_KV_RANK, -1),
                           ukv[..., MLA_NOPE_DIM:].reshape(depth, MLA_KV_RANK, -1)], axis=-1)
    return wq.astype(BF16), wkv.astype(BF16)


def kernel(x, p, positions, mix_norm, w_in, b_forget, mla_q_norm, mla_kv_norm, w_uq, w_ukv,
           w_br_fox, w_br_dil, w_br_mla, w_out, ffn_norm, w_ffn_gate, w_ffn_up, w_ffn_down,
           w_router, w_exp_gate, w_exp_up, w_exp_down, ple_norm, w_ple_gate, w_ple_proj,
           final_norm):
    b, t, _ = x.shape
    n = b * t
    depth = w_in.shape[0]
    assert depth % 2 == 0

    w_main, w_small = _prep_in_weights(w_in)
    wq_all, wkv_all = _prep_mla_weights(w_uq, w_ukv)
    (w_br_fox, w_br_dil, w_br_mla, w_out, w_ffn_gate, w_ffn_up, w_ffn_down, w_exp_down, w_ple_gate,
     w_ple_proj) = [
        _to_bf16(w) for w in (w_br_fox, w_br_dil, w_br_mla, w_out, w_ffn_gate, w_ffn_up, w_ffn_down,
                              w_exp_down, w_ple_gate, w_ple_proj)]
    w_exp_gate_up = _to_bf16_side_by_side(w_exp_gate, w_exp_up)
    w_exp_down = w_exp_down.reshape((-1,) + w_exp_down.shape[2:])
    p3 = p.reshape(depth, n, PLE_DIM)
    bias_all = jnp.concatenate(
        [b_forget.astype(F32), jnp.zeros((depth, LANES - FOX_HEADS), F32)], axis=-1)
    tabs = [tab.reshape(n, LANES) for tab in _rope_tables(positions)]
    final_gain = final_norm.reshape(1, D_MODEL)

    x2 = x.reshape(n, D_MODEL)
    for i in range(depth):
        main2, fz, qn, qr, kn, kr4, vv = _inproj(
            x2, mix_norm[i].reshape(1, D_MODEL), i, w_main, w_small, tabs,
            mla_q_norm[i].reshape(1, -1), mla_kv_norm[i].reshape(1, -1), wq_all[i], wkv_all[i])
        main3 = main2.reshape(b, t, COL_MLA)

        dcols = _decay(fz.reshape(b, t, LANES), bias_all[i].reshape(1, LANES))
        o_fox = _fox_attention(main3, dcols)
        o_dil = _dil_attention(main3)
        o_mla = _mla_attention(*[a.reshape(b, t, -1) for a in (qn, qr, kn, kr4, vv)])

        x2 = _merge(x2, main2, o_fox.reshape(n, -1), o_dil.reshape(n, -1), o_mla.reshape(n, -1),
                    i, w_br_fox, w_br_dil, w_br_mla, w_out)

        j = i // 2
        gain = ffn_norm[i].reshape(1, D_MODEL)
        ple_args = (i, p3, ple_norm[i].reshape(1, D_MODEL), w_ple_gate, w_ple_proj)
        if i % 2 == 0:
            x2 = _ffn_ple(x2, gain, j, w_ffn_gate, w_ffn_up, w_ffn_down, *ple_args)
        else:
            delta = _moe_layer(x2, gain, w_router[j], j, w_exp_gate_up, w_exp_down)
            x2 = _ple(x2, delta, *ple_args, final_gain, final=(i == depth - 1))
    return x2.reshape(b, t, D_MODEL)
```

```python
import functools

import jax
import jax.numpy as jnp
from jax import lax
from jax.experimental import pallas as pl
from jax.experimental.pallas import tpu as pltpu

F32 = jnp.float32
BF16 = jnp.bfloat16

D_MODEL = 1024
HEAD_DIM = 64
ROPE_THETA = 10000.0
NORM_EPS = 1e-6
FOX_HEADS = 6
FOX_WIDTH = FOX_HEADS * HEAD_DIM
DIL_HEADS = 6
DIL_WIDTH = DIL_HEADS * HEAD_DIM
DIL_PAIRS = ((128, 1), (512, 4), (2048, 16))
DIL_BLOCK = 128
MLA_HEADS = 4
MLA_Q_RANK = 256
MLA_KV_RANK = 256
MLA_NOPE_DIM = 64
MLA_ROPE_DIM = 32
MLA_V_DIM = 64
MLA_WIDTH = MLA_HEADS * MLA_V_DIM
N_BRANCHES = 3
D_FF = 2816
N_EXPERTS = 8
D_FF_EXPERT = 1408
PLE_DIM = 256

LANES = 128
GATE_WIDTH = N_BRANCHES * D_MODEL
COL_GATES = 0
COL_DIL_QK = GATE_WIDTH
COL_FOX = COL_DIL_QK + 2 * DIL_WIDTH
COL_DIL_V = COL_FOX + 3 * FOX_WIDTH
COL_MLA = COL_DIL_V + DIL_WIDTH
MAIN_WIDTH = COL_MLA + MLA_Q_RANK + MLA_KV_RANK
MXU_TILE = 256
assert all(c % MXU_TILE == 0 for c in (COL_DIL_QK, COL_FOX, COL_MLA, MAIN_WIDTH))
SMALL_WIDTH = 2 * LANES
VMEM_LIMIT = 56 * 1024 * 1024
VMEM_LIMIT_LARGE = 60 * 1024 * 1024

NEG_INF = float("-inf")


def _cparams(sem):
    return pltpu.CompilerParams(dimension_semantics=sem, vmem_limit_bytes=VMEM_LIMIT)


def _rms(x, gain):
    return x * lax.rsqrt(jnp.mean(x * x, axis=-1, keepdims=True) + NORM_EPS) * gain


def _dot(a, b):
    return jnp.dot(a, b, preferred_element_type=F32)


def _dot_nt(a, b):
    return lax.dot_general(a, b, (((1,), (1,)), ((), ())), preferred_element_type=F32)


def _lane_iota():
    return lax.broadcasted_iota(jnp.int32, (1, LANES), 1)


ROPE_HALF_D = HEAD_DIM // 2
ROPE_HALF_M = MLA_ROPE_DIM // 2


def _rope_tab_kernel(pos_ref, f_ref, cd_ref, sd_ref, cm_ref, sm_ref):
    pos = pos_ref[...].astype(F32)
    lane = _lane_iota()
    ang = pos * f_ref[...]
    src = lax.broadcasted_iota(jnp.int32, (LANES, LANES), 0)
    dst = lax.broadcasted_iota(jnp.int32, (LANES, LANES), 1)
    pick_d = jnp.where(src == dst % ROPE_HALF_D, 1.0, 0.0).astype(BF16)
    pick_m = jnp.where(src == ROPE_HALF_D + dst % ROPE_HALF_M, 1.0, 0.0).astype(BF16)

    def spread(x, pick):
        hi, mid, lo = _split3(x)
        return _dot(hi, pick) + (_dot(mid, pick) + _dot(lo, pick))

    cos = jnp.cos(ang)
    sin = jnp.sin(ang)
    cd_ref[...] = spread(cos, pick_d)
    sd_ref[...] = spread(sin, pick_d) * jnp.where((lane % HEAD_DIM) < ROPE_HALF_D, -1.0, 1.0)
    cm_ref[...] = spread(cos, pick_m)
    sm_ref[...] = spread(sin, pick_m) * jnp.where((lane % MLA_ROPE_DIM) < ROPE_HALF_M, -1.0, 1.0)


def _rope_tables(positions):
    b, t = positions.shape
    inv_d = ROPE_THETA ** (-jnp.arange(ROPE_HALF_D, dtype=F32) / ROPE_HALF_D)
    inv_m = ROPE_THETA ** (-jnp.arange(ROPE_HALF_M, dtype=F32) / ROPE_HALF_M)
    freqs = jnp.concatenate(
        [inv_d, inv_m, jnp.zeros((LANES - ROPE_HALF_D - ROPE_HALF_M,), F32)])[None, :]
    tab = jax.ShapeDtypeStruct((b, t, LANES), F32)
    tab_spec = pl.BlockSpec((None, t, LANES), lambda i: (i, 0, 0))
    return pl.pallas_call(
        _rope_tab_kernel,
        grid=(b,),
        in_specs=[pl.BlockSpec((None, t, 1), lambda i: (i, 0, 0)),
                  pl.BlockSpec((1, LANES), lambda i: (0, 0))],
        out_specs=[tab_spec] * 4,
        out_shape=[tab] * 4,
        name="rope_tables",
        compiler_params=_cparams(("parallel",)),
    )(positions.reshape(b, t, 1), freqs)


def _swap_halves(x, width):
    lane = _lane_iota()
    half = width // 2
    return jnp.where((lane % width) < half,
                     pltpu.roll(x, LANES - half, axis=1),
                     pltpu.roll(x, half, axis=1))


def _rope(x, cos, sin_signed, width):
    return x * cos + _swap_halves(x, width) * sin_signed


def _col_chunks(width, step=512):
    return [(c, min(step, width - c)) for c in range(0, width, step)]


MLA_NOPE_WIDTH = MLA_HEADS * MLA_NOPE_DIM
MLA_SCALE = (MLA_NOPE_DIM + MLA_ROPE_DIM) ** -0.5


def _inproj_kernel(x_ref, g_ref, w_ref, ws_ref, cd_ref, sd_ref, cm_ref, sm_ref,
                   gq_ref, gkv_ref, wq_ref, wkv_ref,
                   o_ref, fz_ref, qn_ref, qr_ref, kn_ref, kr_ref, v_ref):
    h = _rms(x_ref[...], g_ref[...]).astype(BF16)
    for lo, hi in ((0, COL_DIL_QK), (COL_FOX, COL_MLA)):
        for c, n in _col_chunks(hi - lo):
            o_ref[:, lo + c:lo + c + n] = _dot_nt(h, w_ref[lo + c:lo + c + n, :]).astype(BF16)
    cos_d = cd_ref[...]
    sin_d = sd_ref[...]
    for c0 in range(COL_DIL_QK, COL_FOX, MXU_TILE):
        y = _dot_nt(h, w_ref[c0:c0 + MXU_TILE, :])
        for c in range(0, MXU_TILE, LANES):
            o_ref[:, c0 + c:c0 + c + LANES] = _rope(y[:, c:c + LANES], cos_d, sin_d,
                                                    HEAD_DIM).astype(BF16)

    cos_m = cm_ref[...]
    sin_m = sm_ref[...]
    lat = _dot_nt(h, w_ref[COL_MLA:MAIN_WIDTH, :])
    hq = _rms(lat[:, :MLA_Q_RANK], gq_ref[...]).astype(BF16)
    q = _dot(hq, wq_ref[...])
    qn_ref[...] = (q[:, :MLA_NOPE_WIDTH] * MLA_SCALE).astype(BF16)
    qr_ref[...] = (_rope(q[:, MLA_NOPE_WIDTH:], cos_m, sin_m, MLA_ROPE_DIM) * MLA_SCALE).astype(BF16)
    hkv = _rms(lat[:, MLA_Q_RANK:], gkv_ref[...]).astype(BF16)
    kv = _dot(hkv, wkv_ref[...])
    kn_ref[...] = kv[:, :MLA_NOPE_WIDTH].astype(BF16)
    v_ref[...] = kv[:, MLA_NOPE_WIDTH:].astype(BF16)

    small = _dot_nt(h, ws_ref[...])
    fz_ref[...] = small[:, :LANES]
    kr_ref[...] = _rope(small[:, LANES:], cos_m, sin_m, MLA_ROPE_DIM).astype(BF16)


def _inproj(x2, gain, layer, w_main_t, w_small_t, tabs, gq, gkv, wq, wkv, tm=512):
    n = x2.shape[0]

    def tok(width):
        return pl.BlockSpec((tm, width), lambda i: (i, 0))

    def full(a):
        return pl.BlockSpec(a.shape, lambda i: (0, 0))

    def of_layer(a):
        return pl.BlockSpec((None,) + a.shape[1:], lambda i: (layer, 0, 0))

    widths = (COL_MLA, LANES, MLA_NOPE_WIDTH, LANES, MLA_NOPE_WIDTH, LANES, MLA_WIDTH)
    dtypes = (BF16, F32, BF16, BF16, BF16, BF16, BF16)
    return pl.pallas_call(
        _inproj_kernel,
        grid=(n // tm,),
        in_specs=[tok(D_MODEL), full(gain), of_layer(w_main_t), of_layer(w_small_t)]
        + [tok(LANES)] * 4 + [full(gq), full(gkv), full(wq), full(wkv)],
        out_specs=[tok(w) for w in widths],
        out_shape=[jax.ShapeDtypeStruct((n, w), d) for w, d in zip(widths, dtypes)],
        name="inproj",
        compiler_params=_cparams(("parallel",)),
    )(x2, gain, w_main_t, w_small_t, *tabs, gq, gkv, wq, wkv)


CUMSUM_BLOCK = 256
DECAY_PART_STRIDE = 8
DECAY_PARTS = 3


def _split3(a):
    hi = a.astype(BF16)
    r1 = a - hi.astype(F32)
    mid = r1.astype(BF16)
    lo = (r1 - mid.astype(F32)).astype(BF16)
    return hi, mid, lo


def _decay_kernel(z_ref, b_ref, o_ref):
    t = z_ref.shape[0]
    row = lax.broadcasted_iota(jnp.int32, (CUMSUM_BLOCK, CUMSUM_BLOCK), 0)
    col = lax.broadcasted_iota(jnp.int32, (CUMSUM_BLOCK, CUMSUM_BLOCK), 1)
    tri = jnp.where(col <= row, 1.0, 0.0).astype(BF16)
    is_head = _lane_iota() < FOX_HEADS
    carry = jnp.zeros((1, LANES), F32)
    for blk in range(t // CUMSUM_BLOCK):
        sl = slice(blk * CUMSUM_BLOCK, (blk + 1) * CUMSUM_BLOCK)
        z = z_ref[sl, :] + b_ref[...]
        logf = -(jnp.maximum(-z, 0.0) + jnp.log1p(jnp.exp(-jnp.abs(z))))
        hi, mid, lo = _split3(logf)
        cs = _dot(tri, hi) + _dot(tri, mid) + _dot(tri, lo) + carry
        carry = cs[CUMSUM_BLOCK - 1:CUMSUM_BLOCK, :]
        parts = _split3(jnp.where(is_head, cs, 0.0))
        packed = parts[0].astype(F32)
        for j in range(1, DECAY_PARTS):
            packed = packed + pltpu.roll(parts[j].astype(F32), j * DECAY_PART_STRIDE, axis=1)
        o_ref[sl, :] = packed.astype(BF16)


def _decay(small3, bias):
    b, t, _ = small3.shape
    return pl.pallas_call(
        _decay_kernel,
        grid=(b,),
        in_specs=[pl.BlockSpec((None, t, LANES), lambda i: (i, 0, 0)),
                  pl.BlockSpec((1, LANES), lambda i: (0, 0))],
        out_specs=pl.BlockSpec((None, t, LANES), lambda i: (i, 0, 0)),
        out_shape=jax.ShapeDtypeStruct((b, t, LANES), BF16),
        name="decay",
        compiler_params=_cparams(("parallel",)),
    )(small3, bias)


def _pair_softmax(s, m_prev, ok_cols):
    nc = s.shape[1] // LANES
    cols = [s[:, c * LANES:(c + 1) * LANES] for c in range(nc)]
    if ok_cols is not None:
        cols = [jnp.where(ok, col, NEG_INF) for ok, col in zip(ok_cols, cols)]
    cmax = cols[0]
    for col in cols[1:]:
        cmax = jnp.maximum(cmax, col)
    m_new = jnp.maximum(m_prev, jnp.max(cmax, axis=1, keepdims=True))
    alpha = jnp.exp(m_prev - m_new)
    p = jnp.concatenate([jnp.exp((col - m_new).astype(BF16)) for col in cols], axis=1)
    return m_new, alpha, p


def _causal_pair_attention(qs, keys_rows, vals_rows, qi, o_ref, tq):
    lane = _lane_iota()
    head0 = lane < HEAD_DIM

    def scores(j):
        return _dot_nt(qs, keys_rows(j * tq, (j + 1) * tq))

    def weighted_values(p, j):
        v = vals_rows(j * tq, (j + 1) * tq)
        one = jnp.ones_like(v)
        pv0 = _dot(p[:tq], jnp.where(head0, v, one))
        pv1 = _dot(p[tq:], jnp.where(head0, one, v))
        return jnp.concatenate([pv0, pv1], axis=0)

    s = scores(0)
    m_prev = jnp.full((2 * tq, LANES), NEG_INF, F32)
    acc = jnp.zeros((2 * tq, LANES), F32)
    for j in range(qi):
        s_next = scores(j + 1)
        m_prev, alpha, p = _pair_softmax(s, m_prev, None)
        acc = alpha * acc + weighted_values(p, j)
        s = s_next

    r_i = lax.broadcasted_iota(jnp.int32, (tq, LANES), 0)
    c_i = lax.broadcasted_iota(jnp.int32, (tq, LANES), 1)
    ok_cols = []
    for c in range(tq // LANES):
        ok = c_i + c * LANES <= r_i
        ok_cols.append(jnp.concatenate([ok, ok], axis=0))
    _, alpha, p = _pair_softmax(s, m_prev, ok_cols)
    acc = alpha * acc + weighted_values(p, qi)
    out = acc / pltpu.roll(acc, HEAD_DIM, axis=1)
    o_ref[...] = jnp.where(head0, out[:tq], out[tq:]).astype(o_ref.dtype)


def _fox_kernel(q_ref, k_ref, v_ref, d_ref, o_ref, *, tq):
    p = pl.program_id(1)
    lane = _lane_iota()

    def keys_at(lo, hi):
        return jnp.concatenate([k_ref[lo:hi, :], d_ref[lo:hi, :]], axis=1)

    def vals_at(lo, hi):
        return v_ref[lo:hi, :]

    for qi in range(q_ref.shape[0] // tq):
        q = q_ref[qi * tq:(qi + 1) * tq, :]
        zero = jnp.zeros_like(q)
        halves = []
        for h in range(2):
            head = 2 * p + h
            pick = (lane % DECAY_PART_STRIDE == head) & (lane < DECAY_PARTS * DECAY_PART_STRIDE)
            neg = jnp.broadcast_to(jnp.where(pick, -1.0, 0.0).astype(BF16), q.shape)
            q_h = jnp.where((lane // HEAD_DIM) == h, q, zero)
            halves.append(jnp.concatenate([q_h, neg], axis=1))
        qs = jnp.concatenate(halves, axis=0)
        _causal_pair_attention(qs, keys_at, vals_at, qi, o_ref.at[qi * tq:(qi + 1) * tq, :], tq)


def _fox_attention(main3, dcols, tq=512):
    b, t, _ = main3.shape
    cb = COL_FOX // LANES
    npair = FOX_HEADS // 2

    def seq(col):
        return pl.BlockSpec((None, t, LANES), lambda bi, p: (bi, 0, col(p)))

    return pl.pallas_call(
        functools.partial(_fox_kernel, tq=tq),
        grid=(b, npair),
        in_specs=[
            seq(lambda p: cb + p),
            seq(lambda p: cb + npair + p),
            seq(lambda p: cb + 2 * npair + p),
            seq(lambda p: 0),
        ],
        out_specs=pl.BlockSpec((None, t, LANES), lambda bi, p: (bi, 0, p)),
        out_shape=jax.ShapeDtypeStruct((b, t, FOX_WIDTH), BF16),
        name="fox_attention",
        compiler_params=_cparams(("parallel", "parallel")),
    )(main3, main3, main3, dcols)


def _mla_kernel(qn_ref, qr_ref, kn_ref, kr_ref, v_ref, o_ref, *, tq):
    p = pl.program_id(1)
    lane = _lane_iota()

    def keys_at(lo, hi):
        return jnp.concatenate([kn_ref[lo:hi, :], kr_ref[lo:hi, :]], axis=1)

    def vals_at(lo, hi):
        return v_ref[lo:hi, :]

    for qi in range(qn_ref.shape[0] // tq):
        qn = qn_ref[qi * tq:(qi + 1) * tq, :]
        qr = qr_ref[qi * tq:(qi + 1) * tq, :]
        zero = jnp.zeros_like(qn)
        halves = []
        for h in range(2):
            nope_h = jnp.where((lane // MLA_NOPE_DIM) == h, qn, zero)
            rope_h = jnp.where((lane // MLA_ROPE_DIM) == 2 * p + h, qr, zero)
            halves.append(jnp.concatenate([nope_h, rope_h], axis=1))
        qs = jnp.concatenate(halves, axis=0)
        _causal_pair_attention(qs, keys_at, vals_at, qi, o_ref.at[qi * tq:(qi + 1) * tq, :], tq)


def _mla_attention(qn, qr, kn, kr4, vv, tq=512):
    b, t, _ = qn.shape
    npair = MLA_HEADS // 2
    pair = pl.BlockSpec((None, t, LANES), lambda bi, p: (bi, 0, p))
    shared = pl.BlockSpec((None, t, LANES), lambda bi, p: (bi, 0, 0))
    return pl.pallas_call(
        functools.partial(_mla_kernel, tq=tq),
        grid=(b, npair),
        in_specs=[pair, shared, pair, shared, pair],
        out_specs=pair,
        out_shape=jax.ShapeDtypeStruct((b, t, MLA_WIDTH), BF16),
        name="mla_attention",
        compiler_params=_cparams(("parallel", "parallel")),
    )(qn, qr, kn, kr4, vv)


DIL_UNROLL = 8


def _dil_kernel(q_ref, k_ref, v_ref, o_ref, qf, kf, vf, qb, kb, vb, accb, mb, lb):
    t = q_ref.shape[0]
    blk = DIL_BLOCK
    lane = _lane_iota()
    head0 = lane < HEAD_DIM
    qf[...] = q_ref[...].astype(F32)
    kf[...] = k_ref[...].astype(F32)
    vf[...] = v_ref[...].astype(F32)

    for g, (_, rate) in enumerate(DIL_PAIRS):
        length = t // rate
        kb[g, 0:blk, :] = jnp.zeros((blk, LANES), BF16)
        vb[g, 0:blk, :] = jnp.zeros((blk, LANES), BF16)
        if rate == 1:
            qb[g, blk:, :] = q_ref[...]
            kb[g, blk:, :] = k_ref[...]
            vb[g, blk:, :] = v_ref[...]
            continue
        for res in range(rate):
            dst = slice(blk + res * length, blk + (res + 1) * length)
            src = pl.ds(res, length, stride=rate)
            qb[g, dst, :] = qf[src, :].astype(BF16)
            kb[g, dst, :] = kf[src, :].astype(BF16)
            vb[g, dst, :] = vf[src, :].astype(BF16)

    qi2 = lax.broadcasted_iota(jnp.int32, (blk, 2 * blk), 0)
    kj2 = lax.broadcasted_iota(jnp.int32, (blk, 2 * blk), 1)
    band = (kj2 >= qi2) & (kj2 <= qi2 + blk)
    bias_full = jnp.where(band, 0.0, NEG_INF)
    bias_first = jnp.where(band & (kj2 >= blk), 0.0, NEG_INF)

    for g, (window, rate) in enumerate(DIL_PAIRS):
        assert window // rate == blk
        nb = t // rate // blk
        nblocks = t // blk

        def body(j, carry, g=g, nb=nb):
            base = pl.multiple_of(j * blk, blk)
            q = qb[g, pl.ds(base + blk, blk), :]
            zero = jnp.zeros_like(q)
            qs = jnp.concatenate([jnp.where(head0, q, zero), jnp.where(head0, zero, q)], axis=0)
            keys = kb[g, pl.ds(base, 2 * blk), :]
            vals = vb[g, pl.ds(base, 2 * blk), :]
            bias = bias_first if nb == 1 else jnp.where((j % nb) != 0, bias_full, bias_first)
            s = _dot_nt(qs, keys)
            ps, ms = [], []
            for h in range(2):
                sh = s[h * blk:(h + 1) * blk] + bias
                m = jnp.max(sh, axis=1, keepdims=True)
                ps.append(jnp.exp((sh - m).astype(BF16)))
                ms.append(m)
            pv = _dot(jnp.concatenate(ps, axis=0),
                      jnp.concatenate([vals, jnp.ones_like(vals)], axis=1))
            rows = pl.ds(base, blk)
            accb[g, rows, :] = jnp.where(head0, pv[:blk, :LANES], pv[blk:, :LANES])
            mb[g, rows, :] = jnp.where(head0, ms[0], ms[1])
            lb[g, rows, :] = jnp.where(head0, pv[:blk, LANES:], pv[blk:, LANES:])
            return carry

        lax.fori_loop(0, nblocks, body, 0, unroll=DIL_UNROLL)

    rate_max = max(rate for _, rate in DIL_PAIRS)
    length = t // rate_max
    for res in range(rate_max):
        m_all, l_all, a_all = [], [], []
        for g, (_, rate) in enumerate(DIL_PAIRS):
            start = (res % rate) * (t // rate) + res // rate
            step = rate_max // rate
            rows = pl.ds(start, length) if step == 1 else pl.ds(start, length, stride=step)
            m_all.append(mb[g, rows, :])
            l_all.append(lb[g, rows, :])
            a_all.append(accb[g, rows, :])
        m_max = jnp.maximum(jnp.maximum(m_all[0], m_all[1]), m_all[2])
        ws = [jnp.exp(m - m_max) for m in m_all]
        num = ws[0] * a_all[0] + ws[1] * a_all[1] + ws[2] * a_all[2]
        den = ws[0] * l_all[0] + ws[1] * l_all[1] + ws[2] * l_all[2]
        o_ref[pl.ds(res, length, stride=rate_max), :] = num / den


def _dil_attention(main3):
    b, t, _ = main3.shape
    assert all(t % (rate * DIL_BLOCK) == 0 for _, rate in DIL_PAIRS)
    npair = DIL_HEADS // 2
    nbr = len(DIL_PAIRS)

    def col(start):
        return pl.BlockSpec((None, t, LANES), lambda bi, p: (bi, 0, start // LANES + p))

    return pl.pallas_call(
        _dil_kernel,
        grid=(b, npair),
        in_specs=[col(COL_DIL_QK), col(COL_DIL_QK + DIL_WIDTH), col(COL_DIL_V)],
        out_specs=pl.BlockSpec((None, t, LANES), lambda bi, p: (bi, 0, p)),
        out_shape=jax.ShapeDtypeStruct((b, t, DIL_WIDTH), F32),
        name="dilated_attention",
        scratch_shapes=[pltpu.VMEM((t, LANES), F32)] * 3
        + [pltpu.VMEM((nbr, t + DIL_BLOCK, LANES), BF16)] * 3
        + [pltpu.VMEM((nbr, t, LANES), F32)] * 3,
        compiler_params=_cparams(("parallel", "parallel")),
    )(main3, main3, main3)


def _merge_kernel(x_ref, gf_ref, gd_ref, gm_ref, of_ref, od_ref, om_ref,
                  wf_ref, wd_ref, wm_ref, wo_ref, o_ref):
    merged = (jax.nn.sigmoid(gf_ref[...].astype(F32)) * _dot(of_ref[...], wf_ref[...])
              + jax.nn.sigmoid(gd_ref[...].astype(F32)) * _dot(od_ref[...].astype(BF16), wd_ref[...])
              + jax.nn.sigmoid(gm_ref[...].astype(F32)) * _dot(om_ref[...], wm_ref[...]))
    o_ref[...] = x_ref[...] + _dot(merged.astype(BF16), wo_ref[...])


def _merge(x2, main2, o_fox, o_dil, o_mla, layer, wf, wd, wm, wo, tm=512):
    n = x2.shape[0]

    def tok(width, col=0):
        return pl.BlockSpec((tm, width), lambda i: (i, col))

    def full(w):
        return pl.BlockSpec((None,) + w.shape[1:], lambda i: (layer, 0, 0))

    return pl.pallas_call(
        _merge_kernel,
        grid=(n // tm,),
        in_specs=[tok(D_MODEL), tok(D_MODEL, 0), tok(D_MODEL, 1), tok(D_MODEL, 2),
                  tok(FOX_WIDTH), tok(DIL_WIDTH), tok(MLA_WIDTH),
                  full(wf), full(wd), full(wm), full(wo)],
        out_specs=tok(D_MODEL),
        out_shape=jax.ShapeDtypeStruct((n, D_MODEL), F32),
        name="merge_outproj",
        compiler_params=_cparams(("parallel",)),
    )(x2, main2, main2, main2, o_fox, o_dil, o_mla, wf, wd, wm, wo)


def _ffn_kernel(x_ref, g_ref, wg_ref, wu_ref, wd_ref, p_ref, pg_ref, wpg_ref, wpp_ref, o_ref):
    x = x_ref[...]
    h = _rms(x, g_ref[...]).astype(BF16)
    gate = _dot(h, wg_ref[...])
    up = _dot(h, wu_ref[...])
    act = gate * jax.nn.sigmoid(gate) * up
    x = x + _dot(act.astype(BF16), wd_ref[...])
    hp = _rms(x, pg_ref[...]).astype(BF16)
    gate_p = jax.nn.sigmoid(_dot(hp, wpg_ref[...]))
    o_ref[...] = x + gate_p * _dot(p_ref[...].astype(BF16), wpp_ref[...])


def _ffn_ple(x2, gain, j, wg, wu, wd, layer, p3, ple_gain, wpg, wpp, tm=256):
    n = x2.shape[0]
    tok = pl.BlockSpec((tm, D_MODEL), lambda i: (i, 0))
    vec = pl.BlockSpec((1, D_MODEL), lambda i: (0, 0))

    def of(index, w):
        return pl.BlockSpec((None,) + w.shape[1:], lambda i: (index, 0, 0))

    return pl.pallas_call(
        _ffn_kernel,
        grid=(n // tm,),
        in_specs=[tok, vec, of(j, wg), of(j, wu), of(j, wd),
                  pl.BlockSpec((None, tm, PLE_DIM), lambda i: (layer, i, 0)), vec,
                  of(layer, wpg), of(layer, wpp)],
        out_specs=tok,
        out_shape=jax.ShapeDtypeStruct((n, D_MODEL), F32),
        name="ffn_dense_ple",
        compiler_params=pltpu.CompilerParams(dimension_semantics=("parallel",),
                                             vmem_limit_bytes=VMEM_LIMIT_LARGE),
    )(x2, gain, wg, wu, wd, p3, ple_gain, wpg, wpp)


MOE_BLOCK = 2048
MOE_TILE = 256
MOE_TILES = -(-(2 * MOE_BLOCK + N_EXPERTS * (MOE_TILE - 1)) // MOE_TILE)
MOE_META_ROWS = 32
assert MOE_TILES <= MOE_META_ROWS
MOE_CHUNK = 256
MOE_CHUNKS = MOE_BLOCK // MOE_CHUNK
MOE_WINDOW = 5


def _router_kernel(x_ref, g_ref, wc_ref, h_ref, rc_ref, rr_ref, meta_ref):
    h = _rms(x_ref[...], g_ref[...])
    h_ref[...] = h.astype(BF16)
    h_hi, h_mid, h_lo = _split3(h)
    a = _dot(h_hi, wc_ref[...])
    b = _dot(h_mid, wc_ref[:, :2 * LANES])
    c = _dot(h_lo, wc_ref[:, :LANES])
    logits = (a[:, :LANES] + (a[:, LANES:2 * LANES] + b[:, :LANES])
              + (a[:, 2 * LANES:] + b[:, LANES:] + c))
    lane = _lane_iota()
    is_expert = lane < N_EXPERTS
    logits = jnp.where(is_expert, logits, NEG_INF)
    v1 = jnp.max(logits, axis=1, keepdims=True)
    i1 = jnp.min(jnp.where(logits == v1, lane, LANES), axis=1, keepdims=True)
    first = lane == i1
    rest = jnp.where(first, NEG_INF, logits)
    v2 = jnp.max(rest, axis=1, keepdims=True)
    i2 = jnp.min(jnp.where(rest == v2, lane, LANES), axis=1, keepdims=True)
    second = lane == i2
    e2 = jnp.exp(v2 - v1)
    w1 = 1.0 / (1.0 + e2)
    w2 = e2 / (1.0 + e2)

    sel = jnp.where(first, 1.0, jnp.where(second, 1.0, 0.0))
    row = lax.broadcasted_iota(jnp.int32, (CUMSUM_BLOCK, CUMSUM_BLOCK), 0)
    col = lax.broadcasted_iota(jnp.int32, (CUMSUM_BLOCK, CUMSUM_BLOCK), 1)
    tri = jnp.where(col < row, 1.0, 0.0).astype(BF16)
    carry = jnp.zeros((1, LANES), F32)
    ranks = []
    for blk in range(sel.shape[0] // CUMSUM_BLOCK):
        part = sel[blk * CUMSUM_BLOCK:(blk + 1) * CUMSUM_BLOCK]
        ranks.append(_dot(tri, part.astype(BF16)) + carry)
        carry = carry + jnp.sum(part, axis=0, keepdims=True)
    rank = jnp.concatenate(ranks, axis=0)
    padded = jnp.ceil(carry / MOE_TILE) * MOE_TILE
    er = lax.broadcasted_iota(jnp.int32, (LANES, LANES), 0)
    ec = lax.broadcasted_iota(jnp.int32, (LANES, LANES), 1)
    before = jnp.where(er < ec, 1.0, 0.0).astype(BF16)
    start = _dot(jnp.broadcast_to(padded, (8, LANES)).astype(BF16), before)[0:1]
    slot = start + rank
    dest1 = jnp.sum(jnp.where(first, slot, 0.0), axis=1, keepdims=True)
    dest2 = jnp.sum(jnp.where(second, slot, 0.0), axis=1, keepdims=True)
    routing = jnp.where(lane == 0, dest1, jnp.where(lane == 1, dest2,
                        jnp.where(lane == 2, w1, jnp.where(lane == 3, w2, 0.0))))
    rc_ref[...] = routing
    rrow = routing.T[0:8]
    for c in range(MOE_BLOCK // MOE_CHUNK):
        rr_ref[c] = rrow[:, c * MOE_CHUNK:(c + 1) * MOE_CHUNK]

    end = start + padded
    tile_row = lax.broadcasted_iota(jnp.int32, (MOE_META_ROWS, LANES), 0).astype(F32) * MOE_TILE
    passed = jnp.sum(jnp.where(is_expert, jnp.where(tile_row >= end, 1.0, 0.0), 0.0),
                     axis=1, keepdims=True)
    last = jnp.max(jnp.where(is_expert, jnp.where(padded > 0, lane.astype(F32), 0.0), 0.0),
                   axis=1, keepdims=True)
    total = jnp.sum(jnp.where(is_expert, padded, 0.0), axis=1, keepdims=True)
    active = jnp.where(tile_row < total, 1.0, 0.0)
    nblock = rrow.shape[1]
    tok = lax.broadcasted_iota(jnp.int32, (MOE_META_ROWS, nblock), 1).astype(F32)
    tile_id = lax.broadcasted_iota(jnp.int32, (MOE_META_ROWS, nblock), 0).astype(F32)
    in_tile = jnp.where(jnp.floor(rrow[0:1] / MOE_TILE) == tile_id, 1.0,
                        jnp.where(jnp.floor(rrow[1:2] / MOE_TILE) == tile_id, 1.0, 0.0))
    first_tok = jnp.min(jnp.where(in_tile > 0, tok, float(nblock)), axis=1, keepdims=True)
    last_tok = jnp.max(jnp.where(in_tile > 0, tok, -1.0), axis=1, keepdims=True)
    chunk_lo = jnp.floor(first_tok / MOE_CHUNK)
    chunk_hi = jnp.floor(last_tok / MOE_CHUNK)
    meta = jnp.where(lane == 0, jnp.minimum(passed, last),
                     jnp.where(lane == 1, active,
                               jnp.where(lane == 2, chunk_lo, jnp.where(lane == 3, chunk_hi, 0.0))))
    meta_ref[...] = meta.astype(jnp.int32)


def _router(x2, gain, w_router):
    n = x2.shape[0]
    nblk = n // MOE_BLOCK
    w_pad = jnp.zeros((D_MODEL, LANES), F32).at[:, :N_EXPERTS].set(w_router)
    w_hi = w_pad.astype(BF16)
    r1 = w_pad - w_hi.astype(F32)
    w_mid = r1.astype(BF16)
    w_lo = (r1 - w_mid.astype(F32)).astype(BF16)
    w_cat = jnp.concatenate([w_hi, w_mid, w_lo], axis=1)
    return pl.pallas_call(
        _router_kernel,
        grid=(nblk,),
        in_specs=[pl.BlockSpec((MOE_BLOCK, D_MODEL), lambda i: (i, 0)),
                  pl.BlockSpec((1, D_MODEL), lambda i: (0, 0)),
                  pl.BlockSpec(w_cat.shape, lambda i: (0, 0))],
        out_specs=[pl.BlockSpec((MOE_BLOCK, D_MODEL), lambda i: (i, 0)),
                   pl.BlockSpec((MOE_BLOCK, LANES), lambda i: (i, 0)),
                   pl.BlockSpec((None, MOE_CHUNKS, 8, MOE_CHUNK), lambda i: (i, 0, 0, 0)),
                   pl.BlockSpec((None, MOE_META_ROWS, LANES), lambda i: (i, 0, 0))],
        out_shape=[jax.ShapeDtypeStruct((n, D_MODEL), BF16),
                   jax.ShapeDtypeStruct((n, LANES), F32),
                   jax.ShapeDtypeStruct((nblk, MOE_CHUNKS, 8, MOE_CHUNK), F32),
                   jax.ShapeDtypeStruct((nblk, MOE_META_ROWS, LANES), jnp.int32)],
        name="router",
        compiler_params=_cparams(("parallel",)),
    )(x2, gain, w_cat)


def _moe_kernel(te_ref, ta_ref, lo_ref, hi_ref, h_ref, rrow_ref, rcol_ref, wgu_ref, wd_ref, o_ref):
    b = pl.program_id(0)
    r = pl.program_id(1)

    @pl.when(r == 0)
    def _():
        o_ref[...] = jnp.zeros(o_ref.shape, F32)

    def tile_work(first_chunk, nchunks):
        base = (r * MOE_TILE).astype(F32)
        slot_col = lax.broadcasted_iota(jnp.int32, (MOE_TILE, 1), 0).astype(F32) + base
        slot_row = lax.broadcasted_iota(jnp.int32, (1, MOE_TILE), 1).astype(F32) + base
        routing = rrow_ref[pl.ds(first_chunk, nchunks)]
        onehots = []
        weight = jnp.zeros((MOE_TILE, 1), F32)
        for c in range(nchunks):
            hit1 = routing[c, 0:1, :] == slot_col
            hit2 = routing[c, 1:2, :] == slot_col
            onehots.append(jnp.where(hit1, 1.0, jnp.where(hit2, 1.0, 0.0)).astype(BF16))
            weight = weight + jnp.sum(
                jnp.where(hit1, routing[c, 2:3, :], jnp.where(hit2, routing[c, 3:4, :], 0.0)),
                axis=1, keepdims=True)
        toks = pl.ds(pl.multiple_of(first_chunk * MOE_CHUNK, MOE_CHUNK), nchunks * MOE_CHUNK)
        xg = _dot(jnp.concatenate(onehots, axis=1), h_ref[toks, :]).astype(BF16)
        gate_up = _dot(xg, wgu_ref[...])
        gate = gate_up[:, :D_FF_EXPERT]
        up = gate_up[:, D_FF_EXPERT:]
        act = gate * jax.nn.sigmoid(gate) * up
        y = (_dot(act.astype(BF16), wd_ref[...]) * weight).astype(BF16)
        back = jnp.where(rcol_ref[toks, 0:1] == slot_row, 1.0,
                         jnp.where(rcol_ref[toks, 1:2] == slot_row, 1.0, 0.0)).astype(BF16)
        o_ref[toks, :] += _dot(back, y)

    active = ta_ref[b, r] == 1
    first = jnp.minimum(lo_ref[b, r], MOE_CHUNKS - MOE_WINDOW)
    short = hi_ref[b, r] < first + MOE_WINDOW

    @pl.when(active & short)
    def _():
        tile_work(first, MOE_WINDOW)

    @pl.when(active & jnp.logical_not(short))
    def _():
        tile_work(0, MOE_CHUNKS)


def _moe(h2, rrow, rcol, meta, j, wgu, wd):
    n = h2.shape[0]
    nblk = n // MOE_BLOCK
    tile_tables = [meta[:, :MOE_TILES, k] for k in range(4)]

    def blocked(shape, index):
        return pl.BlockSpec(shape, lambda b, r, te, ta, lo, hi: index(b))

    def wspec(shape):
        return pl.BlockSpec((None,) + shape,
                            lambda b, r, te, ta, lo, hi: (j * N_EXPERTS + te[b, r], 0, 0))

    grid_spec = pltpu.PrefetchScalarGridSpec(
        num_scalar_prefetch=len(tile_tables),
        grid=(nblk, MOE_TILES),
        in_specs=[blocked((MOE_BLOCK, D_MODEL), lambda b: (b, 0)),
                  blocked((None, MOE_CHUNKS, 8, MOE_CHUNK), lambda b: (b, 0, 0, 0)),
                  blocked((MOE_BLOCK, LANES), lambda b: (b, 0)),
                  wspec((D_MODEL, 2 * D_FF_EXPERT)), wspec((D_FF_EXPERT, D_MODEL))],
        out_specs=blocked((MOE_BLOCK, D_MODEL), lambda b: (b, 0)),
    )
    return pl.pallas_call(
        _moe_kernel,
        grid_spec=grid_spec,
        out_shape=jax.ShapeDtypeStruct((n, D_MODEL), F32),
        name="moe_experts",
        compiler_params=pltpu.CompilerParams(dimension_semantics=("parallel", "arbitrary"),
                                             vmem_limit_bytes=VMEM_LIMIT_LARGE),
    )(*tile_tables, h2, rrow, rcol, wgu, wd)


def _moe_layer(x2, gain, w_router, j, wgu, wd):
    h2, rcol, rrow, meta = _router(x2, gain, w_router)
    return _moe(h2, rrow, rcol, meta, j, wgu, wd)


def _ple_kernel(x_ref, d_ref, p_ref, g_ref, wg_ref, wp_ref, fg_ref, o_ref, *, final):
    x = x_ref[...] + d_ref[...]
    h = _rms(x, g_ref[...]).astype(BF16)
    gate = jax.nn.sigmoid(_dot(h, wg_ref[...]))
    y = x + gate * _dot(p_ref[...].astype(BF16), wp_ref[...])
    if final:
        y = _rms(y, fg_ref[...])
    o_ref[...] = y


def _ple(x2, delta, layer, p3, gain, wg, wp, final_gain, final, tm=1024):
    n = x2.shape[0]
    vec = pl.BlockSpec((1, D_MODEL), lambda i: (0, 0))
    tok = pl.BlockSpec((tm, D_MODEL), lambda i: (i, 0))
    return pl.pallas_call(
        functools.partial(_ple_kernel, final=final),
        grid=(n // tm,),
        in_specs=[tok, tok,
                  pl.BlockSpec((None, tm, PLE_DIM), lambda i: (layer, i, 0)), vec,
                  pl.BlockSpec((None,) + wg.shape[1:], lambda i: (layer, 0, 0)),
                  pl.BlockSpec((None,) + wp.shape[1:], lambda i: (layer, 0, 0)), vec],
        out_specs=pl.BlockSpec((tm, D_MODEL), lambda i: (i, 0)),
        out_shape=jax.ShapeDtypeStruct((n, D_MODEL), F32),
        name="ple_final" if final else "ple",
        compiler_params=_cparams(("parallel",)),
    )(x2, delta, p3, gain, wg, wp, final_gain)


def _cast_kernel(x_ref, o_ref):
    o_ref[...] = x_ref[...].astype(BF16)


def _to_bf16(a):
    r, c = a.shape[-2:]
    a3 = a.reshape((-1, r, c))
    spec = pl.BlockSpec((None, r, c), lambda i: (i, 0, 0))
    out = pl.pallas_call(
        _cast_kernel,
        grid=(a3.shape[0],),
        in_specs=[spec],
        out_specs=spec,
        out_shape=jax.ShapeDtypeStruct(a3.shape, BF16),
        name="cast_bf16",
        compiler_params=_cparams(("parallel",)),
    )(a3)
    return out.reshape(a.shape)


def _cast_pair_kernel(a_ref, b_ref, o_ref):
    width = a_ref.shape[-1]
    o_ref[:, :width] = a_ref[...].astype(BF16)
    o_ref[:, width:] = b_ref[...].astype(BF16)


def _to_bf16_side_by_side(a, b):
    r, c = a.shape[-2:]
    a3 = a.reshape((-1, r, c))
    b3 = b.reshape((-1, r, c))
    spec = pl.BlockSpec((None, r, c), lambda i: (i, 0, 0))
    return pl.pallas_call(
        _cast_pair_kernel,
        grid=(a3.shape[0],),
        in_specs=[spec, spec],
        out_specs=pl.BlockSpec((None, r, 2 * c), lambda i: (i, 0, 0)),
        out_shape=jax.ShapeDtypeStruct((a3.shape[0], r, 2 * c), BF16),
        name="cast_pair_bf16",
        compiler_params=_cparams(("parallel",)),
    )(a3, b3)


IN_SIZES = (FOX_WIDTH, FOX_WIDTH, FOX_WIDTH, FOX_HEADS, DIL_WIDTH, DIL_WIDTH, DIL_WIDTH,
            MLA_Q_RANK, MLA_KV_RANK, MLA_ROPE_DIM, GATE_WIDTH)
IN_WIDTH = sum(IN_SIZES)
IN_PREP_COLS = 256


def _inproj_weight_kernel(w_ref, main_ref, small_ref):
    offs = [0]
    for s in IN_SIZES:
        offs.append(offs[-1] + s)
    fq, fk, fv, fz, dq, dk, dv, cq, ckv, kr, gates = [(offs[i], offs[i + 1])
                                                      for i in range(len(IN_SIZES))]
    scale = HEAD_DIM ** -0.5
    row = 0
    for (lo, hi), mult in ((gates, 1.0), (dq, scale), (dk, 1.0), (fq, scale), (fk, 1.0), (fv, 1.0),
                           (dv, 1.0), (cq, 1.0), (ckv, 1.0)):
        main_ref[row:row + hi - lo, :] = (w_ref[lo:hi, :] * mult).astype(BF16)
        row += hi - lo
    small_ref[...] = jnp.zeros(small_ref.shape, BF16)
    small_ref[0:FOX_HEADS, :] = w_ref[fz[0]:fz[1], :].astype(BF16)
    for h in range(MLA_HEADS):
        r0 = LANES + h * MLA_ROPE_DIM
        small_ref[r0:r0 + MLA_ROPE_DIM, :] = w_ref[kr[0]:kr[1], :].astype(BF16)


def _prep_in_weights(w_in):
    depth = w_in.shape[0]
    cb = IN_PREP_COLS
    w_t = jnp.swapaxes(w_in, 1, 2)

    def blk(rows):
        return pl.BlockSpec((None, rows, cb), lambda l, i: (l, 0, i))

    return pl.pallas_call(
        _inproj_weight_kernel,
        grid=(depth, D_MODEL // cb),
        in_specs=[blk(IN_WIDTH)],
        out_specs=[blk(MAIN_WIDTH), blk(SMALL_WIDTH)],
        out_shape=[jax.ShapeDtypeStruct((depth, MAIN_WIDTH, D_MODEL), BF16),
                   jax.ShapeDtypeStruct((depth, SMALL_WIDTH, D_MODEL), BF16)],
        name="inproj_weights",
        compiler_params=_cparams(("parallel", "parallel")),
    )(w_t)


def _prep_mla_weights(w_uq, w_ukv):
    depth = w_uq.shape[0]
    uq = w_uq.reshape(depth, MLA_Q_RANK, MLA_HEADS, MLA_NOPE_DIM + MLA_ROPE_DIM)
    wq = jnp.concatenate([uq[..., :MLA_NOPE_DIM].reshape(depth, MLA_Q_RANK, -1),
                          uq[..., MLA_NOPE_DIM:].reshape(depth, MLA_Q_RANK, -1)], axis=-1)
    ukv = w_ukv.reshape(depth, MLA_KV_RANK, MLA_HEADS, MLA_NOPE_DIM + MLA_V_DIM)
    wkv = jnp.concatenate([ukv[..., :MLA_NOPE_DIM].reshape(depth, MLA_KV_RANK, -1),
                           ukv[..., MLA_NOPE_DIM:].reshape(depth, MLA_KV_RANK, -1)], axis=-1)
    return wq.astype(BF16), wkv.astype(BF16)


def kernel(x, p, positions, mix_norm, w_in, b_forget, mla_q_norm, mla_kv_norm, w_uq, w_ukv,
           w_br_fox, w_br_dil, w_br_mla, w_out, ffn_norm, w_ffn_gate, w_ffn_up, w_ffn_down,
           w_router, w_exp_gate, w_exp_up, w_exp_down, ple_norm, w_ple_gate, w_ple_proj,
           final_norm):
    b, t, _ = x.shape
    n = b * t
    depth = w_in.shape[0]
    assert depth % 2 == 0

    w_main, w_small = _prep_in_weights(w_in)
    wq_all, wkv_all = _prep_mla_weights(w_uq, w_ukv)
    (w_br_fox, w_br_dil, w_br_mla, w_out, w_ffn_gate, w_ffn_up, w_ffn_down, w_exp_down, w_ple_gate,
     w_ple_proj) = [
        _to_bf16(w) for w in (w_br_fox, w_br_dil, w_br_mla, w_out, w_ffn_gate, w_ffn_up, w_ffn_down,
                              w_exp_down, w_ple_gate, w_ple_proj)]
    w_exp_gate_up = _to_bf16_side_by_side(w_exp_gate, w_exp_up)
    w_exp_down = w_exp_down.reshape((-1,) + w_exp_down.shape[2:])
    p3 = p.reshape(depth, n, PLE_DIM)
    bias_all = jnp.concatenate(
        [b_forget.astype(F32), jnp.zeros((depth, LANES - FOX_HEADS), F32)], axis=-1)
    tabs = [tab.reshape(n, LANES) for tab in _rope_tables(positions)]
    final_gain = final_norm.reshape(1, D_MODEL)

    x2 = x.reshape(n, D_MODEL)
    for i in range(depth):
        main2, fz, qn, qr, kn, kr4, vv = _inproj(
            x2, mix_norm[i].reshape(1, D_MODEL), i, w_main, w_small, tabs,
            mla_q_norm[i].reshape(1, -1), mla_kv_norm[i].reshape(1, -1), wq_all[i], wkv_all[i])
        main3 = main2.reshape(b, t, COL_MLA)

        dcols = _decay(fz.reshape(b, t, LANES), bias_all[i].reshape(1, LANES))
        o_fox = _fox_attention(main3, dcols)
        o_dil = _dil_attention(main3)
        o_mla = _mla_attention(*[a.reshape(b, t, -1) for a in (qn, qr, kn, kr4, vv)])

        x2 = _merge(x2, main2, o_fox.reshape(n, -1), o_dil.reshape(n, -1), o_mla.reshape(n, -1),
                    i, w_br_fox, w_br_dil, w_br_mla, w_out)

        j = i // 2
        gain = ffn_norm[i].reshape(1, D_MODEL)
        ple_args = (i, p3, ple_norm[i].reshape(1, D_MODEL), w_ple_gate, w_ple_proj)
        if i % 2 == 0:
            x2 = _ffn_ple(x2, gain, j, w_ffn_gate, w_ffn_up, w_ffn_down, *ple_args)
        else:
            delta = _moe_layer(x2, gain, w_router[j], j, w_exp_gate_up, w_exp_down)
            x2 = _ple(x2, delta, *ple_args, final_gain, final=(i == depth - 1))
    return x2.reshape(b, t, D_MODEL)
```
